```python
import math
import jax, jax.numpy as jnp
from jax import lax
import numpy as np

D_MODEL = 2048
BATCH = 8
SEQ = 4096
DEPTH = 1
DEC_BATCH = 16
DEC_SEQ = 32
PAST_LEN = 4096

CHUNK = 64
N_PREV_CHUNKS = 8
BAND = (N_PREV_CHUNKS + 1) * CHUNK
HEAD_DIM = 128
N_HEADS_FOX = 8
N_HEADS_BAND = 8
D_FOX = N_HEADS_FOX * HEAD_DIM
D_BAND = N_HEADS_BAND * HEAD_DIM
D_FF = ((8 * D_MODEL + 3 * 256 - 1) // (3 * 256)) * 256
REL_CLIP = 128
Q_BLOCK = 128
FORGET_BIAS = 3.0
RMS_EPS = 1e-6
NEG_INF = -1e30
N_IN = 3 * D_FOX + N_HEADS_FOX + 3 * D_BAND + 2 * D_MODEL

kernel_name = "fox_chunkband_hybrid_stream_step"


def _rmsnorm(x, g):
    xf = x.astype(jnp.float32)
    y = xf * lax.rsqrt(jnp.mean(xf * xf, axis=-1, keepdims=True) + RMS_EPS)
    return (y * g.astype(jnp.float32)).astype(x.dtype)


def _mod_norm(x, g, shift, scale):
    return _rmsnorm(x, g) * (1.0 + scale[:, None, :]) + shift[:, None, :]


def _ada(c, w_ada, b_ada):
    m = jax.nn.silu(c) @ w_ada + b_ada
    return jnp.split(m, 6, axis=-1)


def _mixer_in(x, shift, scale, g, w_in, b_f):
    B, S, _ = x.shape
    h = _mod_norm(x, g, shift, scale)
    z = h @ w_in
    sizes = [D_FOX, D_FOX, D_FOX, N_HEADS_FOX, D_BAND, D_BAND, D_BAND, D_MODEL]
    cuts = [int(v) for v in np.cumsum(sizes)]
    qa, ka, va, fa, qb, kb, vb, za, zb = jnp.split(z, cuts, axis=-1)
    logf = jax.nn.log_sigmoid((fa + b_f).astype(jnp.float32))
    ha = lambda t: t.reshape(B, S, N_HEADS_FOX, HEAD_DIM)
    hb = lambda t: t.reshape(B, S, N_HEADS_BAND, HEAD_DIM)
    return ha(qa), ha(ka), ha(va), logf, hb(qb), hb(kb), hb(vb), za, zb


def _fox_prompt(q, k, v, logf):
    B, S, H, Dh = q.shape
    nb = S // Q_BLOCK
    Ft = jnp.cumsum(logf, axis=1).transpose(0, 2, 1)
    qb = q.reshape(B, nb, Q_BLOCK, H, Dh).transpose(1, 0, 2, 3, 4)
    Fq = Ft.reshape(B, H, nb, Q_BLOCK).transpose(2, 0, 1, 3)
    kpos = jnp.arange(S)
    inv = 1.0 / math.sqrt(Dh)

    def one_block(args):
        i, qi, fi = args
        s = jnp.einsum('bqhd,bkhd->bhqk', qi, k, preferred_element_type=jnp.float32) * inv
        s = s + fi[..., None] - Ft[:, :, None, :]
        qpos = i * Q_BLOCK + jnp.arange(Q_BLOCK)
        s = jnp.where(kpos[None, :] <= qpos[:, None], s, NEG_INF)
        p = jax.nn.softmax(s, axis=-1)
        return jnp.einsum('bhqk,bkhd->bqhd', p.astype(v.dtype), v)

    o = lax.map(one_block, (jnp.arange(nb), qb, Fq))
    return o.transpose(1, 0, 2, 3, 4).reshape(B, S, H * Dh)


def _fox_sample(q, k, v, logf, ck, cv, clogf):
    B, T, H, Dh = q.shape
    P = ck.shape[1]
    k_all = jnp.concatenate([ck, k], axis=1)
    v_all = jnp.concatenate([cv, v], axis=1)
    Ft = jnp.cumsum(jnp.concatenate([clogf.astype(jnp.float32), logf], axis=1), axis=1).transpose(0, 2, 1)
    s = jnp.einsum('bqhd,bkhd->bhqk', q, k_all, preferred_element_type=jnp.float32) / math.sqrt(Dh)
    s = s + Ft[:, :, P:, None] - Ft[:, :, None, :]
    mask = jnp.arange(P + T)[None, :] <= (P + jnp.arange(T))[:, None]
    s = jnp.where(mask, s, NEG_INF)
    p = jax.nn.softmax(s, axis=-1)
    return jnp.einsum('bhqk,bkhd->bqhd', p.astype(v_all.dtype), v_all).reshape(B, T, H * Dh)


def _rel_bias(table, rel):
    return table[jnp.clip(rel, -REL_CLIP, REL_CLIP) + REL_CLIP].transpose(2, 0, 1).astype(jnp.float32)


def _band_prompt(q, k, v, table):
    B, S, H, Dh = q.shape
    nc = S // CHUNK
    pad = N_PREV_CHUNKS * CHUNK
    kp = jnp.pad(k, ((0, 0), (pad, 0), (0, 0), (0, 0)))
    vp = jnp.pad(v, ((0, 0), (pad, 0), (0, 0), (0, 0)))
    qc = q.reshape(B, nc, CHUNK, H, Dh).transpose(1, 0, 2, 3, 4)
    rel = pad + jnp.arange(CHUNK)[:, None] - jnp.arange(BAND)[None, :]
    bias = _rel_bias(table, rel)
    inv = 1.0 / math.sqrt(Dh)

    def one_chunk(args):
        n, qn = args
        kn = lax.dynamic_slice_in_dim(kp, n * CHUNK, BAND, axis=1)
        vn = lax.dynamic_slice_in_dim(vp, n * CHUNK, BAND, axis=1)
        s = jnp.einsum('bqhd,bkhd->bhqk', qn, kn, preferred_element_type=jnp.float32) * inv + bias
        valid = (n - N_PREV_CHUNKS) * CHUNK + jnp.arange(BAND) >= 0
        s = jnp.where(valid[None, None, None, :], s, NEG_INF)
        p = jax.nn.softmax(s, axis=-1)
        return jnp.einsum('bhqk,bkhd->bqhd', p.astype(vn.dtype), vn)

    o = lax.map(one_chunk, (jnp.arange(nc), qc))
    return o.transpose(1, 0, 2, 3, 4).reshape(B, S, H * Dh)


def _band_sample(q, k, v, ck, cv, table):
    B, T, H, Dh = q.shape
    lb = ck.shape[1]
    k_all = jnp.concatenate([ck, k], axis=1)
    v_all = jnp.concatenate([cv, v], axis=1)
    rel = lb + jnp.arange(T)[:, None] - jnp.arange(lb + T)[None, :]
    s = jnp.einsum('bqhd,bkhd->bhqk', q, k_all, preferred_element_type=jnp.float32) / math.sqrt(Dh)
    s = s + _rel_bias(table, rel)
    p = jax.nn.softmax(s, axis=-1)
    return jnp.einsum('bhqk,bkhd->bqhd', p.astype(v_all.dtype), v_all).reshape(B, T, H * Dh)


def _merge(oa, ob, za, zb, w_oa, w_ob, w_out):
    m = jax.nn.sigmoid(za) * (oa @ w_oa) + jax.nn.sigmoid(zb) * (ob @ w_ob)
    return m @ w_out


def _swiglu(h, w_gate, w_up, w_down):
    return (jax.nn.silu(h @ w_gate) * (h @ w_up)) @ w_down


def _layer(x, c, attend, w_ada, b_ada, g_mix, w_in, b_f, w_oa, w_ob, w_out, g_ffn, w_gate, w_up, w_down):
    sh1, sc1, gt1, sh2, sc2, gt2 = _ada(c, w_ada, b_ada)
    qa, ka, va, lf, qb, kb, vb, za, zb = _mixer_in(x, sh1, sc1, g_mix, w_in, b_f)
    oa, ob = attend(qa, ka, va, lf, qb, kb, vb)
    x = x + gt1[:, None, :] * _merge(oa, ob, za, zb, w_oa, w_ob, w_out)
    x = x + gt2[:, None, :] * _swiglu(_mod_norm(x, g_ffn, sh2, sc2), w_gate, w_up, w_down)
    return x, ka, va, lf, kb, vb


def setup_inputs(seed: int = 0) -> dict:
    key = jax.random.key(seed)
    ks = jax.random.split(key, 24)
    f32 = jnp.float32
    nrm = lambda k, shape, s: s * jax.random.normal(k, shape, f32)
    lb = min(N_PREV_CHUNKS * CHUNK, PAST_LEN)
    return {
        "x_prompt": nrm(ks[0], (BATCH, SEQ, D_MODEL), 1.0),
        "x_sample": nrm(ks[1], (DEC_BATCH, DEC_SEQ, D_MODEL), 1.0),
        "cache_fox_k": nrm(ks[2], (DEPTH, DEC_BATCH, PAST_LEN, N_HEADS_FOX, HEAD_DIM), 1.0),
        "cache_fox_v": nrm(ks[3], (DEPTH, DEC_BATCH, PAST_LEN, N_HEADS_FOX, HEAD_DIM), 1.0),
        "cache_fox_logf": jax.nn.log_sigmoid(FORGET_BIAS + nrm(ks[4], (DEPTH, DEC_BATCH, PAST_LEN, N_HEADS_FOX), 1.0)),
        "cache_band_k": nrm(ks[5], (DEPTH, DEC_BATCH, lb, N_HEADS_BAND, HEAD_DIM), 1.0),
        "cache_band_v": nrm(ks[6], (DEPTH, DEC_BATCH, lb, N_HEADS_BAND, HEAD_DIM), 1.0),
        "c_prompt": nrm(ks[7], (BATCH, D_MODEL), 1.0),
        "c_sample": nrm(ks[8], (DEC_BATCH, D_MODEL), 1.0),
        "w_ada": nrm(ks[9], (DEPTH, D_MODEL, 6 * D_MODEL), 0.5 * D_MODEL ** -0.5),
        "b_ada": nrm(ks[10], (DEPTH, 6 * D_MODEL), 0.01),
        "g_mix": 1.0 + nrm(ks[11], (DEPTH, D_MODEL), 0.05),
        "w_in": nrm(ks[12], (DEPTH, D_MODEL, N_IN), D_MODEL ** -0.5),
        "b_f": FORGET_BIAS + nrm(ks[13], (DEPTH, N_HEADS_FOX), 0.1),
        "rel_bias": nrm(ks[14], (DEPTH, 2 * REL_CLIP + 1, N_HEADS_BAND), 0.1),
        "w_oa": nrm(ks[15], (DEPTH, D_FOX, D_MODEL), D_FOX ** -0.5),
        "w_ob": nrm(ks[16], (DEPTH, D_BAND, D_MODEL), D_BAND ** -0.5),
        "w_out": nrm(ks[17], (DEPTH, D_MODEL, D_MODEL), D_MODEL ** -0.5),
        "g_ffn": 1.0 + nrm(ks[18], (DEPTH, D_MODEL), 0.05),
        "w_gate": nrm(ks[19], (DEPTH, D_MODEL, D_FF), D_MODEL ** -0.5),
        "w_up": nrm(ks[20], (DEPTH, D_MODEL, D_FF), D_MODEL ** -0.5),
        "w_down": nrm(ks[21], (DEPTH, D_FF, D_MODEL), D_FF ** -0.5),
        "g_final": 1.0 + nrm(ks[22], (D_MODEL,), 0.05),
    }


def reference(x_prompt, x_sample, cache_fox_k, cache_fox_v, cache_fox_logf, cache_band_k, cache_band_v,
              c_prompt, c_sample, w_ada, b_ada, g_mix, w_in, b_f, rel_bias, w_oa, w_ob, w_out,
              g_ffn, w_gate, w_up, w_down, g_final):
    xp, xs = x_prompt, x_sample
    fk_p, fv_p, fl_p, bk_p, bv_p = [], [], [], [], []
    fk_s, fv_s, fl_s, bk_s, bv_s = [], [], [], [], []
    for l in range(DEPTH):
        params = (w_ada[l], b_ada[l], g_mix[l], w_in[l], b_f[l], w_oa[l], w_ob[l], w_out[l],
                  g_ffn[l], w_gate[l], w_up[l], w_down[l])
        tbl = rel_bias[l]
        attend_p = lambda qa, ka, va, lf, qb, kb, vb: (
            _fox_prompt(qa, ka, va, lf), _band_prompt(qb, kb, vb, tbl))
        xp, ka, va, lf, kb, vb = _layer(xp, c_prompt, attend_p, *params)
        lbp = min(N_PREV_CHUNKS * CHUNK, kb.shape[1])
        fk_p.append(ka); fv_p.append(va); fl_p.append(lf)
        bk_p.append(kb[:, -lbp:]); bv_p.append(vb[:, -lbp:])

        ck, cv, cl, cbk, cbv = cache_fox_k[l], cache_fox_v[l], cache_fox_logf[l], cache_band_k[l], cache_band_v[l]
        attend_s = lambda qa, ka, va, lf, qb, kb, vb: (
            _fox_sample(qa, ka, va, lf, ck, cv, cl), _band_sample(qb, kb, vb, cbk, cbv, tbl))
        xs, ka, va, lf, kb, vb = _layer(xs, c_sample, attend_s, *params)
        fk_s.append(ka); fv_s.append(va); fl_s.append(lf)
        bk_s.append(kb); bv_s.append(vb)

    y_prompt = _rmsnorm(xp, g_final)
    y_sample = _rmsnorm(xs, g_final)
    return (y_prompt, y_sample,
            jnp.stack(fk_p), jnp.stack(fv_p), jnp.stack(fl_p), jnp.stack(bk_p), jnp.stack(bv_p),
            jnp.stack(fk_s), jnp.stack(fv_s), jnp.stack(fl_s), jnp.stack(bk_s), jnp.stack(bv_s))
```

```python
import functools
import math

import jax
import jax.numpy as jnp
from jax import lax
from jax.experimental import pallas as pl
from jax.experimental.pallas import tpu as pltpu

F32 = jnp.float32
BF16 = jnp.bfloat16

HEAD_DIM = 128
N_HEADS = 8
D_HEADS = N_HEADS * HEAD_DIM
CHUNK = 64
N_PREV_CHUNKS = 8
BAND_PAD = N_PREV_CHUNKS * CHUNK
REL_CLIP = 128
RMS_EPS = 1e-6
NEG_INF = -1e30
LANES = 128
Z_COLS = 6 * D_HEADS

VMEM_CAP_BYTES = 60 * 1024 * 1024
MIB = 1024 * 1024

BAND_TQ = 128
BAND_W = BAND_PAD + BAND_TQ
FOX_T = 256


def _vmem_limit(pipelined_bytes, resident_bytes=0):
    est = 2 * pipelined_bytes + resident_bytes + 8 * MIB
    return int(min(max(est, 16 * MIB), VMEM_CAP_BYTES))


def _nbytes(shape, dtype):
    return math.prod(shape) * jnp.dtype(dtype).itemsize


def _dot(a, b):
    return jnp.dot(a, b, preferred_element_type=F32)


def _dot_nt(a, b):
    return lax.dot_general(a, b, (((1,), (1,)), ((), ())), preferred_element_type=F32)


def _split3(x):
    hi = x.astype(BF16)
    r1 = x - hi.astype(F32)
    mid = r1.astype(BF16)
    lo = (r1 - mid.astype(F32)).astype(BF16)
    return hi, mid, lo


def _sigmoid(x):
    return 1.0 / (1.0 + jnp.exp(-x))


def _rms_scale(x):
    return x * lax.rsqrt(jnp.mean(x * x, axis=-1, keepdims=True) + RMS_EPS)


def _per_batch(y, ref, nb, t):
    if nb == 1:
        return y, ref[0]
    return y.reshape(nb, t, y.shape[-1]), ref[...]


def _mod_norm(x, g_ref, sc_ref, sh_ref, nb, t):
    y = _rms_scale(x) * g_ref[...]
    y, sc = _per_batch(y, sc_ref, nb, t)
    _, sh = _per_batch(x, sh_ref, nb, t)
    h = y * (1.0 + sc) + sh
    return h.reshape(x.shape)


def _col_bcast(frow, n):
    r = lax.broadcasted_iota(jnp.int32, (n, n), 0)
    c = lax.broadcasted_iota(jnp.int32, (n, n), 1)
    d = jnp.where(r == c, jnp.broadcast_to(frow, (n, n)), 0.0)
    ones = jnp.ones((n, LANES), BF16)
    hi, mid, lo = _split3(d)
    return _dot(hi, ones) + _dot(mid, ones) + _dot(lo, ones)


def _tile_lanes(x, n):
    reps = n // x.shape[1]
    return x if reps == 1 else jnp.concatenate([x] * reps, axis=1)


def _ada_kernel(c_ref, w_ref, b_ref, o_ref):
    c = c_ref[...]
    a = (c * _sigmoid(c)).astype(BF16)
    o_ref[...] = _dot(a, w_ref[...].astype(BF16)) + b_ref[...]


def _ada(c, w, b):
    n, d = c.shape
    nout = w.shape[1]
    tn = 1024 if nout % 1024 == 0 else 512
    return pl.pallas_call(
        _ada_kernel,
        out_shape=jax.ShapeDtypeStruct((n, nout), F32),
        grid=(nout // tn,),
        in_specs=[pl.BlockSpec((n, d), lambda j: (0, 0)),
                  pl.BlockSpec((d, tn), lambda j: (0, j)),
                  pl.BlockSpec((1, tn), lambda j: (0, j))],
        out_specs=pl.BlockSpec((n, tn), lambda j: (0, j)),
        compiler_params=pltpu.CompilerParams(
            dimension_semantics=("arbitrary",),
            vmem_limit_bytes=_vmem_limit(_nbytes((d, tn), F32) + _nbytes((n, tn), F32),
                                         _nbytes((n, d), F32) + _nbytes((d, tn), BF16))),
        name="ada",
    )(c, w, b.reshape(1, nout))


def _mixer_kernel(x_ref, sh_ref, sc_ref, g_ref, w_ref, cs_ref, wf_ref, bf_ref,
                  zq_ref, ka_ref, va_ref, kb_ref, vb_ref, lf_ref, h_scr,
                  *, nb, t, tiles_per_batch, tail, npg):
    i = pl.program_id(0)
    j = pl.program_id(1)

    @pl.when(j == 0)
    def _():
        h = _mod_norm(x_ref[...], g_ref, sc_ref, sh_ref, nb, t).astype(BF16)
        h_scr[...] = h
        fa = _dot(h, wf_ref[...])[:, :N_HEADS] + bf_ref[...]
        lf_ref[...] = jnp.minimum(fa, 0.0) - jnp.log1p(jnp.exp(-jnp.abs(fa)))

    acc = _dot(h_scr[...], w_ref[...]) * cs_ref[...]
    zq_ref[...] = acc.astype(BF16)

    def _group(g):
        return (j >= g * npg) & (j < (g + 1) * npg)

    @pl.when(_group(1))
    def _():
        ka_ref[...] = acc

    @pl.when(_group(2))
    def _():
        va_ref[...] = acc

    is_tail = (i % tiles_per_batch) == (tiles_per_batch - 1)
    rows = acc.shape[0]

    @pl.when(is_tail & _group(4))
    def _():
        kb_ref[...] = acc[rows - tail:, :]

    @pl.when(is_tail & _group(5))
    def _():
        vb_ref[...] = acc[rows - tail:, :]


def _mixer_in(x2d, shift, scale, g, w_main, col_scale, w_f, b_f, *, nb, t, tiles_per_batch, tail):
    m, d = x2d.shape
    n = w_main.shape[1]
    tm = nb * t
    tn = 512
    npg = D_HEADS // tn
    n_batches = (m // tm) // tiles_per_batch
    if nb == 1:
        mod_map = lambda i, j: (i // tiles_per_batch, 0, 0)
    else:
        mod_map = lambda i, j: (i, 0, 0)

    def grp_map(gidx):
        return lambda i, j: (i, jnp.clip(j - gidx * npg, 0, npg - 1))

    def tail_map(gidx):
        def f(i, j):
            is_tail = (i % tiles_per_batch) == (tiles_per_batch - 1)
            return (i // tiles_per_batch, jnp.where(is_tail, jnp.clip(j - gidx * npg, 0, npg - 1), 0))
        return f

    kern = functools.partial(_mixer_kernel, nb=nb, t=t, tiles_per_batch=tiles_per_batch, tail=tail, npg=npg)
    pipelined = (_nbytes((tm, d), F32) + _nbytes((d, tn), BF16) + _nbytes((tm, tn), BF16)
                 + 2 * _nbytes((tm, tn), F32) + 2 * _nbytes((tail, tn), F32))
    resident = _nbytes((tm, d), BF16) + 3 * _nbytes((tm, d), F32) // 2
    return pl.pallas_call(
        kern,
        out_shape=(jax.ShapeDtypeStruct((m, n), BF16),
                   jax.ShapeDtypeStruct((m, D_HEADS), F32),
                   jax.ShapeDtypeStruct((m, D_HEADS), F32),
                   jax.ShapeDtypeStruct((n_batches * tail, D_HEADS), F32),
                   jax.ShapeDtypeStruct((n_batches * tail, D_HEADS), F32),
                   jax.ShapeDtypeStruct((m, N_HEADS), F32)),
        grid=(m // tm, n // tn),
        in_specs=[pl.BlockSpec((tm, d), lambda i, j: (i, 0)),
                  pl.BlockSpec((nb, 1, d), mod_map),
                  pl.BlockSpec((nb, 1, d), mod_map),
                  pl.BlockSpec((1, d), lambda i, j: (0, 0)),
                  pl.BlockSpec((d, tn), lambda i, j: (0, j)),
                  pl.BlockSpec((1, tn), lambda i, j: (0, j)),
                  pl.BlockSpec((d, LANES), lambda i, j: (0, 0)),
                  pl.BlockSpec((1, N_HEADS), lambda i, j: (0, 0))],
        out_specs=(pl.BlockSpec((tm, tn), lambda i, j: (i, j)),
                   pl.BlockSpec((tm, tn), grp_map(1)),
                   pl.BlockSpec((tm, tn), grp_map(2)),
                   pl.BlockSpec((tail, tn), tail_map(4)),
                   pl.BlockSpec((tail, tn), tail_map(5)),
                   pl.BlockSpec((tm, N_HEADS), lambda i, j: (i, 0))),
        scratch_shapes=[pltpu.VMEM((tm, d), BF16)],
        compiler_params=pltpu.CompilerParams(
            dimension_semantics=("arbitrary", "arbitrary"),
            vmem_limit_bytes=_vmem_limit(pipelined, resident)),
        name="mixer_in",
    )(x2d, shift, scale, g, w_main, col_scale, w_f, b_f)


def _cumsum_kernel(x_ref, o_ref):
    grp, nr, _ = x_ref.shape
    r = lax.broadcasted_iota(jnp.int32, (LANES, LANES), 0)
    c = lax.broadcasted_iota(jnp.int32, (LANES, LANES), 1)
    upper = (r <= c).astype(BF16)
    rr = lax.broadcasted_iota(jnp.int32, (nr, nr), 0)
    cc = lax.broadcasted_iota(jnp.int32, (nr, nr), 1)
    lower = (cc < rr).astype(BF16)
    for gi in range(grp):
        hi, mid, lo = _split3(x_ref[gi])
        within = _dot(hi, upper) + _dot(mid, upper) + _dot(lo, upper)
        tot = jnp.broadcast_to(within[:, LANES - 1:LANES], (nr, LANES))
        hi, mid, lo = _split3(tot)
        before = _dot(lower, hi) + _dot(lower, mid) + _dot(lower, lo)
        o_ref[gi] = within + before


def _cumsum_rows(x):
    rows, length = x.shape
    nr = length // LANES
    grp = 8
    x3 = x.reshape(rows, nr, LANES)
    out = pl.pallas_call(
        _cumsum_kernel,
        out_shape=jax.ShapeDtypeStruct(x3.shape, F32),
        grid=(rows // grp,),
        in_specs=[pl.BlockSpec((grp, nr, LANES), lambda i: (i, 0, 0))],
        out_specs=pl.BlockSpec((grp, nr, LANES), lambda i: (i, 0, 0)),
        compiler_params=pltpu.CompilerParams(dimension_semantics=("arbitrary",)),
        name="cumsum_logf",
    )(x3)
    return out.reshape(rows, length)


def _fox_prompt_kernel(q_ref, k_ref, v_ref, f_ref, o_ref, m_scr, l_scr, acc_scr, *, tile):
    s_len = q_ref.shape[0]
    r = lax.broadcasted_iota(jnp.int32, (tile, tile), 0)
    c = lax.broadcasted_iota(jnp.int32, (tile, tile), 1)
    causal = c <= r

    def q_body(qi, carry):
        r0 = pl.multiple_of(qi * tile, tile)
        q = q_ref[pl.ds(r0, tile), :]
        f_q_row = f_ref[:, pl.ds(r0, tile)]
        f_q = _tile_lanes(_col_bcast(f_q_row, tile), tile)

        s = _dot_nt(q, k_ref[pl.ds(r0, tile), :]) + f_q - f_q_row
        s = jnp.where(causal, s, NEG_INF)
        m0 = jnp.max(s, axis=1, keepdims=True)
        p = jnp.exp(s - m0)
        m_scr[...] = m0
        l_scr[...] = jnp.sum(p, axis=1, keepdims=True)
        acc_scr[...] = _dot(p.astype(BF16), v_ref[pl.ds(r0, tile), :])

        def k_body(kj, inner):
            c0 = pl.multiple_of(kj * tile, tile)
            s = _dot_nt(q, k_ref[pl.ds(c0, tile), :]) + f_q - f_ref[:, pl.ds(c0, tile)]
            m_old = m_scr[...]
            m_new = jnp.maximum(m_old, jnp.max(s, axis=1, keepdims=True))
            alpha = jnp.exp(m_old - m_new)
            p = jnp.exp(s - m_new)
            l_scr[...] = alpha * l_scr[...] + jnp.sum(p, axis=1, keepdims=True)
            acc_scr[...] = alpha * acc_scr[...] + _dot(p.astype(BF16), v_ref[pl.ds(c0, tile), :])
            m_scr[...] = m_new
            return inner

        lax.fori_loop(0, qi, k_body, 0)
        o_ref[pl.ds(r0, tile), :] = (acc_scr[...] / l_scr[...]).astype(BF16)
        return carry

    lax.fori_loop(0, s_len // tile, q_body, 0)


def _fox_prompt(zq, f_rows, *, n_batch, s_len):
    tile = min(FOX_T, s_len)
    blk = (s_len, HEAD_DIM)
    return pl.pallas_call(
        functools.partial(_fox_prompt_kernel, tile=tile),
        out_shape=jax.ShapeDtypeStruct((n_batch * s_len, D_HEADS), BF16),
        grid=(n_batch, N_HEADS),
        in_specs=[pl.BlockSpec(blk, lambda b, h: (b, h)),
                  pl.BlockSpec(blk, lambda b, h: (b, N_HEADS + h)),
                  pl.BlockSpec(blk, lambda b, h: (b, 2 * N_HEADS + h)),
                  pl.BlockSpec((None, 1, s_len), lambda b, h: (b * N_HEADS + h, 0, 0))],
        out_specs=pl.BlockSpec(blk, lambda b, h: (b, h)),
        scratch_shapes=[pltpu.VMEM((tile, 1), F32), pltpu.VMEM((tile, 1), F32),
                        pltpu.VMEM((tile, HEAD_DIM), F32)],
        compiler_params=pltpu.CompilerParams(
            dimension_semantics=("arbitrary", "arbitrary"),
            vmem_limit_bytes=_vmem_limit(4 * _nbytes(blk, BF16) + _nbytes((8, s_len), F32))),
        name="fox_prompt",
    )(zq, zq, zq, f_rows)


def _band_bias_kernel(tbl_ref, o_ref):
    h = pl.program_id(0)
    wide = BAND_W - (BAND_PAD - REL_CLIP)
    r = lax.broadcasted_iota(jnp.int32, (BAND_TQ, wide), 0)
    c = lax.broadcasted_iota(jnp.int32, (BAND_TQ, wide), 1) + (BAND_W - wide)
    idx = jnp.clip(BAND_PAD + r - c, -REL_CLIP, REL_CLIP) + REL_CLIP
    far = tbl_ref[h, 2 * REL_CLIP]
    near = jnp.full((BAND_TQ, wide), far, F32)
    for tix in range(1, 2 * REL_CLIP):
        near = jnp.where(idx == tix, tbl_ref[h, tix], near)
    bias = jnp.concatenate([jnp.full((BAND_TQ, BAND_W - wide), far, F32), near], axis=1)
    rf = lax.broadcasted_iota(jnp.int32, (BAND_TQ, BAND_W), 0) // CHUNK
    cf = lax.broadcasted_iota(jnp.int32, (BAND_TQ, BAND_W), 1) // CHUNK
    visible = (cf >= rf) & (cf <= rf + N_PREV_CHUNKS)
    o_ref[...] = jnp.where(visible, bias, NEG_INF)


def _band_bias(table):
    return pl.pallas_call(
        _band_bias_kernel,
        out_shape=jax.ShapeDtypeStruct((N_HEADS, BAND_TQ, BAND_W), F32),
        grid=(N_HEADS,),
        in_specs=[pl.BlockSpec(memory_space=pltpu.SMEM)],
        out_specs=pl.BlockSpec((None, BAND_TQ, BAND_W), lambda h: (h, 0, 0)),
        compiler_params=pltpu.CompilerParams(dimension_semantics=("arbitrary",)),
        name="band_bias",
    )(table.T)


def _band_prompt_kernel(q_ref, k_ref, v_ref, mb_ref, o_ref, kpad, vpad):
    s_len = q_ref.shape[0]
    kpad[0:BAND_PAD, :] = jnp.zeros((BAND_PAD, HEAD_DIM), BF16)
    vpad[0:BAND_PAD, :] = jnp.zeros((BAND_PAD, HEAD_DIM), BF16)
    kpad[BAND_PAD:, :] = k_ref[...]
    vpad[BAND_PAD:, :] = v_ref[...]
    col = lax.broadcasted_iota(jnp.int32, (BAND_TQ, BAND_W), 1)

    def make_body(check_start):
        def body(i, carry):
            r0 = pl.multiple_of(i * BAND_TQ, BAND_TQ)
            q = q_ref[pl.ds(r0, BAND_TQ), :]
            s = _dot_nt(q, kpad[pl.ds(r0, BAND_W), :]) + mb_ref[...]
            if check_start:
                s = jnp.where(r0 + col >= BAND_PAD, s, NEG_INF)
            m = jnp.max(s, axis=1, keepdims=True)
            p = jnp.exp(s - m)
            den = jnp.sum(p, axis=1, keepdims=True)
            o = _dot(p.astype(BF16), vpad[pl.ds(r0, BAND_W), :])
            o_ref[pl.ds(r0, BAND_TQ), :] = (o / den).astype(BF16)
            return carry
        return body

    n_tiles = s_len // BAND_TQ
    n_head = BAND_PAD // BAND_TQ
    lax.fori_loop(0, n_head, make_body(True), 0)
    lax.fori_loop(n_head, n_tiles, make_body(False), 0)


def _band_prompt(zq, mb, *, n_batch, s_len):
    blk = (s_len, HEAD_DIM)
    off = 3 * N_HEADS
    return pl.pallas_call(
        _band_prompt_kernel,
        out_shape=jax.ShapeDtypeStruct((n_batch * s_len, D_HEADS), BF16),
        grid=(n_batch, N_HEADS),
        in_specs=[pl.BlockSpec(blk, lambda b, h: (b, off + h)),
                  pl.BlockSpec(blk, lambda b, h: (b, off + N_HEADS + h)),
                  pl.BlockSpec(blk, lambda b, h: (b, off + 2 * N_HEADS + h)),
                  pl.BlockSpec((None, BAND_TQ, BAND_W), lambda b, h: (h, 0, 0))],
        out_specs=pl.BlockSpec(blk, lambda b, h: (b, h)),
        scratch_shapes=[pltpu.VMEM((BAND_PAD + s_len, HEAD_DIM), BF16),
                        pltpu.VMEM((BAND_PAD + s_len, HEAD_DIM), BF16)],
        compiler_params=pltpu.CompilerParams(
            dimension_semantics=("arbitrary", "arbitrary"),
            vmem_limit_bytes=_vmem_limit(4 * _nbytes(blk, BF16) + _nbytes((BAND_TQ, BAND_W), F32),
                                         2 * _nbytes((BAND_PAD + s_len, HEAD_DIM), BF16))),
        name="band_prompt",
    )(zq, zq, zq, mb)


def _fox_sample_kernel(q_ref, kn_ref, vn_ref, ck_ref, cv_ref, f_ref, o_ref,
                       m_scr, l_scr, acc_scr, fq_scr, *, past, tk):
    kt = pl.program_id(1)
    t_new = q_ref.shape[0]

    def heads(h):
        return slice(h * HEAD_DIM, (h + 1) * HEAD_DIM)

    @pl.when(kt == 0)
    def _():
        for h in range(N_HEADS):
            fq_scr[h] = _col_bcast(f_ref[h:h + 1, past:past + t_new], t_new)
        m_scr[...] = jnp.full(m_scr.shape, NEG_INF, F32)
        l_scr[...] = jnp.zeros(l_scr.shape, F32)
        acc_scr[...] = jnp.zeros(acc_scr.shape, F32)

    def update(h, s, v):
        m_old = m_scr[h]
        m_new = jnp.maximum(m_old, jnp.max(s, axis=1, keepdims=True))
        alpha = jnp.exp(m_old - m_new)
        p = jnp.exp(s - m_new)
        l_scr[h] = alpha * l_scr[h] + jnp.sum(p, axis=1, keepdims=True)
        acc_scr[h] = alpha * acc_scr[h] + _dot(p.astype(BF16), v)
        m_scr[h] = m_new

    c0 = pl.multiple_of(kt * tk, tk)
    for h in range(N_HEADS):
        q = q_ref[:, heads(h)]
        k = ck_ref[:, heads(h)].astype(BF16)
        v = cv_ref[:, heads(h)].astype(BF16)
        s = _dot_nt(q, k) + _tile_lanes(fq_scr[h], tk) - f_ref[h:h + 1, pl.ds(c0, tk)]
        update(h, s, v)

    @pl.when(kt == pl.num_programs(1) - 1)
    def _():
        r = lax.broadcasted_iota(jnp.int32, (t_new, t_new), 0)
        c = lax.broadcasted_iota(jnp.int32, (t_new, t_new), 1)
        for h in range(N_HEADS):
            q = q_ref[:, heads(h)]
            s = _dot_nt(q, kn_ref[:, heads(h)]) + fq_scr[h][:, :t_new] - f_ref[h:h + 1, past:past + t_new]
            s = jnp.where(c <= r, s, NEG_INF)
            update(h, s, vn_ref[:, heads(h)])
            o_ref[:, heads(h)] = (acc_scr[h] / l_scr[h]).astype(BF16)


def _fox_sample(zq, cache_k, cache_v, f_all, *, n_batch, t_new, past):
    tk = min(1024, past)
    lp = f_all.shape[-1]
    blk = (t_new, D_HEADS)
    cblk = (None, tk, D_HEADS)
    return pl.pallas_call(
        functools.partial(_fox_sample_kernel, past=past, tk=tk),
        out_shape=jax.ShapeDtypeStruct((n_batch * t_new, D_HEADS), BF16),
        grid=(n_batch, past // tk),
        in_specs=[pl.BlockSpec(blk, lambda b, k: (b, 0)),
                  pl.BlockSpec(blk, lambda b, k: (b, 1)),
                  pl.BlockSpec(blk, lambda b, k: (b, 2)),
                  pl.BlockSpec(cblk, lambda b, k: (b, k, 0)),
                  pl.BlockSpec(cblk, lambda b, k: (b, k, 0)),
                  pl.BlockSpec((None, N_HEADS, lp), lambda b, k: (b, 0, 0))],
        out_specs=pl.BlockSpec(blk, lambda b, k: (b, 0)),
        scratch_shapes=[pltpu.VMEM((N_HEADS, t_new, 1), F32), pltpu.VMEM((N_HEADS, t_new, 1), F32),
                        pltpu.VMEM((N_HEADS, t_new, HEAD_DIM), F32),
                        pltpu.VMEM((N_HEADS, t_new, LANES), F32)],
        compiler_params=pltpu.CompilerParams(
            dimension_semantics=("arbitrary", "arbitrary"),
            vmem_limit_bytes=_vmem_limit(2 * _nbytes((tk, D_HEADS), F32) + 4 * _nbytes(blk, BF16)
                                         + _nbytes((N_HEADS, lp), F32), 8 * MIB)),
        name="fox_sample",
    )(zq, zq, zq, cache_k, cache_v, f_all)


def _band_sample_kernel(q_ref, kn_ref, vn_ref, ck_ref, cv_ref, mb_ref, o_ref):
    t_new = q_ref.shape[0]
    lb = ck_ref.shape[0]
    for h in range(N_HEADS):
        hs = slice(h * HEAD_DIM, (h + 1) * HEAD_DIM)
        q = q_ref[:, hs]
        s1 = _dot_nt(q, ck_ref[:, hs].astype(BF16)) + mb_ref[h, :, 0:lb]
        s2 = _dot_nt(q, kn_ref[:, hs]) + mb_ref[h, :, lb:lb + t_new]
        m = jnp.maximum(jnp.max(s1, axis=1, keepdims=True), jnp.max(s2, axis=1, keepdims=True))
        p1 = jnp.exp(s1 - m)
        p2 = jnp.exp(s2 - m)
        den = jnp.sum(p1, axis=1, keepdims=True) + jnp.sum(p2, axis=1, keepdims=True)
        o = _dot(p1.astype(BF16), cv_ref[:, hs].astype(BF16)) + _dot(p2.astype(BF16), vn_ref[:, hs])
        o_ref[:, hs] = (o / den).astype(BF16)


def _band_sample(zq, cache_k, cache_v, mb, *, n_batch, t_new):
    lb = cache_k.shape[1]
    blk = (t_new, D_HEADS)
    cblk = (None, lb, D_HEADS)
    return pl.pallas_call(
        _band_sample_kernel,
        out_shape=jax.ShapeDtypeStruct((n_batch * t_new, D_HEADS), BF16),
        grid=(n_batch,),
        in_specs=[pl.BlockSpec(blk, lambda b: (b, 3)),
                  pl.BlockSpec(blk, lambda b: (b, 4)),
                  pl.BlockSpec(blk, lambda b: (b, 5)),
                  pl.BlockSpec(cblk, lambda b: (b, 0, 0)),
                  pl.BlockSpec(cblk, lambda b: (b, 0, 0)),
                  pl.BlockSpec((N_HEADS, t_new, BAND_W), lambda b: (0, 0, 0))],
        out_specs=pl.BlockSpec(blk, lambda b: (b, 0)),
        compiler_params=pltpu.CompilerParams(
            dimension_semantics=("arbitrary",),
            vmem_limit_bytes=_vmem_limit(2 * _nbytes((lb, D_HEADS), F32) + 4 * _nbytes(blk, BF16),
                                         _nbytes((N_HEADS, t_new, BAND_W), F32) + 4 * MIB)),
        name="band_sample",
    )(zq, zq, zq, cache_k, cache_v, mb)


def _merge_kernel(oa_ref, ob_ref, za_ref, zb_ref, x_ref, gt_ref, woa_ref, wob_ref, wout_ref, o_ref, *, nb, t):
    a = _dot(oa_ref[...], woa_ref[...])
    b = _dot(ob_ref[...], wob_ref[...])
    mix = _sigmoid(za_ref[...].astype(F32)) * a + _sigmoid(zb_ref[...].astype(F32)) * b
    y = _dot(mix.astype(BF16), wout_ref[...])
    x = x_ref[...]
    y, gt = _per_batch(y, gt_ref, nb, t)
    x3, _ = _per_batch(x, gt_ref, nb, t)
    o_ref[...] = (x3 + gt * y).reshape(x.shape)


def _merge(oa, ob, zq, x2d, gate, w_oa, w_ob, w_out, *, nb, t, tiles_per_batch):
    m, d = x2d.shape
    tm = nb * t
    za_blk = Z_COLS // d
    if nb == 1:
        mod_map = lambda i: (i // tiles_per_batch, 0, 0)
    else:
        mod_map = lambda i: (i, 0, 0)
    const = lambda i: (0, 0)
    pipelined = (2 * _nbytes((tm, D_HEADS), BF16) + 2 * _nbytes((tm, d), BF16) + 2 * _nbytes((tm, d), F32))
    resident = (2 * (2 * _nbytes((D_HEADS, d), BF16) + _nbytes((d, d), BF16)) + 4 * _nbytes((tm, d), F32))
    return pl.pallas_call(
        functools.partial(_merge_kernel, nb=nb, t=t),
        out_shape=jax.ShapeDtypeStruct((m, d), F32),
        grid=(m // tm,),
        in_specs=[pl.BlockSpec((tm, D_HEADS), lambda i: (i, 0)),
                  pl.BlockSpec((tm, D_HEADS), lambda i: (i, 0)),
                  pl.BlockSpec((tm, d), lambda i: (i, za_blk)),
                  pl.BlockSpec((tm, d), lambda i: (i, za_blk + 1)),
                  pl.BlockSpec((tm, d), lambda i: (i, 0)),
                  pl.BlockSpec((nb, 1, d), mod_map),
                  pl.BlockSpec((D_HEADS, d), const),
                  pl.BlockSpec((D_HEADS, d), const),
                  pl.BlockSpec((d, d), const)],
        out_specs=pl.BlockSpec((tm, d), lambda i: (i, 0)),
        compiler_params=pltpu.CompilerParams(
            dimension_semantics=("arbitrary",),
            vmem_limit_bytes=_vmem_limit(pipelined, resident)),
        name="merge_out",
    )(oa, ob, zq, zq, x2d, gate, w_oa, w_ob, w_out)


def _ffn_kernel(x_ref, sh_ref, sc_ref, gt_ref, g_ref, gf_ref, wg_ref, wu_ref, wd_ref, o_ref,
                h_scr, acc_scr, *, nb, t):
    f = pl.program_id(1)

    @pl.when(f == 0)
    def _():
        h_scr[...] = _mod_norm(x_ref[...], g_ref, sc_ref, sh_ref, nb, t).astype(BF16)

    h = h_scr[...]
    gate = _dot(h, wg_ref[...])
    up = _dot(h, wu_ref[...])
    act = (gate * _sigmoid(gate) * up).astype(BF16)
    part = _dot(act, wd_ref[...])

    @pl.when(f == 0)
    def _():
        acc_scr[...] = part

    @pl.when(f > 0)
    def _():
        acc_scr[...] += part

    @pl.when(f == pl.num_programs(1) - 1)
    def _():
        x = x_ref[...]
        y, gt = _per_batch(acc_scr[...], gt_ref, nb, t)
        x3, _ = _per_batch(x, gt_ref, nb, t)
        x2 = (x3 + gt * y).reshape(x.shape)
        o_ref[...] = _rms_scale(x2) * gf_ref[...]


def _ffn(x2d, shift, scale, gate, g_ffn, g_final, w_gate, w_up, w_down, *, nb, t, tiles_per_batch):
    m, d = x2d.shape
    dff = w_gate.shape[1]
    tm = nb * t
    tf = 512
    if nb == 1:
        mod_map = lambda i, f: (i // tiles_per_batch, 0, 0)
    else:
        mod_map = lambda i, f: (i, 0, 0)
    pipelined = 2 * _nbytes((tm, d), F32) + 3 * _nbytes((d, tf), BF16)
    resident = _nbytes((tm, d), BF16) + _nbytes((tm, d), F32) + 3 * _nbytes((tm, tf), F32) + 2 * _nbytes((tm, d), F32)
    return pl.pallas_call(
        functools.partial(_ffn_kernel, nb=nb, t=t),
        out_shape=jax.ShapeDtypeStruct((m, d), F32),
        grid=(m // tm, dff // tf),
        in_specs=[pl.BlockSpec((tm, d), lambda i, f: (i, 0)),
                  pl.BlockSpec((nb, 1, d), mod_map),
                  pl.BlockSpec((nb, 1, d), mod_map),
                  pl.BlockSpec((nb, 1, d), mod_map),
                  pl.BlockSpec((1, d), lambda i, f: (0, 0)),
                  pl.BlockSpec((1, d), lambda i, f: (0, 0)),
                  pl.BlockSpec((d, tf), lambda i, f: (0, f)),
                  pl.BlockSpec((d, tf), lambda i, f: (0, f)),
                  pl.BlockSpec((tf, d), lambda i, f: (f, 0))],
        out_specs=pl.BlockSpec((tm, d), lambda i, f: (i, 0)),
        scratch_shapes=[pltpu.VMEM((tm, d), BF16), pltpu.VMEM((tm, d), F32)],
        compiler_params=pltpu.CompilerParams(
            dimension_semantics=("arbitrary", "arbitrary"),
            vmem_limit_bytes=_vmem_limit(pipelined, resident)),
        name="ffn_final",
    )(x2d, shift, scale, gate, g_ffn, g_final, w_gate, w_up, w_down)


def _row_tile(s_len, cap):
    tile = min(cap, s_len)
    assert s_len % tile == 0
    return tile


def kernel(x_prompt, x_sample, cache_fox_k, cache_fox_v, cache_fox_logf, cache_band_k, cache_band_v,
           c_prompt, c_sample, w_ada, b_ada, g_mix, w_in, b_f, rel_bias, w_oa, w_ob, w_out,
           g_ffn, w_gate, w_up, w_down, g_final):
    n_b, s_len, d = x_prompt.shape
    n_db, t_new, _ = x_sample.shape
    past = cache_fox_k.shape[2]
    lb = cache_band_k.shape[2]
    assert w_ada.shape[0] == 1, "single-layer trunk"
    assert lb == BAND_PAD and s_len >= BAND_PAD and s_len % FOX_T == 0
    assert Z_COLS % d == 0 and t_new % 8 == 0 and t_new + lb <= BAND_W and t_new <= BAND_TQ
    layer = 0
    inv = 1.0 / math.sqrt(HEAD_DIM)

    w_l = w_in[layer]
    w_main = jnp.concatenate([w_l[:, :3 * D_HEADS], w_l[:, 3 * D_HEADS + N_HEADS:]], axis=1).astype(BF16)
    w_f = jnp.pad(w_l[:, 3 * D_HEADS:3 * D_HEADS + N_HEADS], ((0, 0), (0, LANES - N_HEADS))).astype(BF16)
    n_main = w_main.shape[1]
    col = jnp.arange(n_main)
    is_q = (col < D_HEADS) | ((col >= 3 * D_HEADS) & (col < 4 * D_HEADS))
    col_scale = jnp.where(is_q, inv, 1.0).astype(F32).reshape(1, n_main)
    b_f2 = b_f[layer].reshape(1, N_HEADS)
    g_mix2 = g_mix[layer].reshape(1, d)
    g_ffn2 = g_ffn[layer].reshape(1, d)
    g_fin2 = g_final.reshape(1, d)
    w_oa_b, w_ob_b, w_out_b = w_oa[layer].astype(BF16), w_ob[layer].astype(BF16), w_out[layer].astype(BF16)
    w_gate_b, w_up_b, w_down_b = w_gate[layer].astype(BF16), w_up[layer].astype(BF16), w_down[layer].astype(BF16)

    mods = _ada(jnp.concatenate([c_prompt, c_sample], axis=0), w_ada[layer], b_ada[layer])

    def chunks(rows):
        return [rows[:, k * d:(k + 1) * d][:, None, :] for k in range(6)]

    sh1p, sc1p, gt1p, sh2p, sc2p, gt2p = chunks(mods[:n_b])
    sh1s, sc1s, gt1s, sh2s, sc2s, gt2s = chunks(mods[n_b:])
    mb = _band_bias(rel_bias[layer])

    xp = x_prompt.reshape(n_b * s_len, d)
    tm = _row_tile(s_len, 1024)
    zq, ka, va, kb, vb, lf = _mixer_in(xp, sh1p, sc1p, g_mix2, w_main, col_scale, w_f, b_f2,
                                       nb=1, t=tm, tiles_per_batch=s_len // tm, tail=BAND_PAD)
    lf_rows = lf.reshape(n_b, s_len, N_HEADS).transpose(0, 2, 1).reshape(n_b * N_HEADS, s_len)
    f_rows = _cumsum_rows(lf_rows).reshape(n_b * N_HEADS, 1, s_len)
    oa = _fox_prompt(zq, f_rows, n_batch=n_b, s_len=s_len)
    ob = _band_prompt(zq, mb, n_batch=n_b, s_len=s_len)
    tm = _row_tile(s_len, 256)
    x1 = _merge(oa, ob, zq, xp, gt1p, w_oa_b, w_ob_b, w_out_b, nb=1, t=tm, tiles_per_batch=s_len // tm)
    tm = _row_tile(s_len, 512)
    y_prompt = _ffn(x1, sh2p, sc2p, gt2p, g_ffn2, g_fin2, w_gate_b, w_up_b, w_down_b,
                    nb=1, t=tm, tiles_per_batch=s_len // tm).reshape(n_b, s_len, d)
    fox_shape = (1, n_b, s_len, N_HEADS, HEAD_DIM)
    band_shape = (1, n_b, BAND_PAD, N_HEADS, HEAD_DIM)
    prompt_out = (ka.reshape(fox_shape), va.reshape(fox_shape), lf.reshape(1, n_b, s_len, N_HEADS),
                  kb.reshape(band_shape), vb.reshape(band_shape))

    xs = x_sample.reshape(n_db * t_new, d)
    zqs, kas, vas, kbs, vbs, lfs = _mixer_in(xs, sh1s, sc1s, g_mix2, w_main, col_scale, w_f, b_f2,
                                             nb=n_db, t=t_new, tiles_per_batch=1, tail=n_db * t_new)
    lp = -(-(past + t_new) // 2048) * 2048
    lf_all = jnp.concatenate([cache_fox_logf[layer].transpose(0, 2, 1),
                              lfs.reshape(n_db, t_new, N_HEADS).transpose(0, 2, 1),
                              jnp.zeros((n_db, N_HEADS, lp - past - t_new), F32)], axis=-1)
    f_all = _cumsum_rows(lf_all.reshape(n_db * N_HEADS, lp)).reshape(n_db, N_HEADS, lp)
    oas = _fox_sample(zqs, cache_fox_k[layer].reshape(n_db, past, D_HEADS),
                      cache_fox_v[layer].reshape(n_db, past, D_HEADS), f_all,
                      n_batch=n_db, t_new=t_new, past=past)
    obs = _band_sample(zqs, cache_band_k[layer].reshape(n_db, lb, D_HEADS),
                       cache_band_v[layer].reshape(n_db, lb, D_HEADS), mb, n_batch=n_db, t_new=t_new)
    x1s = _merge(oas, obs, zqs, xs, gt1s, w_oa_b, w_ob_b, w_out_b, nb=n_db, t=t_new, tiles_per_batch=1)
    y_sample = _ffn(x1s, sh2s, sc2s, gt2s, g_ffn2, g_fin2, w_gate_b, w_up_b, w_down_b,
                    nb=n_db, t=t_new, tiles_per_batch=1).reshape(n_db, t_new, d)
    s_shape = (1, n_db, t_new, N_HEADS, HEAD_DIM)
    sample_out = (kas.reshape(s_shape), vas.reshape(s_shape), lfs.reshape(1, n_db, t_new, N_HEADS),
                  kbs.reshape(s_shape), vbs.reshape(s_shape))

    return (y_prompt, y_sample) + prompt_out + sample_out
```

```python
import functools
import math

import jax
import jax.numpy as jnp
from jax import lax
from jax.experimental import pallas as pl
from jax.experimental.pallas import tpu as pltpu

F32 = jnp.float32
BF16 = jnp.bfloat16

HEAD_DIM = 128
N_HEADS = 8
D_HEADS = N_HEADS * HEAD_DIM
CHUNK = 64
N_PREV_CHUNKS = 8
BAND_PAD = N_PREV_CHUNKS * CHUNK
REL_CLIP = 128
RMS_EPS = 1e-6
NEG_INF = -1e30
LOG2E = math.log2(math.e)
LANES = 128
Z_COLS = 6 * D_HEADS

VMEM_CAP_BYTES = 60 * 1024 * 1024
MIB = 1024 * 1024

BAND_TQ = 256
BAND_W = BAND_PAD + BAND_TQ
BAND_TQ_S = 128
BAND_W_S = BAND_PAD + BAND_TQ_S
FOX_TQ = 512
FOX_TK = 256


def _vmem_limit(pipelined_bytes, resident_bytes=0):
    est = 2 * pipelined_bytes + resident_bytes + 8 * MIB
    return int(min(max(est, 16 * MIB), VMEM_CAP_BYTES))


def _nbytes(shape, dtype):
    return math.prod(shape) * jnp.dtype(dtype).itemsize


def _dot(a, b):
    return jnp.dot(a, b, preferred_element_type=F32)


def _dot_nt(a, b):
    return lax.dot_general(a, b, (((1,), (1,)), ((), ())), preferred_element_type=F32)


def _split3(x):
    hi = x.astype(BF16)
    r1 = x - hi.astype(F32)
    mid = r1.astype(BF16)
    lo = (r1 - mid.astype(F32)).astype(BF16)
    return hi, mid, lo


def _sigmoid(x):
    return 1.0 / (1.0 + jnp.exp(-x))


def _rms_scale(x):
    return x * lax.rsqrt(jnp.mean(x * x, axis=-1, keepdims=True) + RMS_EPS)


def _per_batch(y, ref, nb, t):
    if nb == 1:
        return y, ref[0]
    return y.reshape(nb, t, y.shape[-1]), ref[...]


def _mod_norm(x, g_ref, sc_ref, sh_ref, nb, t):
    y = _rms_scale(x) * g_ref[...]
    y, sc = _per_batch(y, sc_ref, nb, t)
    _, sh = _per_batch(x, sh_ref, nb, t)
    h = y * (1.0 + sc) + sh
    return h.reshape(x.shape)


def _col_bcast(frow, n):
    r = lax.broadcasted_iota(jnp.int32, (n, n), 0)
    c = lax.broadcasted_iota(jnp.int32, (n, n), 1)
    d = jnp.where(r == c, jnp.broadcast_to(frow, (n, n)), 0.0)
    ones = jnp.ones((n, LANES), BF16)
    hi, mid, lo = _split3(d)
    return _dot(hi, ones) + _dot(mid, ones) + _dot(lo, ones)


def _tile_lanes(x, n):
    reps = n // x.shape[1]
    return x if reps == 1 else jnp.concatenate([x] * reps, axis=1)


def _transpose_bf16(x):
    return x.astype(F32).T.astype(BF16)


def _ada_kernel(c_ref, w_ref, b_ref, o_ref):
    c = c_ref[...]
    a = (c * _sigmoid(c)).astype(BF16)
    o_ref[...] = _dot(a, w_ref[...].astype(BF16)) + b_ref[...]


def _ada(c, w, b):
    n, d = c.shape
    nout = w.shape[1]
    tn = 1024 if nout % 1024 == 0 else 512
    return pl.pallas_call(
        _ada_kernel,
        out_shape=jax.ShapeDtypeStruct((n, nout), F32),
        grid=(nout // tn,),
        in_specs=[pl.BlockSpec((n, d), lambda j: (0, 0)),
                  pl.BlockSpec((d, tn), lambda j: (0, j)),
                  pl.BlockSpec((1, tn), lambda j: (0, j))],
        out_specs=pl.BlockSpec((n, tn), lambda j: (0, j)),
        compiler_params=pltpu.CompilerParams(
            dimension_semantics=("arbitrary",),
            vmem_limit_bytes=_vmem_limit(_nbytes((d, tn), F32) + _nbytes((n, tn), F32),
                                         _nbytes((n, d), F32) + _nbytes((d, tn), BF16))),
        name="ada",
    )(c, w, b.reshape(1, nout))


def _mixer_kernel(x_ref, sh_ref, sc_ref, g_ref, w_ref, cs_ref, wf_ref, bf_ref,
                  zq_ref, ka_ref, va_ref, kb_ref, vb_ref, lf_ref, h_scr,
                  *, nb, t, tiles_per_batch, tail, npg):
    i = pl.program_id(0)
    j = pl.program_id(1)

    @pl.when(j == 0)
    def _():
        h = _mod_norm(x_ref[...], g_ref, sc_ref, sh_ref, nb, t).astype(BF16)
        h_scr[...] = h
        fa = _dot(h, wf_ref[...])[:, :N_HEADS] + bf_ref[...]
        lf_ref[...] = jnp.minimum(fa, 0.0) - jnp.log1p(jnp.exp(-jnp.abs(fa)))

    acc = _dot(h_scr[...], w_ref[...])
    zq_ref[...] = (acc * cs_ref[...]).astype(BF16)

    def _group(g):
        return (j >= g * npg) & (j < (g + 1) * npg)

    @pl.when(_group(1))
    def _():
        ka_ref[...] = acc

    @pl.when(_group(2))
    def _():
        va_ref[...] = acc

    is_tail = (i % tiles_per_batch) == (tiles_per_batch - 1)
    rows = acc.shape[0]

    @pl.when(is_tail & _group(4))
    def _():
        kb_ref[...] = acc[rows - tail:, :]

    @pl.when(is_tail & _group(5))
    def _():
        vb_ref[...] = acc[rows - tail:, :]


def _mixer_in(x2d, shift, scale, g, w_main, col_scale, w_f, b_f, *, nb, t, tiles_per_batch, tail):
    m, d = x2d.shape
    n = w_main.shape[1]
    tm = nb * t
    tn = 512
    npg = D_HEADS // tn
    n_batches = (m // tm) // tiles_per_batch
    if nb == 1:
        mod_map = lambda i, j: (i // tiles_per_batch, 0, 0)
    else:
        mod_map = lambda i, j: (i, 0, 0)

    def grp_map(gidx):
        return lambda i, j: (i, jnp.clip(j - gidx * npg, 0, npg - 1))

    def tail_map(gidx):
        def f(i, j):
            is_tail = (i % tiles_per_batch) == (tiles_per_batch - 1)
            return (i // tiles_per_batch, jnp.where(is_tail, jnp.clip(j - gidx * npg, 0, npg - 1), 0))
        return f

    kern = functools.partial(_mixer_kernel, nb=nb, t=t, tiles_per_batch=tiles_per_batch, tail=tail, npg=npg)
    pipelined = (_nbytes((tm, d), F32) + _nbytes((d, tn), BF16) + _nbytes((tm, tn), BF16)
                 + 2 * _nbytes((tm, tn), F32) + 2 * _nbytes((tail, tn), F32))
    resident = _nbytes((tm, d), BF16) + 3 * _nbytes((tm, d), F32) // 2
    return pl.pallas_call(
        kern,
        out_shape=(jax.ShapeDtypeStruct((m, n), BF16),
                   jax.ShapeDtypeStruct((m, D_HEADS), F32),
                   jax.ShapeDtypeStruct((m, D_HEADS), F32),
                   jax.ShapeDtypeStruct((n_batches * tail, D_HEADS), F32),
                   jax.ShapeDtypeStruct((n_batches * tail, D_HEADS), F32),
                   jax.ShapeDtypeStruct((m, N_HEADS), F32)),
        grid=(m // tm, n // tn),
        in_specs=[pl.BlockSpec((tm, d), lambda i, j: (i, 0)),
                  pl.BlockSpec((nb, 1, d), mod_map),
                  pl.BlockSpec((nb, 1, d), mod_map),
                  pl.BlockSpec((1, d), lambda i, j: (0, 0)),
                  pl.BlockSpec((d, tn), lambda i, j: (0, j)),
                  pl.BlockSpec((1, tn), lambda i, j: (0, j)),
                  pl.BlockSpec((d, LANES), lambda i, j: (0, 0)),
                  pl.BlockSpec((1, N_HEADS), lambda i, j: (0, 0))],
        out_specs=(pl.BlockSpec((tm, tn), lambda i, j: (i, j)),
                   pl.BlockSpec((tm, tn), grp_map(1)),
                   pl.BlockSpec((tm, tn), grp_map(2)),
                   pl.BlockSpec((tail, tn), tail_map(4)),
                   pl.BlockSpec((tail, tn), tail_map(5)),
                   pl.BlockSpec((tm, N_HEADS), lambda i, j: (i, 0))),
        scratch_shapes=[pltpu.VMEM((tm, d), BF16)],
        compiler_params=pltpu.CompilerParams(
            dimension_semantics=("arbitrary", "arbitrary"),
            vmem_limit_bytes=_vmem_limit(pipelined, resident)),
        name="mixer_in",
    )(x2d, shift, scale, g, w_main, col_scale, w_f, b_f)


def _cumsum_kernel(x_ref, o_ref):
    grp, nr, _ = x_ref.shape
    r = lax.broadcasted_iota(jnp.int32, (LANES, LANES), 0)
    c = lax.broadcasted_iota(jnp.int32, (LANES, LANES), 1)
    upper = (r <= c).astype(BF16)
    rr = lax.broadcasted_iota(jnp.int32, (nr, nr), 0)
    cc = lax.broadcasted_iota(jnp.int32, (nr, nr), 1)
    lower = (cc < rr).astype(BF16)
    for gi in range(grp):
        hi, mid, lo = _split3(x_ref[gi])
        within = _dot(hi, upper) + _dot(mid, upper) + _dot(lo, upper)
        tot = jnp.broadcast_to(within[:, LANES - 1:LANES], (nr, LANES))
        hi, mid, lo = _split3(tot)
        before = _dot(lower, hi) + _dot(lower, mid) + _dot(lower, lo)
        o_ref[gi] = within + before


def _cumsum_rows(x):
    rows, length = x.shape
    nr = length // LANES
    grp = 8
    x3 = x.reshape(rows, nr, LANES)
    out = pl.pallas_call(
        _cumsum_kernel,
        out_shape=jax.ShapeDtypeStruct(x3.shape, F32),
        grid=(rows // grp,),
        in_specs=[pl.BlockSpec((grp, nr, LANES), lambda i: (i, 0, 0))],
        out_specs=pl.BlockSpec((grp, nr, LANES), lambda i: (i, 0, 0)),
        compiler_params=pltpu.CompilerParams(dimension_semantics=("arbitrary",)),
        name="cumsum_logf",
    )(x3)
    return out.reshape(rows, length)


def _fox_prompt_kernel(q_ref, k_ref, v_ref, f_ref, o_ref, qxt_scr, kx_scr, vt_scr, m_scr, l_scr, acc_scr,
                       *, tq, tk):
    s_len = q_ref.shape[0]
    n_sub = tq // tk
    lane = lax.broadcasted_iota(jnp.int32, (tk, HEAD_DIM), 1)
    sub = lax.broadcasted_iota(jnp.int32, (HEAD_DIM, tk), 0)
    minus_ones = jnp.where(sub < 3, -1.0, 0.0).astype(BF16)
    zeros = jnp.zeros((tk, HEAD_DIM), BF16)

    def setup(ci, carry):
        c0 = pl.multiple_of(ci * tk, tk)
        rows = pl.ds(c0, tk)
        qxt_scr[0:HEAD_DIM, rows] = _transpose_bf16(q_ref[rows, :])
        qxt_scr[HEAD_DIM:, rows] = minus_ones
        vt_scr[:, rows] = _transpose_bf16(v_ref[rows, :])
        hi, mid, lo = _split3(_col_bcast(f_ref[:, rows] * LOG2E, tk))
        kx_scr[rows, 0:HEAD_DIM] = k_ref[rows, :]
        kx_scr[rows, HEAD_DIM:] = jnp.where(lane == 0, hi, jnp.where(lane == 1, mid, jnp.where(lane == 2, lo, zeros)))
        return carry

    lax.fori_loop(0, s_len // tk, setup, 0)

    key_ix = lax.broadcasted_iota(jnp.int32, (tk, tq), 0)
    qry_ix = lax.broadcasted_iota(jnp.int32, (tk, tq), 1)

    def q_body(qi, carry):
        r0 = pl.multiple_of(qi * tq, tq)
        f_q = f_ref[:, pl.ds(r0, tq)] * LOG2E

        def qk(c0):
            c0 = pl.multiple_of(c0, tk)
            return _dot(kx_scr[pl.ds(c0, tk), :], qxt_scr[:, pl.ds(r0, tq)])

        def fin(c0, s, first=False):
            c0 = pl.multiple_of(c0, tk)
            t_max = jnp.max(s, axis=0, keepdims=True) + f_q
            if first:
                m_new = t_max
            else:
                m_old = m_scr[...]
                m_new = jnp.maximum(m_old, t_max)
                alpha = jnp.exp2(m_old - m_new)
            p = jnp.exp2(s - (m_new - f_q))
            p_sum = jnp.sum(p, axis=0, keepdims=True)
            pv = _dot(vt_scr[:, pl.ds(c0, tk)], p.astype(BF16))
            if first:
                l_scr[...] = p_sum
                acc_scr[...] = pv
            else:
                l_scr[...] = alpha * l_scr[...] + p_sum
                acc_scr[...] = alpha * acc_scr[...] + pv
            m_scr[...] = m_new

        diag = [jnp.where(key_ix + j * tk <= qry_ix, qk(r0 + j * tk), NEG_INF) for j in range(n_sub)]
        s_first = qk(0)
        for j in range(n_sub):
            fin(r0 + j * tk, diag[j], first=(j == 0))

        def k_body(kb, s_cur):
            c0 = kb * tq
            for j in range(1, n_sub):
                s_nxt = qk(c0 + j * tk)
                fin(c0 + (j - 1) * tk, s_cur)
                s_cur = s_nxt
            s_nxt = qk(c0 + tq)
            fin(c0 + (n_sub - 1) * tk, s_cur)
            return s_nxt

        lax.fori_loop(0, qi, k_body, s_first)
        o_t = acc_scr[...] * (1.0 / l_scr[...])
        o_ref[pl.ds(r0, tq), :] = o_t.T.astype(BF16)
        return carry

    lax.fori_loop(0, s_len // tq, q_body, 0)


def _fox_prompt(zq, f_rows, *, n_batch, s_len):
    tq = min(FOX_TQ, s_len)
    tk = min(FOX_TK, tq)
    blk = (s_len, HEAD_DIM)
    scratch = [pltpu.VMEM((2 * HEAD_DIM, s_len), BF16), pltpu.VMEM((s_len, 2 * HEAD_DIM), BF16),
               pltpu.VMEM((HEAD_DIM, s_len), BF16),
               pltpu.VMEM((1, tq), F32), pltpu.VMEM((1, tq), F32), pltpu.VMEM((HEAD_DIM, tq), F32)]
    resident = 5 * _nbytes(blk, BF16) + 8 * _nbytes((tk, tq), F32)
    return pl.pallas_call(
        functools.partial(_fox_prompt_kernel, tq=tq, tk=tk),
        out_shape=jax.ShapeDtypeStruct((n_batch * s_len, D_HEADS), BF16),
        grid=(n_batch, N_HEADS),
        in_specs=[pl.BlockSpec(blk, lambda b, h: (b, h)),
                  pl.BlockSpec(blk, lambda b, h: (b, N_HEADS + h)),
                  pl.BlockSpec(blk, lambda b, h: (b, 2 * N_HEADS + h)),
                  pl.BlockSpec((None, 1, s_len), lambda b, h: (b * N_HEADS + h, 0, 0))],
        out_specs=pl.BlockSpec(blk, lambda b, h: (b, h)),
        scratch_shapes=scratch,
        compiler_params=pltpu.CompilerParams(
            dimension_semantics=("arbitrary", "arbitrary"),
            vmem_limit_bytes=_vmem_limit(4 * _nbytes(blk, BF16) + _nbytes((8, s_len), F32), resident)),
        name="fox_prompt",
    )(zq, zq, zq, f_rows)


def _band_bias_kernel(tbl_ref, o_ref, *, tq, width, transposed):
    h = pl.program_id(0)
    near_w = width - (BAND_PAD - REL_CLIP)
    if transposed:
        shape, near_shape, q_ax, k_ax = (width, tq), (near_w, tq), 1, 0
    else:
        shape, near_shape, q_ax, k_ax = (tq, width), (tq, near_w), 0, 1
    r = lax.broadcasted_iota(jnp.int32, near_shape, q_ax)
    c = lax.broadcasted_iota(jnp.int32, near_shape, k_ax) + (width - near_w)
    idx = jnp.clip(BAND_PAD + r - c, -REL_CLIP, REL_CLIP) + REL_CLIP
    far = tbl_ref[h, 2 * REL_CLIP]
    near = jnp.full(near_shape, far, F32)
    for tix in range(1, 2 * REL_CLIP):
        near = jnp.where(idx == tix, tbl_ref[h, tix], near)
    far_shape = (width - near_w, tq) if transposed else (tq, width - near_w)
    bias = jnp.concatenate([jnp.full(far_shape, far, F32), near], axis=k_ax)
    rf = lax.broadcasted_iota(jnp.int32, shape, q_ax) // CHUNK
    cf = lax.broadcasted_iota(jnp.int32, shape, k_ax) // CHUNK
    visible = (cf >= rf) & (cf <= rf + N_PREV_CHUNKS)
    o_ref[...] = jnp.where(visible, bias * LOG2E, NEG_INF)


def _band_bias(table, *, tq, transposed):
    width = BAND_PAD + tq
    shape = (width, tq) if transposed else (tq, width)
    return pl.pallas_call(
        functools.partial(_band_bias_kernel, tq=tq, width=width, transposed=transposed),
        out_shape=jax.ShapeDtypeStruct((N_HEADS,) + shape, F32),
        grid=(N_HEADS,),
        in_specs=[pl.BlockSpec(memory_space=pltpu.SMEM)],
        out_specs=pl.BlockSpec((None,) + shape, lambda h: (h, 0, 0)),
        compiler_params=pltpu.CompilerParams(dimension_semantics=("arbitrary",)),
        name="band_bias",
    )(table.T)


def _band_prompt_kernel(q_ref, k_ref, v_ref, mb_ref, o_ref, qt_scr, kpad, vtpad):
    s_len = q_ref.shape[0]
    kpad[0:BAND_PAD, :] = jnp.zeros((BAND_PAD, HEAD_DIM), BF16)
    kpad[BAND_PAD:, :] = k_ref[...]
    vtpad[:, 0:BAND_PAD] = jnp.zeros((HEAD_DIM, BAND_PAD), BF16)

    def setup(ci, carry):
        c0 = pl.multiple_of(ci * BAND_TQ, BAND_TQ)
        qt_scr[:, pl.ds(c0, BAND_TQ)] = _transpose_bf16(q_ref[pl.ds(c0, BAND_TQ), :])
        vtpad[:, pl.ds(pl.multiple_of(BAND_PAD + c0, BAND_TQ), BAND_TQ)] = _transpose_bf16(v_ref[pl.ds(c0, BAND_TQ), :])
        return carry

    n_tiles = s_len // BAND_TQ
    lax.fori_loop(0, n_tiles, setup, 0)
    key_ix = lax.broadcasted_iota(jnp.int32, (BAND_W, BAND_TQ), 0)

    def qk(ti):
        r0 = pl.multiple_of(ti * BAND_TQ, BAND_TQ)
        return _dot(kpad[pl.ds(r0, BAND_W), :], qt_scr[:, pl.ds(r0, BAND_TQ)]) + mb_ref[...]

    def fin(ti, s):
        r0 = pl.multiple_of(ti * BAND_TQ, BAND_TQ)
        m = jnp.max(s, axis=0, keepdims=True)
        p = jnp.exp2(s - m)
        den = jnp.sum(p, axis=0, keepdims=True)
        o_t = _dot(vtpad[:, pl.ds(r0, BAND_W)], p.astype(BF16)) * (1.0 / den)
        o_ref[pl.ds(r0, BAND_TQ), :] = o_t.T.astype(BF16)

    n_head = BAND_PAD // BAND_TQ
    head = [jnp.where(ti * BAND_TQ + key_ix >= BAND_PAD, qk(ti), NEG_INF) for ti in range(n_head)]
    s_first = qk(n_head)
    for ti in range(n_head):
        fin(ti, head[ti])

    def pair(pi, s_a):
        a = n_head + 2 * pi
        s_b = qk(a + 1)
        fin(a, s_a)
        s_n = qk(jnp.minimum(a + 2, n_tiles - 1))
        fin(a + 1, s_b)
        return s_n

    lax.fori_loop(0, (n_tiles - n_head) // 2, pair, s_first)


def _band_prompt(zq, mbt, *, n_batch, s_len):
    blk = (s_len, HEAD_DIM)
    off = 3 * N_HEADS
    assert (s_len // BAND_TQ - BAND_PAD // BAND_TQ) % 2 == 0
    scratch = [pltpu.VMEM((HEAD_DIM, s_len), BF16), pltpu.VMEM((BAND_PAD + s_len, HEAD_DIM), BF16),
               pltpu.VMEM((HEAD_DIM, BAND_PAD + s_len), BF16)]
    resident = 3 * _nbytes((BAND_PAD + s_len, HEAD_DIM), BF16) + 8 * _nbytes((BAND_W, BAND_TQ), F32)
    return pl.pallas_call(
        _band_prompt_kernel,
        out_shape=jax.ShapeDtypeStruct((n_batch * s_len, D_HEADS), BF16),
        grid=(n_batch, N_HEADS),
        in_specs=[pl.BlockSpec(blk, lambda b, h: (b, off + h)),
                  pl.BlockSpec(blk, lambda b, h: (b, off + N_HEADS + h)),
                  pl.BlockSpec(blk, lambda b, h: (b, off + 2 * N_HEADS + h)),
                  pl.BlockSpec((None, BAND_W, BAND_TQ), lambda b, h: (h, 0, 0))],
        out_specs=pl.BlockSpec(blk, lambda b, h: (b, h)),
        scratch_shapes=scratch,
        compiler_params=pltpu.CompilerParams(
            dimension_semantics=("arbitrary", "arbitrary"),
            vmem_limit_bytes=_vmem_limit(4 * _nbytes(blk, BF16) + _nbytes((BAND_W, BAND_TQ), F32), resident)),
        name="band_prompt",
    )(zq, zq, zq, mbt)


def _fox_sample_kernel(q_ref, kn_ref, vn_ref, ck_ref, cv_ref, f_ref, o_ref,
                       m_scr, l_scr, acc_scr, fq_scr, *, past, tk):
    kt = pl.program_id(1)
    t_new = q_ref.shape[0]

    def heads(h):
        return slice(h * HEAD_DIM, (h + 1) * HEAD_DIM)

    @pl.when(kt == 0)
    def _():
        for h in range(N_HEADS):
            fq_scr[h] = _col_bcast(f_ref[h:h + 1, past:past + t_new] * LOG2E, t_new)
        m_scr[...] = jnp.full(m_scr.shape, NEG_INF, F32)
        l_scr[...] = jnp.zeros(l_scr.shape, F32)
        acc_scr[...] = jnp.zeros(acc_scr.shape, F32)

    def update(h, s, v):
        m_old = m_scr[h]
        m_new = jnp.maximum(m_old, jnp.max(s, axis=1, keepdims=True))
        alpha = jnp.exp2(m_old - m_new)
        p = jnp.exp2(s - m_new)
        l_scr[h] = alpha * l_scr[h] + jnp.sum(p, axis=1, keepdims=True)
        acc_scr[h] = alpha * acc_scr[h] + _dot(p.astype(BF16), v)
        m_scr[h] = m_new

    c0 = pl.multiple_of(kt * tk, tk)
    for h in range(N_HEADS):
        q = q_ref[:, heads(h)]
        k = ck_ref[:, h, :].astype(BF16)
        v = cv_ref[:, h, :].astype(BF16)
        s = _dot_nt(q, k) + _tile_lanes(fq_scr[h], tk) - f_ref[h:h + 1, pl.ds(c0, tk)] * LOG2E
        update(h, s, v)

    @pl.when(kt == pl.num_programs(1) - 1)
    def _():
        r = lax.broadcasted_iota(jnp.int32, (t_new, t_new), 0)
        c = lax.broadcasted_iota(jnp.int32, (t_new, t_new), 1)
        for h in range(N_HEADS):
            q = q_ref[:, heads(h)]
            f_new = f_ref[h:h + 1, past:past + t_new] * LOG2E
            s = _dot_nt(q, kn_ref[:, heads(h)]) + fq_scr[h][:, :t_new] - f_new
            s = jnp.where(c <= r, s, NEG_INF)
            update(h, s, vn_ref[:, heads(h)])
            o_ref[:, heads(h)] = (acc_scr[h] / l_scr[h]).astype(BF16)


def _fox_sample(zq, cache_k, cache_v, f_all, *, n_batch, t_new, past):
    tk = min(1024, past)
    lp = f_all.shape[-1]
    blk = (t_new, D_HEADS)
    cblk = (None, tk, N_HEADS, HEAD_DIM)
    return pl.pallas_call(
        functools.partial(_fox_sample_kernel, past=past, tk=tk),
        out_shape=jax.ShapeDtypeStruct((n_batch * t_new, D_HEADS), BF16),
        grid=(n_batch, past // tk),
        in_specs=[pl.BlockSpec(blk, lambda b, k: (b, 0)),
                  pl.BlockSpec(blk, lambda b, k: (b, 1)),
                  pl.BlockSpec(blk, lambda b, k: (b, 2)),
                  pl.BlockSpec(cblk, lambda b, k: (b, k, 0, 0)),
                  pl.BlockSpec(cblk, lambda b, k: (b, k, 0, 0)),
                  pl.BlockSpec((None, N_HEADS, lp), lambda b, k: (b, 0, 0))],
        out_specs=pl.BlockSpec(blk, lambda b, k: (b, 0)),
        scratch_shapes=[pltpu.VMEM((N_HEADS, t_new, 1), F32), pltpu.VMEM((N_HEADS, t_new, 1), F32),
                        pltpu.VMEM((N_HEADS, t_new, HEAD_DIM), F32),
                        pltpu.VMEM((N_HEADS, t_new, LANES), F32)],
        compiler_params=pltpu.CompilerParams(
            dimension_semantics=("arbitrary", "arbitrary"),
            vmem_limit_bytes=_vmem_limit(2 * _nbytes((tk, D_HEADS), F32) + 4 * _nbytes(blk, BF16)
                                         + _nbytes((N_HEADS, lp), F32), 8 * MIB)),
        name="fox_sample",
    )(zq, zq, zq, cache_k, cache_v, f_all)


def _band_sample_kernel(q_ref, kn_ref, vn_ref, ck_ref, cv_ref, mb_ref, o_ref):
    t_new = q_ref.shape[0]
    lb = ck_ref.shape[0]
    for h in range(N_HEADS):
        hs = slice(h * HEAD_DIM, (h + 1) * HEAD_DIM)
        q = q_ref[:, hs]
        s1 = _dot_nt(q, ck_ref[:, h, :].astype(BF16)) + mb_ref[h, :, 0:lb]
        s2 = _dot_nt(q, kn_ref[:, hs]) + mb_ref[h, :, lb:lb + t_new]
        m = jnp.maximum(jnp.max(s1, axis=1, keepdims=True), jnp.max(s2, axis=1, keepdims=True))
        p1 = jnp.exp2(s1 - m)
        p2 = jnp.exp2(s2 - m)
        den = jnp.sum(p1, axis=1, keepdims=True) + jnp.sum(p2, axis=1, keepdims=True)
        o = _dot(p1.astype(BF16), cv_ref[:, h, :].astype(BF16)) + _dot(p2.astype(BF16), vn_ref[:, hs])
        o_ref[:, hs] = (o / den).astype(BF16)


def _band_sample(zq, cache_k, cache_v, mb, *, n_batch, t_new):
    lb = cache_k.shape[1]
    blk = (t_new, D_HEADS)
    cblk = (None, lb, N_HEADS, HEAD_DIM)
    return pl.pallas_call(
        _band_sample_kernel,
        out_shape=jax.ShapeDtypeStruct((n_batch * t_new, D_HEADS), BF16),
        grid=(n_batch,),
        in_specs=[pl.BlockSpec(blk, lambda b: (b, 3)),
                  pl.BlockSpec(blk, lambda b: (b, 4)),
                  pl.BlockSpec(blk, lambda b: (b, 5)),
                  pl.BlockSpec(cblk, lambda b: (b, 0, 0, 0)),
                  pl.BlockSpec(cblk, lambda b: (b, 0, 0, 0)),
                  pl.BlockSpec((N_HEADS, t_new, BAND_W_S), lambda b: (0, 0, 0))],
        out_specs=pl.BlockSpec(blk, lambda b: (b, 0)),
        compiler_params=pltpu.CompilerParams(
            dimension_semantics=("arbitrary",),
            vmem_limit_bytes=_vmem_limit(2 * _nbytes((lb, D_HEADS), F32) + 4 * _nbytes(blk, BF16),
                                         _nbytes((N_HEADS, t_new, BAND_W_S), F32) + 4 * MIB)),
        name="band_sample",
    )(zq, zq, zq, cache_k, cache_v, mb)


def _merge_kernel(oa_ref, ob_ref, za_ref, zb_ref, x_ref, gt_ref, woa_ref, wob_ref, wout_ref, o_ref, *, nb, t):
    a = _dot(oa_ref[...], woa_ref[...])
    b = _dot(ob_ref[...], wob_ref[...])
    mix = _sigmoid(za_ref[...].astype(F32)) * a + _sigmoid(zb_ref[...].astype(F32)) * b
    y = _dot(mix.astype(BF16), wout_ref[...])
    x = x_ref[...]
    y, gt = _per_batch(y, gt_ref, nb, t)
    x3, _ = _per_batch(x, gt_ref, nb, t)
    o_ref[...] = (x3 + gt * y).reshape(x.shape)


def _merge(oa, ob, zq, x2d, gate, w_oa, w_ob, w_out, *, nb, t, tiles_per_batch):
    m, d = x2d.shape
    tm = nb * t
    za_blk = Z_COLS // d
    if nb == 1:
        mod_map = lambda i: (i // tiles_per_batch, 0, 0)
    else:
        mod_map = lambda i: (i, 0, 0)
    const = lambda i: (0, 0)
    pipelined = (2 * _nbytes((tm, D_HEADS), BF16) + 2 * _nbytes((tm, d), BF16) + 2 * _nbytes((tm, d), F32))
    resident = (2 * (2 * _nbytes((D_HEADS, d), BF16) + _nbytes((d, d), BF16)) + 4 * _nbytes((tm, d), F32))
    return pl.pallas_call(
        functools.partial(_merge_kernel, nb=nb, t=t),
        out_shape=jax.ShapeDtypeStruct((m, d), F32),
        grid=(m // tm,),
        in_specs=[pl.BlockSpec((tm, D_HEADS), lambda i: (i, 0)),
                  pl.BlockSpec((tm, D_HEADS), lambda i: (i, 0)),
                  pl.BlockSpec((tm, d), lambda i: (i, za_blk)),
                  pl.BlockSpec((tm, d), lambda i: (i, za_blk + 1)),
                  pl.BlockSpec((tm, d), lambda i: (i, 0)),
                  pl.BlockSpec((nb, 1, d), mod_map),
                  pl.BlockSpec((D_HEADS, d), const),
                  pl.BlockSpec((D_HEADS, d), const),
                  pl.BlockSpec((d, d), const)],
        out_specs=pl.BlockSpec((tm, d), lambda i: (i, 0)),
        compiler_params=pltpu.CompilerParams(
            dimension_semantics=("arbitrary",),
            vmem_limit_bytes=_vmem_limit(pipelined, resident)),
        name="merge_out",
    )(oa, ob, zq, zq, x2d, gate, w_oa, w_ob, w_out)


def _ffn_kernel(x_ref, sh_ref, sc_ref, gt_ref, g_ref, gf_ref, wg_ref, wu_ref, wd_ref, o_ref,
                h_scr, acc_scr, *, nb, t):
    f = pl.program_id(1)

    @pl.when(f == 0)
    def _():
        h_scr[...] = _mod_norm(x_ref[...], g_ref, sc_ref, sh_ref, nb, t).astype(BF16)

    h = h_scr[...]
    gate = _dot(h, wg_ref[...])
    up = _dot(h, wu_ref[...])
    act = (gate * _sigmoid(gate) * up).astype(BF16)
    part = _dot(act, wd_ref[...])

    @pl.when(f == 0)
    def _():
        acc_scr[...] = part

    @pl.when(f > 0)
    def _():
        acc_scr[...] += part

    @pl.when(f == pl.num_programs(1) - 1)
    def _():
        x = x_ref[...]
        y, gt = _per_batch(acc_scr[...], gt_ref, nb, t)
        x3, _ = _per_batch(x, gt_ref, nb, t)
        x2 = (x3 + gt * y).reshape(x.shape)
        o_ref[...] = _rms_scale(x2) * gf_ref[...]


def _ffn(x2d, shift, scale, gate, g_ffn, g_final, w_gate, w_up, w_down, *, nb, t, tiles_per_batch):
    m, d = x2d.shape
    dff = w_gate.shape[1]
    tm = nb * t
    tf = 512
    if nb == 1:
        mod_map = lambda i, f: (i // tiles_per_batch, 0, 0)
    else:
        mod_map = lambda i, f: (i, 0, 0)
    pipelined = 2 * _nbytes((tm, d), F32) + 3 * _nbytes((d, tf), BF16)
    resident = _nbytes((tm, d), BF16) + _nbytes((tm, d), F32) + 3 * _nbytes((tm, tf), F32) + 2 * _nbytes((tm, d), F32)
    return pl.pallas_call(
        functools.partial(_ffn_kernel, nb=nb, t=t),
        out_shape=jax.ShapeDtypeStruct((m, d), F32),
        grid=(m // tm, dff // tf),
        in_specs=[pl.BlockSpec((tm, d), lambda i, f: (i, 0)),
                  pl.BlockSpec((nb, 1, d), mod_map),
                  pl.BlockSpec((nb, 1, d), mod_map),
                  pl.BlockSpec((nb, 1, d), mod_map),
                  pl.BlockSpec((1, d), lambda i, f: (0, 0)),
                  pl.BlockSpec((1, d), lambda i, f: (0, 0)),
                  pl.BlockSpec((d, tf), lambda i, f: (0, f)),
                  pl.BlockSpec((d, tf), lambda i, f: (0, f)),
                  pl.BlockSpec((tf, d), lambda i, f: (f, 0))],
        out_specs=pl.BlockSpec((tm, d), lambda i, f: (i, 0)),
        scratch_shapes=[pltpu.VMEM((tm, d), BF16), pltpu.VMEM((tm, d), F32)],
        compiler_params=pltpu.CompilerParams(
            dimension_semantics=("arbitrary", "arbitrary"),
            vmem_limit_bytes=_vmem_limit(pipelined, resident)),
        name="ffn_final",
    )(x2d, shift, scale, gate, g_ffn, g_final, w_gate, w_up, w_down)


def _row_tile(s_len, cap):
    tile = min(cap, s_len)
    assert s_len % tile == 0
    return tile


def kernel(x_prompt, x_sample, cache_fox_k, cache_fox_v, cache_fox_logf, cache_band_k, cache_band_v,
           c_prompt, c_sample, w_ada, b_ada, g_mix, w_in, b_f, rel_bias, w_oa, w_ob, w_out,
           g_ffn, w_gate, w_up, w_down, g_final):
    n_b, s_len, d = x_prompt.shape
    n_db, t_new, _ = x_sample.shape
    past = cache_fox_k.shape[2]
    lb = cache_band_k.shape[2]
    assert w_ada.shape[0] == 1, "single-layer trunk"
    assert lb == BAND_PAD and s_len >= BAND_PAD and s_len % FOX_TQ == 0
    assert Z_COLS % d == 0 and t_new % 8 == 0 and t_new + lb <= BAND_W_S and t_new <= CHUNK
    layer = 0
    q_scale = LOG2E / math.sqrt(HEAD_DIM)

    w_l = w_in[layer]
    w_main = jnp.concatenate([w_l[:, :3 * D_HEADS], w_l[:, 3 * D_HEADS + N_HEADS:]], axis=1).astype(BF16)
    w_f = jnp.pad(w_l[:, 3 * D_HEADS:3 * D_HEADS + N_HEADS], ((0, 0), (0, LANES - N_HEADS))).astype(BF16)
    n_main = w_main.shape[1]
    col = jnp.arange(n_main)
    is_q = (col < D_HEADS) | ((col >= 3 * D_HEADS) & (col < 4 * D_HEADS))
    col_scale = jnp.where(is_q, q_scale, 1.0).astype(F32).reshape(1, n_main)
    b_f2 = b_f[layer].reshape(1, N_HEADS)
    g_mix2 = g_mix[layer].reshape(1, d)
    g_ffn2 = g_ffn[layer].reshape(1, d)
    g_fin2 = g_final.reshape(1, d)
    w_oa_b, w_ob_b, w_out_b = w_oa[layer].astype(BF16), w_ob[layer].astype(BF16), w_out[layer].astype(BF16)
    w_gate_b, w_up_b, w_down_b = w_gate[layer].astype(BF16), w_up[layer].astype(BF16), w_down[layer].astype(BF16)

    mods = _ada(jnp.concatenate([c_prompt, c_sample], axis=0), w_ada[layer], b_ada[layer])

    def chunks(rows):
        return [rows[:, k * d:(k + 1) * d][:, None, :] for k in range(6)]

    sh1p, sc1p, gt1p, sh2p, sc2p, gt2p = chunks(mods[:n_b])
    sh1s, sc1s, gt1s, sh2s, sc2s, gt2s = chunks(mods[n_b:])
    mbt = _band_bias(rel_bias[layer], tq=BAND_TQ, transposed=True)
    mbs = _band_bias(rel_bias[layer], tq=BAND_TQ_S, transposed=False)

    xp = x_prompt.reshape(n_b * s_len, d)
    tm = _row_tile(s_len, 1024)
    zq, ka, va, kb, vb, lf = _mixer_in(xp, sh1p, sc1p, g_mix2, w_main, col_scale, w_f, b_f2,
                                       nb=1, t=tm, tiles_per_batch=s_len // tm, tail=BAND_PAD)
    lf_rows = lf.reshape(n_b, s_len, N_HEADS).transpose(0, 2, 1).reshape(n_b * N_HEADS, s_len)
    f_rows = _cumsum_rows(lf_rows).reshape(n_b * N_HEADS, 1, s_len)
    oa = _fox_prompt(zq, f_rows, n_batch=n_b, s_len=s_len)
    ob = _band_prompt(zq, mbt, n_batch=n_b, s_len=s_len)
    tm = _row_tile(s_len, 256)
    x1 = _merge(oa, ob, zq, xp, gt1p, w_oa_b, w_ob_b, w_out_b, nb=1, t=tm, tiles_per_batch=s_len // tm)
    tm = _row_tile(s_len, 512)
    y_prompt = _ffn(x1, sh2p, sc2p, gt2p, g_ffn2, g_fin2, w_gate_b, w_up_b, w_down_b,
                    nb=1, t=tm, tiles_per_batch=s_len // tm).reshape(n_b, s_len, d)
    fox_shape = (1, n_b, s_len, N_HEADS, HEAD_DIM)
    band_shape = (1, n_b, BAND_PAD, N_HEADS, HEAD_DIM)
    prompt_out = (ka.reshape(fox_shape), va.reshape(fox_shape), lf.reshape(1, n_b, s_len, N_HEADS),
                  kb.reshape(band_shape), vb.reshape(band_shape))

    xs = x_sample.reshape(n_db * t_new, d)
    zqs, kas, vas, kbs, vbs, lfs = _mixer_in(xs, sh1s, sc1s, g_mix2, w_main, col_scale, w_f, b_f2,
                                             nb=n_db, t=t_new, tiles_per_batch=1, tail=n_db * t_new)
    lp = -(-(past + t_new) // 2048) * 2048
    lf_all = jnp.concatenate([cache_fox_logf[layer].transpose(0, 2, 1),
                              lfs.reshape(n_db, t_new, N_HEADS).transpose(0, 2, 1),
                              jnp.zeros((n_db, N_HEADS, lp - past - t_new), F32)], axis=-1)
    f_all = _cumsum_rows(lf_all.reshape(n_db * N_HEADS, lp)).reshape(n_db, N_HEADS, lp)
    oas = _fox_sample(zqs, cache_fox_k[layer], cache_fox_v[layer], f_all,
                      n_batch=n_db, t_new=t_new, past=past)
    obs = _band_sample(zqs, cache_band_k[layer], cache_band_v[layer], mbs, n_batch=n_db, t_new=t_new)
    x1s = _merge(oas, obs, zqs, xs, gt1s, w_oa_b, w_ob_b, w_out_b, nb=n_db, t=t_new, tiles_per_batch=1)
    y_sample = _ffn(x1s, sh2s, sc2s, gt2s, g_ffn2, g_fin2, w_gate_b, w_up_b, w_down_b,
                    nb=n_db, t=t_new, tiles_per_batch=1).reshape(n_db, t_new, d)
    s_shape = (1, n_db, t_new, N_HEADS, HEAD_DIM)
    sample_out = (kas.reshape(s_shape), vas.reshape(s_shape), lfs.reshape(1, n_db, t_new, N_HEADS),
                  kbs.reshape(s_shape), vbs.reshape(s_shape))

    return (y_prompt, y_sample) + prompt_out + sample_out
```

```python
import functools
import math

import jax
import jax.numpy as jnp
from jax import lax
from jax.experimental import pallas as pl
from jax.experimental.pallas import tpu as pltpu

F32 = jnp.float32
BF16 = jnp.bfloat16

HEAD_DIM = 128
N_HEADS = 8
D_HEADS = N_HEADS * HEAD_DIM
CHUNK = 64
N_PREV_CHUNKS = 8
BAND_PAD = N_PREV_CHUNKS * CHUNK
REL_CLIP = 128
RMS_EPS = 1e-6
NEG_INF = -1e30
LOG2E = math.log2(math.e)
LANES = 128
Z_COLS = 6 * D_HEADS

VMEM_CAP_BYTES = 60 * 1024 * 1024
MIB = 1024 * 1024

BAND_TQ = 256
BAND_W = BAND_PAD + BAND_TQ
BAND_TQ_S = 128
BAND_W_S = BAND_PAD + BAND_TQ_S
FOX_TQ = 512
FOX_TK = 256


def _vmem_limit(pipelined_bytes, resident_bytes=0):
    est = 2 * pipelined_bytes + resident_bytes + 8 * MIB
    return int(min(max(est, 16 * MIB), VMEM_CAP_BYTES))


def _nbytes(shape, dtype):
    return math.prod(shape) * jnp.dtype(dtype).itemsize


def _dot(a, b):
    return jnp.dot(a, b, preferred_element_type=F32)


def _dot_nt(a, b):
    return lax.dot_general(a, b, (((1,), (1,)), ((), ())), preferred_element_type=F32)


def _split3(x):
    hi = x.astype(BF16)
    r1 = x - hi.astype(F32)
    mid = r1.astype(BF16)
    lo = (r1 - mid.astype(F32)).astype(BF16)
    return hi, mid, lo


def _sigmoid(x):
    return 1.0 / (1.0 + jnp.exp(-x))


def _rms_scale(x):
    return x * lax.rsqrt(jnp.mean(x * x, axis=-1, keepdims=True) + RMS_EPS)


def _per_batch(y, ref, nb, t):
    if nb == 1:
        return y, ref[0]
    return y.reshape(nb, t, y.shape[-1]), ref[...]


def _mod_norm(x, g_ref, sc_ref, sh_ref, nb, t):
    y = _rms_scale(x) * g_ref[...]
    y, sc = _per_batch(y, sc_ref, nb, t)
    _, sh = _per_batch(x, sh_ref, nb, t)
    h = y * (1.0 + sc) + sh
    return h.reshape(x.shape)


def _col_bcast(frow, n):
    r = lax.broadcasted_iota(jnp.int32, (n, n), 0)
    c = lax.broadcasted_iota(jnp.int32, (n, n), 1)
    d = jnp.where(r == c, jnp.broadcast_to(frow, (n, n)), 0.0)
    ones = jnp.ones((n, LANES), BF16)
    hi, mid, lo = _split3(d)
    return _dot(hi, ones) + _dot(mid, ones) + _dot(lo, ones)


def _tile_lanes(x, n):
    reps = n // x.shape[1]
    return x if reps == 1 else jnp.concatenate([x] * reps, axis=1)


def _transpose_bf16(x):
    return x.astype(F32).T.astype(BF16)


def _ada_kernel(c_ref, w_ref, b_ref, o_ref):
    c = c_ref[...]
    a = (c * _sigmoid(c)).astype(BF16)
    o_ref[...] = _dot(a, w_ref[...].astype(BF16)) + b_ref[...]


def _ada(c, w, b):
    n, d = c.shape
    nout = w.shape[1]
    tn = 1024 if nout % 1024 == 0 else 512
    return pl.pallas_call(
        _ada_kernel,
        out_shape=jax.ShapeDtypeStruct((n, nout), F32),
        grid=(nout // tn,),
        in_specs=[pl.BlockSpec((n, d), lambda j: (0, 0)),
                  pl.BlockSpec((d, tn), lambda j: (0, j)),
                  pl.BlockSpec((1, tn), lambda j: (0, j))],
        out_specs=pl.BlockSpec((n, tn), lambda j: (0, j)),
        compiler_params=pltpu.CompilerParams(
            dimension_semantics=("arbitrary",),
            vmem_limit_bytes=_vmem_limit(_nbytes((d, tn), F32) + _nbytes((n, tn), F32),
                                         _nbytes((n, d), F32) + _nbytes((d, tn), BF16))),
        name="ada",
    )(c, w, b.reshape(1, nout))


def _mixer_kernel(x_ref, sh_ref, sc_ref, g_ref, w_ref, cs_ref, wf_ref, bf_ref,
                  zq_ref, ka_ref, va_ref, kb_ref, vb_ref, lf_ref, h_scr,
                  *, nb, t, tiles_per_batch, tail, npg):
    i = pl.program_id(0)
    j = pl.program_id(1)

    @pl.when(j == 0)
    def _():
        h = _mod_norm(x_ref[...], g_ref, sc_ref, sh_ref, nb, t).astype(BF16)
        h_scr[...] = h
        fa = _dot(h, wf_ref[...])[:, :N_HEADS] + bf_ref[...]
        lf_ref[...] = jnp.minimum(fa, 0.0) - jnp.log1p(jnp.exp(-jnp.abs(fa)))

    acc = _dot(h_scr[...], w_ref[...])
    zq_ref[...] = (acc * cs_ref[...]).astype(BF16)

    def _group(g):
        return (j >= g * npg) & (j < (g + 1) * npg)

    @pl.when(_group(1))
    def _():
        ka_ref[...] = acc

    @pl.when(_group(2))
    def _():
        va_ref[...] = acc

    is_tail = (i % tiles_per_batch) == (tiles_per_batch - 1)
    rows = acc.shape[0]

    @pl.when(is_tail & _group(4))
    def _():
        kb_ref[...] = acc[rows - tail:, :]

    @pl.when(is_tail & _group(5))
    def _():
        vb_ref[...] = acc[rows - tail:, :]


def _mixer_in(x2d, shift, scale, g, w_main, col_scale, w_f, b_f, *, nb, t, tiles_per_batch, tail):
    m, d = x2d.shape
    n = w_main.shape[1]
    tm = nb * t
    tn = 512
    npg = D_HEADS // tn
    n_batches = (m // tm) // tiles_per_batch
    if nb == 1:
        mod_map = lambda i, j: (i // tiles_per_batch, 0, 0)
    else:
        mod_map = lambda i, j: (i, 0, 0)

    def grp_map(gidx):
        return lambda i, j: (i, jnp.clip(j - gidx * npg, 0, npg - 1))

    def tail_map(gidx):
        def f(i, j):
            is_tail = (i % tiles_per_batch) == (tiles_per_batch - 1)
            return (i // tiles_per_batch, jnp.where(is_tail, jnp.clip(j - gidx * npg, 0, npg - 1), 0))
        return f

    kern = functools.partial(_mixer_kernel, nb=nb, t=t, tiles_per_batch=tiles_per_batch, tail=tail, npg=npg)
    pipelined = (_nbytes((tm, d), F32) + _nbytes((d, tn), BF16) + _nbytes((tm, tn), BF16)
                 + 2 * _nbytes((tm, tn), F32) + 2 * _nbytes((tail, tn), F32))
    resident = _nbytes((tm, d), BF16) + 3 * _nbytes((tm, d), F32) // 2
    return pl.pallas_call(
        kern,
        out_shape=(jax.ShapeDtypeStruct((m, n), BF16),
                   jax.ShapeDtypeStruct((m, D_HEADS), F32),
                   jax.ShapeDtypeStruct((m, D_HEADS), F32),
                   jax.ShapeDtypeStruct((n_batches * tail, D_HEADS), F32),
                   jax.ShapeDtypeStruct((n_batches * tail, D_HEADS), F32),
                   jax.ShapeDtypeStruct((m, N_HEADS), F32)),
        grid=(m // tm, n // tn),
        in_specs=[pl.BlockSpec((tm, d), lambda i, j: (i, 0)),
                  pl.BlockSpec((nb, 1, d), mod_map),
                  pl.BlockSpec((nb, 1, d), mod_map),
                  pl.BlockSpec((1, d), lambda i, j: (0, 0)),
                  pl.BlockSpec((d, tn), lambda i, j: (0, j)),
                  pl.BlockSpec((1, tn), lambda i, j: (0, j)),
                  pl.BlockSpec((d, LANES), lambda i, j: (0, 0)),
                  pl.BlockSpec((1, N_HEADS), lambda i, j: (0, 0))],
        out_specs=(pl.BlockSpec((tm, tn), lambda i, j: (i, j)),
                   pl.BlockSpec((tm, tn), grp_map(1)),
                   pl.BlockSpec((tm, tn), grp_map(2)),
                   pl.BlockSpec((tail, tn), tail_map(4)),
                   pl.BlockSpec((tail, tn), tail_map(5)),
                   pl.BlockSpec((tm, N_HEADS), lambda i, j: (i, 0))),
        scratch_shapes=[pltpu.VMEM((tm, d), BF16)],
        compiler_params=pltpu.CompilerParams(
            dimension_semantics=("arbitrary", "arbitrary"),
            vmem_limit_bytes=_vmem_limit(pipelined, resident)),
        name="mixer_in",
    )(x2d, shift, scale, g, w_main, col_scale, w_f, b_f)


def _cumsum_kernel(x_ref, o_ref):
    grp, nr, _ = x_ref.shape
    r = lax.broadcasted_iota(jnp.int32, (LANES, LANES), 0)
    c = lax.broadcasted_iota(jnp.int32, (LANES, LANES), 1)
    upper = (r <= c).astype(BF16)
    rr = lax.broadcasted_iota(jnp.int32, (nr, nr), 0)
    cc = lax.broadcasted_iota(jnp.int32, (nr, nr), 1)
    lower = (cc < rr).astype(BF16)
    for gi in range(grp):
        hi, mid, lo = _split3(x_ref[gi])
        within = _dot(hi, upper) + _dot(mid, upper) + _dot(lo, upper)
        tot = jnp.broadcast_to(within[:, LANES - 1:LANES], (nr, LANES))
        hi, mid, lo = _split3(tot)
        before = _dot(lower, hi) + _dot(lower, mid) + _dot(lower, lo)
        o_ref[gi] = within + before


def _cumsum_rows(x):
    rows, length = x.shape
    nr = length // LANES
    grp = 8
    x3 = x.reshape(rows, nr, LANES)
    out = pl.pallas_call(
        _cumsum_kernel,
        out_shape=jax.ShapeDtypeStruct(x3.shape, F32),
        grid=(rows // grp,),
        in_specs=[pl.BlockSpec((grp, nr, LANES), lambda i: (i, 0, 0))],
        out_specs=pl.BlockSpec((grp, nr, LANES), lambda i: (i, 0, 0)),
        compiler_params=pltpu.CompilerParams(dimension_semantics=("arbitrary",)),
        name="cumsum_logf",
    )(x3)
    return out.reshape(rows, length)


def _fox_prompt_kernel(q_ref, k_ref, v_ref, f_ref, o_ref, qxt_scr, kx_scr, vt_scr, acc_scr, s_scr, pa_scr, pb_scr,
                       *, tq, tk):
    s_len = q_ref.shape[0]
    assert tq == 2 * tk
    sub = lax.broadcasted_iota(jnp.int32, (HEAD_DIM, tk), 0)
    minus_ones = jnp.where(sub < 3, -1.0, 0.0).astype(BF16)

    def setup(ci, carry):
        c0 = pl.multiple_of(ci * tk, tk)
        rows = pl.ds(c0, tk)
        qxt_scr[0:HEAD_DIM, rows] = _transpose_bf16(q_ref[rows, :])
        qxt_scr[HEAD_DIM:, rows] = minus_ones
        vt_scr[:, rows] = _transpose_bf16(v_ref[rows, :])
        hi, mid, lo = [part.astype(F32) for part in _split3(f_ref[:, rows] * LOG2E)]
        ext_t = jnp.where(sub == 0, hi, jnp.where(sub == 1, mid, jnp.where(sub == 2, lo, 0.0)))
        kx_scr[rows, 0:HEAD_DIM] = k_ref[rows, :]
        kx_scr[rows, HEAD_DIM:] = ext_t.T.astype(BF16)
        return carry

    lax.fori_loop(0, s_len // tk, setup, 0)

    key_ix = lax.broadcasted_iota(jnp.int32, (tk, tq), 0)
    qry_ix = lax.broadcasted_iota(jnp.int32, (tk, tq), 1)

    def q_body(qi, carry):
        r0 = pl.multiple_of(qi * tq, tq)
        f_q = f_ref[:, pl.ds(r0, tq)] * LOG2E

        def scores(c0):
            c0 = pl.multiple_of(c0, tk)
            return _dot(kx_scr[pl.ds(c0, tk), :], qxt_scr[:, pl.ds(r0, tq)])

        def softmax(s, m_old, l_old, p_ref):
            t_max = jnp.max(s, axis=0, keepdims=True) + f_q
            if m_old is None:
                m_new, alpha = t_max, None
            else:
                m_new = jnp.maximum(m_old, t_max)
                alpha = jnp.exp2(m_old - m_new)
            p = jnp.exp2(s - (m_new - f_q))
            p_sum = jnp.sum(p, axis=0, keepdims=True)
            p_ref[...] = p.astype(BF16)
            return m_new, (p_sum if m_old is None else alpha * l_old + p_sum), alpha

        def accumulate(c0, p_ref, alpha):
            c0 = pl.multiple_of(c0, tk)
            pv = _dot(vt_scr[:, pl.ds(c0, tk)], p_ref[...])
            acc_scr[...] = pv if alpha is None else alpha * acc_scr[...] + pv

        d0 = jnp.where(key_ix <= qry_ix, scores(r0), NEG_INF)
        d1 = jnp.where(key_ix + tk <= qry_ix, scores(r0 + tk), NEG_INF)
        s_scr[...] = scores(0)
        m, l, _ = softmax(d0, None, None, pa_scr)
        accumulate(r0, pa_scr, None)
        m, l, alpha = softmax(d1, m, l, pb_scr)

        def k_body(kb, state):
            c_pend, alpha_pend, m, l = state
            c0 = kb * tq
            s_1 = scores(c0 + tk)
            accumulate(c_pend, pb_scr, alpha_pend)
            m, l, alpha_0 = softmax(s_scr[...], m, l, pa_scr)
            s_scr[...] = scores(c0 + tq)
            accumulate(c0, pa_scr, alpha_0)
            m, l, alpha_1 = softmax(s_1, m, l, pb_scr)
            return c0 + tk, alpha_1, m, l

        c_pend, alpha_pend, m, l = lax.fori_loop(0, qi, k_body, (r0 + tk, alpha, m, l))
        accumulate(c_pend, pb_scr, alpha_pend)
        o_t = acc_scr[...] * (1.0 / l)
        o_ref[pl.ds(r0, tq), :] = o_t.T.astype(BF16)
        return carry

    lax.fori_loop(0, s_len // tq, q_body, 0)


def _fox_prompt(zq, f_rows, *, n_batch, s_len):
    tq = min(FOX_TQ, s_len)
    tk = min(FOX_TK, tq)
    blk = (s_len, HEAD_DIM)
    scratch = [pltpu.VMEM((2 * HEAD_DIM, s_len), BF16), pltpu.VMEM((s_len, 2 * HEAD_DIM), BF16),
               pltpu.VMEM((HEAD_DIM, s_len), BF16), pltpu.VMEM((HEAD_DIM, tq), F32),
               pltpu.VMEM((tk, tq), F32), pltpu.VMEM((tk, tq), BF16), pltpu.VMEM((tk, tq), BF16)]
    resident = 5 * _nbytes(blk, BF16) + 8 * _nbytes((tk, tq), F32)
    return pl.pallas_call(
        functools.partial(_fox_prompt_kernel, tq=tq, tk=tk),
        out_shape=jax.ShapeDtypeStruct((n_batch * s_len, D_HEADS), BF16),
        grid=(n_batch, N_HEADS),
        in_specs=[pl.BlockSpec(blk, lambda b, h: (b, h)),
                  pl.BlockSpec(blk, lambda b, h: (b, N_HEADS + h)),
                  pl.BlockSpec(blk, lambda b, h: (b, 2 * N_HEADS + h)),
                  pl.BlockSpec((None, 1, s_len), lambda b, h: (b * N_HEADS + h, 0, 0))],
        out_specs=pl.BlockSpec(blk, lambda b, h: (b, h)),
        scratch_shapes=scratch,
        compiler_params=pltpu.CompilerParams(
            dimension_semantics=("arbitrary", "arbitrary"),
            vmem_limit_bytes=_vmem_limit(4 * _nbytes(blk, BF16) + _nbytes((8, s_len), F32), resident)),
        name="fox_prompt",
    )(zq, zq, zq, f_rows)


def _band_bias_kernel(tbl_ref, o_ref, *, tq, width, transposed):
    h = pl.program_id(0)
    near_w = width - (BAND_PAD - REL_CLIP)
    if transposed:
        shape, near_shape, q_ax, k_ax = (width, tq), (near_w, tq), 1, 0
    else:
        shape, near_shape, q_ax, k_ax = (tq, width), (tq, near_w), 0, 1
    r = lax.broadcasted_iota(jnp.int32, near_shape, q_ax)
    c = lax.broadcasted_iota(jnp.int32, near_shape, k_ax) + (width - near_w)
    idx = jnp.clip(BAND_PAD + r - c, -REL_CLIP, REL_CLIP) + REL_CLIP
    far = tbl_ref[h, 2 * REL_CLIP]
    near = jnp.full(near_shape, far, F32)
    for tix in range(1, 2 * REL_CLIP):
        near = jnp.where(idx == tix, tbl_ref[h, tix], near)
    far_shape = (width - near_w, tq) if transposed else (tq, width - near_w)
    bias = jnp.concatenate([jnp.full(far_shape, far, F32), near], axis=k_ax)
    rf = lax.broadcasted_iota(jnp.int32, shape, q_ax) // CHUNK
    cf = lax.broadcasted_iota(jnp.int32, shape, k_ax) // CHUNK
    visible = (cf >= rf) & (cf <= rf + N_PREV_CHUNKS)
    o_ref[...] = jnp.where(visible, bias * LOG2E, NEG_INF)


def _band_bias(table, *, tq, transposed):
    width = BAND_PAD + tq
    shape = (width, tq) if transposed else (tq, width)
    return pl.pallas_call(
        functools.partial(_band_bias_kernel, tq=tq, width=width, transposed=transposed),
        out_shape=jax.ShapeDtypeStruct((N_HEADS,) + shape, F32),
        grid=(N_HEADS,),
        in_specs=[pl.BlockSpec(memory_space=pltpu.SMEM)],
        out_specs=pl.BlockSpec((None,) + shape, lambda h: (h, 0, 0)),
        compiler_params=pltpu.CompilerParams(dimension_semantics=("arbitrary",)),
        name="band_bias",
    )(table.T)


def _band_prompt_kernel(q_ref, k_ref, v_ref, mb_ref, o_ref, qt_scr, kpad, vtpad, s_scr, pa_scr, pb_scr):
    s_len = q_ref.shape[0]
    kpad[0:BAND_PAD, :] = jnp.zeros((BAND_PAD, HEAD_DIM), BF16)
    kpad[BAND_PAD:, :] = k_ref[...]
    vtpad[:, 0:BAND_PAD] = jnp.zeros((HEAD_DIM, BAND_PAD), BF16)

    def setup(ci, carry):
        c0 = pl.multiple_of(ci * BAND_TQ, BAND_TQ)
        qt_scr[:, pl.ds(c0, BAND_TQ)] = _transpose_bf16(q_ref[pl.ds(c0, BAND_TQ), :])
        vtpad[:, pl.ds(pl.multiple_of(BAND_PAD + c0, BAND_TQ), BAND_TQ)] = _transpose_bf16(v_ref[pl.ds(c0, BAND_TQ), :])
        return carry

    n_tiles = s_len // BAND_TQ
    lax.fori_loop(0, n_tiles, setup, 0)
    key_ix = lax.broadcasted_iota(jnp.int32, (BAND_W, BAND_TQ), 0)

    def scores(ti):
        r0 = pl.multiple_of(ti * BAND_TQ, BAND_TQ)
        return _dot(kpad[pl.ds(r0, BAND_W), :], qt_scr[:, pl.ds(r0, BAND_TQ)]) + mb_ref[...]

    def softmax(s, p_ref):
        m = jnp.max(s, axis=0, keepdims=True)
        p = jnp.exp2(s - m)
        p_ref[...] = p.astype(BF16)
        return 1.0 / jnp.sum(p, axis=0, keepdims=True)

    def emit(ti, p_ref, r_den):
        r0 = pl.multiple_of(ti * BAND_TQ, BAND_TQ)
        o_t = _dot(vtpad[:, pl.ds(r0, BAND_W)], p_ref[...]) * r_den
        o_ref[pl.ds(r0, BAND_TQ), :] = o_t.T.astype(BF16)

    assert BAND_PAD // BAND_TQ == 2
    h0 = jnp.where(key_ix >= BAND_PAD, scores(0), NEG_INF)
    h1 = jnp.where(BAND_TQ + key_ix >= BAND_PAD, scores(1), NEG_INF)
    s_scr[...] = scores(2)
    r_0 = softmax(h0, pa_scr)
    emit(0, pa_scr, r_0)
    r_1 = softmax(h1, pb_scr)

    def pair(pi, r_pend):
        a = 2 + 2 * pi
        s_b = scores(a + 1)
        emit(a - 1, pb_scr, r_pend)
        r_a = softmax(s_scr[...], pa_scr)
        s_scr[...] = scores(jnp.minimum(a + 2, n_tiles - 1))
        emit(a, pa_scr, r_a)
        return softmax(s_b, pb_scr)

    r_last = lax.fori_loop(0, (n_tiles - 2) // 2, pair, r_1)
    emit(n_tiles - 1, pb_scr, r_last)


def _band_prompt(zq, mbt, *, n_batch, s_len):
    blk = (s_len, HEAD_DIM)
    off = 3 * N_HEADS
    assert (s_len // BAND_TQ - BAND_PAD // BAND_TQ) % 2 == 0
    scratch = [pltpu.VMEM((HEAD_DIM, s_len), BF16), pltpu.VMEM((BAND_PAD + s_len, HEAD_DIM), BF16),
               pltpu.VMEM((HEAD_DIM, BAND_PAD + s_len), BF16), pltpu.VMEM((BAND_W, BAND_TQ), F32),
               pltpu.VMEM((BAND_W, BAND_TQ), BF16), pltpu.VMEM((BAND_W, BAND_TQ), BF16)]
    resident = 3 * _nbytes((BAND_PAD + s_len, HEAD_DIM), BF16) + 8 * _nbytes((BAND_W, BAND_TQ), F32)
    return pl.pallas_call(
        _band_prompt_kernel,
        out_shape=jax.ShapeDtypeStruct((n_batch * s_len, D_HEADS), BF16),
        grid=(n_batch, N_HEADS),
        in_specs=[pl.BlockSpec(blk, lambda b, h: (b, off + h)),
                  pl.BlockSpec(blk, lambda b, h: (b, off + N_HEADS + h)),
                  pl.BlockSpec(blk, lambda b, h: (b, off + 2 * N_HEADS + h)),
                  pl.BlockSpec((None, BAND_W, BAND_TQ), lambda b, h: (h, 0, 0))],
        out_specs=pl.BlockSpec(blk, lambda b, h: (b, h)),
        scratch_shapes=scratch,
        compiler_params=pltpu.CompilerParams(
            dimension_semantics=("arbitrary", "arbitrary"),
            vmem_limit_bytes=_vmem_limit(4 * _nbytes(blk, BF16) + _nbytes((BAND_W, BAND_TQ), F32), resident)),
        name="band_prompt",
    )(zq, zq, zq, mbt)


def _fox_sample_kernel(q_ref, kn_ref, vn_ref, ck_ref, cv_ref, f_ref, o_ref,
                       m_scr, l_scr, acc_scr, fq_scr, *, past, tk):
    kt = pl.program_id(1)
    t_new = q_ref.shape[0]

    def heads(h):
        return slice(h * HEAD_DIM, (h + 1) * HEAD_DIM)

    @pl.when(kt == 0)
    def _():
        for h in range(N_HEADS):
            fq_scr[h] = _col_bcast(f_ref[h:h + 1, past:past + t_new] * LOG2E, t_new)
        m_scr[...] = jnp.full(m_scr.shape, NEG_INF, F32)
        l_scr[...] = jnp.zeros(l_scr.shape, F32)
        acc_scr[...] = jnp.zeros(acc_scr.shape, F32)

    def update(h, s, v):
        m_old = m_scr[h]
        m_new = jnp.maximum(m_old, jnp.max(s, axis=1, keepdims=True))
        alpha = jnp.exp2(m_old - m_new)
        p = jnp.exp2(s - m_new)
        l_scr[h] = alpha * l_scr[h] + jnp.sum(p, axis=1, keepdims=True)
        acc_scr[h] = alpha * acc_scr[h] + _dot(p.astype(BF16), v)
        m_scr[h] = m_new

    c0 = pl.multiple_of(kt * tk, tk)
    scores = []
    for h in range(N_HEADS):
        k = ck_ref[pl.ds(h, tk, stride=N_HEADS), :].astype(BF16)
        scores.append(_dot_nt(q_ref[:, heads(h)], k) + _tile_lanes(fq_scr[h], tk)
                      - f_ref[h:h + 1, pl.ds(c0, tk)] * LOG2E)
    for h in range(N_HEADS):
        update(h, scores[h], cv_ref[pl.ds(h, tk, stride=N_HEADS), :].astype(BF16))

    @pl.when(kt == pl.num_programs(1) - 1)
    def _():
        r = lax.broadcasted_iota(jnp.int32, (t_new, t_new), 0)
        c = lax.broadcasted_iota(jnp.int32, (t_new, t_new), 1)
        for h in range(N_HEADS):
            q = q_ref[:, heads(h)]
            f_new = f_ref[h:h + 1, past:past + t_new] * LOG2E
            s = _dot_nt(q, kn_ref[:, heads(h)]) + fq_scr[h][:, :t_new] - f_new
            s = jnp.where(c <= r, s, NEG_INF)
            update(h, s, vn_ref[:, heads(h)])
            o_ref[:, heads(h)] = (acc_scr[h] / l_scr[h]).astype(BF16)


def _fox_sample(zq, cache_k, cache_v, f_all, *, n_batch, t_new, past):
    tk = min(1024, past)
    lp = f_all.shape[-1]
    blk = (t_new, D_HEADS)
    cblk = (None, tk * N_HEADS, HEAD_DIM)
    return pl.pallas_call(
        functools.partial(_fox_sample_kernel, past=past, tk=tk),
        out_shape=jax.ShapeDtypeStruct((n_batch * t_new, D_HEADS), BF16),
        grid=(n_batch, past // tk),
        in_specs=[pl.BlockSpec(blk, lambda b, k: (b, 0)),
                  pl.BlockSpec(blk, lambda b, k: (b, 1)),
                  pl.BlockSpec(blk, lambda b, k: (b, 2)),
                  pl.BlockSpec(cblk, lambda b, k: (b, k, 0)),
                  pl.BlockSpec(cblk, lambda b, k: (b, k, 0)),
                  pl.BlockSpec((None, N_HEADS, lp), lambda b, k: (b, 0, 0))],
        out_specs=pl.BlockSpec(blk, lambda b, k: (b, 0)),
        scratch_shapes=[pltpu.VMEM((N_HEADS, t_new, 1), F32), pltpu.VMEM((N_HEADS, t_new, 1), F32),
                        pltpu.VMEM((N_HEADS, t_new, HEAD_DIM), F32),
                        pltpu.VMEM((N_HEADS, t_new, LANES), F32)],
        compiler_params=pltpu.CompilerParams(
            dimension_semantics=("arbitrary", "arbitrary"),
            vmem_limit_bytes=_vmem_limit(2 * _nbytes((tk, D_HEADS), F32) + 4 * _nbytes(blk, BF16)
                                         + _nbytes((N_HEADS, lp), F32), 8 * MIB)),
        name="fox_sample",
    )(zq, zq, zq, cache_k, cache_v, f_all)


def _band_sample_kernel(q_ref, kn_ref, vn_ref, ck_ref, cv_ref, mb_ref, o_ref):
    t_new = q_ref.shape[0]
    lb = ck_ref.shape[0] // N_HEADS
    for h in range(N_HEADS):
        hs = slice(h * HEAD_DIM, (h + 1) * HEAD_DIM)
        q = q_ref[:, hs]
        s1 = _dot_nt(q, ck_ref[pl.ds(h, lb, stride=N_HEADS), :].astype(BF16)) + mb_ref[h, :, 0:lb]
        s2 = _dot_nt(q, kn_ref[:, hs]) + mb_ref[h, :, lb:lb + t_new]
        m = jnp.maximum(jnp.max(s1, axis=1, keepdims=True), jnp.max(s2, axis=1, keepdims=True))
        p1 = jnp.exp2(s1 - m)
        p2 = jnp.exp2(s2 - m)
        den = jnp.sum(p1, axis=1, keepdims=True) + jnp.sum(p2, axis=1, keepdims=True)
        o = _dot(p1.astype(BF16), cv_ref[pl.ds(h, lb, stride=N_HEADS), :].astype(BF16)) + _dot(p2.astype(BF16), vn_ref[:, hs])
        o_ref[:, hs] = (o / den).astype(BF16)


def _band_sample(zq, cache_k, cache_v, mb, *, n_batch, t_new):
    lb = cache_k.shape[1] // N_HEADS
    blk = (t_new, D_HEADS)
    cblk = (None, lb * N_HEADS, HEAD_DIM)
    return pl.pallas_call(
        _band_sample_kernel,
        out_shape=jax.ShapeDtypeStruct((n_batch * t_new, D_HEADS), BF16),
        grid=(n_batch,),
        in_specs=[pl.BlockSpec(blk, lambda b: (b, 3)),
                  pl.BlockSpec(blk, lambda b: (b, 4)),
                  pl.BlockSpec(blk, lambda b: (b, 5)),
                  pl.BlockSpec(cblk, lambda b: (b, 0, 0)),
                  pl.BlockSpec(cblk, lambda b: (b, 0, 0)),
                  pl.BlockSpec((N_HEADS, t_new, BAND_W_S), lambda b: (0, 0, 0))],
        out_specs=pl.BlockSpec(blk, lambda b: (b, 0)),
        compiler_params=pltpu.CompilerParams(
            dimension_semantics=("arbitrary",),
            vmem_limit_bytes=_vmem_limit(2 * _nbytes((lb, D_HEADS), F32) + 4 * _nbytes(blk, BF16),
                                         _nbytes((N_HEADS, t_new, BAND_W_S), F32) + 4 * MIB)),
        name="band_sample",
    )(zq, zq, zq, cache_k, cache_v, mb)


def _merge_kernel(oa_ref, ob_ref, za_ref, zb_ref, x_ref, gt_ref, woa_ref, wob_ref, wout_ref, o_ref, *, nb, t):
    a = _dot(oa_ref[...], woa_ref[...])
    b = _dot(ob_ref[...], wob_ref[...])
    mix = _sigmoid(za_ref[...].astype(F32)) * a + _sigmoid(zb_ref[...].astype(F32)) * b
    y = _dot(mix.astype(BF16), wout_ref[...])
    x = x_ref[...]
    y, gt = _per_batch(y, gt_ref, nb, t)
    x3, _ = _per_batch(x, gt_ref, nb, t)
    o_ref[...] = (x3 + gt * y).reshape(x.shape)


def _merge(oa, ob, zq, x2d, gate, w_oa, w_ob, w_out, *, nb, t, tiles_per_batch):
    m, d = x2d.shape
    tm = nb * t
    za_blk = Z_COLS // d
    if nb == 1:
        mod_map = lambda i: (i // tiles_per_batch, 0, 0)
    else:
        mod_map = lambda i: (i, 0, 0)
    const = lambda i: (0, 0)
    pipelined = (2 * _nbytes((tm, D_HEADS), BF16) + 2 * _nbytes((tm, d), BF16) + 2 * _nbytes((tm, d), F32))
    resident = (2 * (2 * _nbytes((D_HEADS, d), BF16) + _nbytes((d, d), BF16)) + 4 * _nbytes((tm, d), F32))
    return pl.pallas_call(
        functools.partial(_merge_kernel, nb=nb, t=t),
        out_shape=jax.ShapeDtypeStruct((m, d), F32),
        grid=(m // tm,),
        in_specs=[pl.BlockSpec((tm, D_HEADS), lambda i: (i, 0)),
                  pl.BlockSpec((tm, D_HEADS), lambda i: (i, 0)),
                  pl.BlockSpec((tm, d), lambda i: (i, za_blk)),
                  pl.BlockSpec((tm, d), lambda i: (i, za_blk + 1)),
                  pl.BlockSpec((tm, d), lambda i: (i, 0)),
                  pl.BlockSpec((nb, 1, d), mod_map),
                  pl.BlockSpec((D_HEADS, d), const),
                  pl.BlockSpec((D_HEADS, d), const),
                  pl.BlockSpec((d, d), const)],
        out_specs=pl.BlockSpec((tm, d), lambda i: (i, 0)),
        compiler_params=pltpu.CompilerParams(
            dimension_semantics=("arbitrary",),
            vmem_limit_bytes=_vmem_limit(pipelined, resident)),
        name="merge_out",
    )(oa, ob, zq, zq, x2d, gate, w_oa, w_ob, w_out)


def _ffn_kernel(x_ref, sh_ref, sc_ref, gt_ref, g_ref, gf_ref, wg_ref, wu_ref, wd_ref, o_ref,
                h_scr, acc_scr, *, nb, t):
    f = pl.program_id(1)

    @pl.when(f == 0)
    def _():
        h_scr[...] = _mod_norm(x_ref[...], g_ref, sc_ref, sh_ref, nb, t).astype(BF16)

    h = h_scr[...]
    gate = _dot(h, wg_ref[...])
    up = _dot(h, wu_ref[...])
    act = (gate * _sigmoid(gate) * up).astype(BF16)
    part = _dot(act, wd_ref[...])

    @pl.when(f == 0)
    def _():
        acc_scr[...] = part

    @pl.when(f > 0)
    def _():
        acc_scr[...] += part

    @pl.when(f == pl.num_programs(1) - 1)
    def _():
        x = x_ref[...]
        y, gt = _per_batch(acc_scr[...], gt_ref, nb, t)
        x3, _ = _per_batch(x, gt_ref, nb, t)
        x2 = (x3 + gt * y).reshape(x.shape)
        o_ref[...] = _rms_scale(x2) * gf_ref[...]


def _ffn(x2d, shift, scale, gate, g_ffn, g_final, w_gate, w_up, w_down, *, nb, t, tiles_per_batch):
    m, d = x2d.shape
    dff = w_gate.shape[1]
    tm = nb * t
    tf = 1408 if dff % 1408 == 0 else 512
    if nb == 1:
        mod_map = lambda i, f: (i // tiles_per_batch, 0, 0)
    else:
        mod_map = lambda i, f: (i, 0, 0)
    pipelined = 3 * _nbytes((d, tf), BF16)
    resident = _nbytes((tm, d), BF16) + _nbytes((tm, d), F32) + 3 * _nbytes((tm, tf), F32) + 2 * _nbytes((tm, d), F32)
    return pl.pallas_call(
        functools.partial(_ffn_kernel, nb=nb, t=t),
        out_shape=jax.ShapeDtypeStruct((m, d), F32),
        grid=(m // tm, dff // tf),
        in_specs=[pl.BlockSpec((tm, d), lambda i, f: (i, 0), pipeline_mode=pl.Buffered(1)),
                  pl.BlockSpec((nb, 1, d), mod_map),
                  pl.BlockSpec((nb, 1, d), mod_map),
                  pl.BlockSpec((nb, 1, d), mod_map),
                  pl.BlockSpec((1, d), lambda i, f: (0, 0)),
                  pl.BlockSpec((1, d), lambda i, f: (0, 0)),
                  pl.BlockSpec((d, tf), lambda i, f: (0, f)),
                  pl.BlockSpec((d, tf), lambda i, f: (0, f)),
                  pl.BlockSpec((tf, d), lambda i, f: (f, 0))],
        out_specs=pl.BlockSpec((tm, d), lambda i, f: (i, 0), pipeline_mode=pl.Buffered(1)),
        scratch_shapes=[pltpu.VMEM((tm, d), BF16), pltpu.VMEM((tm, d), F32)],
        compiler_params=pltpu.CompilerParams(
            dimension_semantics=("arbitrary", "arbitrary"),
            vmem_limit_bytes=_vmem_limit(pipelined, resident)),
        name="ffn_final",
    )(x2d, shift, scale, gate, g_ffn, g_final, w_gate, w_up, w_down)


def _row_tile(s_len, cap):
    tile = min(cap, s_len)
    assert s_len % tile == 0
    return tile


def kernel(x_prompt, x_sample, cache_fox_k, cache_fox_v, cache_fox_logf, cache_band_k, cache_band_v,
           c_prompt, c_sample, w_ada, b_ada, g_mix, w_in, b_f, rel_bias, w_oa, w_ob, w_out,
           g_ffn, w_gate, w_up, w_down, g_final):
    n_b, s_len, d = x_prompt.shape
    n_db, t_new, _ = x_sample.shape
    past = cache_fox_k.shape[2]
    lb = cache_band_k.shape[2]
    assert w_ada.shape[0] == 1, "single-layer trunk"
    assert lb == BAND_PAD and s_len >= BAND_PAD and s_len % FOX_TQ == 0
    assert Z_COLS % d == 0 and t_new % 8 == 0 and t_new + lb <= BAND_W_S and t_new <= CHUNK
    layer = 0
    q_scale = LOG2E / math.sqrt(HEAD_DIM)

    w_l = w_in[layer]
    w_main = jnp.concatenate([w_l[:, :3 * D_HEADS], w_l[:, 3 * D_HEADS + N_HEADS:]], axis=1).astype(BF16)
    w_f = jnp.pad(w_l[:, 3 * D_HEADS:3 * D_HEADS + N_HEADS], ((0, 0), (0, LANES - N_HEADS))).astype(BF16)
    n_main = w_main.shape[1]
    col = jnp.arange(n_main)
    is_q = (col < D_HEADS) | ((col >= 3 * D_HEADS) & (col < 4 * D_HEADS))
    col_scale = jnp.where(is_q, q_scale, 1.0).astype(F32).reshape(1, n_main)
    b_f2 = b_f[layer].reshape(1, N_HEADS)
    g_mix2 = g_mix[layer].reshape(1, d)
    g_ffn2 = g_ffn[layer].reshape(1, d)
    g_fin2 = g_final.reshape(1, d)
    w_oa_b, w_ob_b, w_out_b = w_oa[layer].astype(BF16), w_ob[layer].astype(BF16), w_out[layer].astype(BF16)
    w_gate_b, w_up_b, w_down_b = w_gate[layer].astype(BF16), w_up[layer].astype(BF16), w_down[layer].astype(BF16)

    mods = _ada(jnp.concatenate([c_prompt, c_sample], axis=0), w_ada[layer], b_ada[layer])

    def chunks(rows):
        return [rows[:, k * d:(k + 1) * d][:, None, :] for k in range(6)]

    sh1p, sc1p, gt1p, sh2p, sc2p, gt2p = chunks(mods[:n_b])
    sh1s, sc1s, gt1s, sh2s, sc2s, gt2s = chunks(mods[n_b:])
    mbt = _band_bias(rel_bias[layer], tq=BAND_TQ, transposed=True)
    mbs = _band_bias(rel_bias[layer], tq=BAND_TQ_S, transposed=False)

    xp = x_prompt.reshape(n_b * s_len, d)
    tm = _row_tile(s_len, 1024)
    zq, ka, va, kb, vb, lf = _mixer_in(xp, sh1p, sc1p, g_mix2, w_main, col_scale, w_f, b_f2,
                                       nb=1, t=tm, tiles_per_batch=s_len // tm, tail=BAND_PAD)
    lf_rows = lf.reshape(n_b, s_len, N_HEADS).transpose(0, 2, 1).reshape(n_b * N_HEADS, s_len)
    f_rows = _cumsum_rows(lf_rows).reshape(n_b * N_HEADS, 1, s_len)
    oa = _fox_prompt(zq, f_rows, n_batch=n_b, s_len=s_len)
    ob = _band_prompt(zq, mbt, n_batch=n_b, s_len=s_len)
    tm = _row_tile(s_len, 256)
    x1 = _merge(oa, ob, zq, xp, gt1p, w_oa_b, w_ob_b, w_out_b, nb=1, t=tm, tiles_per_batch=s_len // tm)
    tm = _row_tile(s_len, 512)
    y_prompt = _ffn(x1, sh2p, sc2p, gt2p, g_ffn2, g_fin2, w_gate_b, w_up_b, w_down_b,
                    nb=1, t=tm, tiles_per_batch=s_len // tm).reshape(n_b, s_len, d)
    fox_shape = (1, n_b, s_len, N_HEADS, HEAD_DIM)
    band_shape = (1, n_b, BAND_PAD, N_HEADS, HEAD_DIM)
    prompt_out = (ka.reshape(fox_shape), va.reshape(fox_shape), lf.reshape(1, n_b, s_len, N_HEADS),
                  kb.reshape(band_shape), vb.reshape(band_shape))

    xs = x_sample.reshape(n_db * t_new, d)
    zqs, kas, vas, kbs, vbs, lfs = _mixer_in(xs, sh1s, sc1s, g_mix2, w_main, col_scale, w_f, b_f2,
                                             nb=n_db, t=t_new, tiles_per_batch=1, tail=n_db * t_new)
    lp = -(-(past + t_new) // 2048) * 2048
    lf_all = jnp.concatenate([cache_fox_logf[layer].transpose(0, 2, 1),
                              lfs.reshape(n_db, t_new, N_HEADS).transpose(0, 2, 1),
                              jnp.zeros((n_db, N_HEADS, lp - past - t_new), F32)], axis=-1)
    f_all = _cumsum_rows(lf_all.reshape(n_db * N_HEADS, lp)).reshape(n_db, N_HEADS, lp)
    oas = _fox_sample(zqs, cache_fox_k[layer].reshape(n_db, past * N_HEADS, HEAD_DIM),
                      cache_fox_v[layer].reshape(n_db, past * N_HEADS, HEAD_DIM), f_all,
                      n_batch=n_db, t_new=t_new, past=past)
    obs = _band_sample(zqs, cache_band_k[layer].reshape(n_db, lb * N_HEADS, HEAD_DIM),
                       cache_band_v[layer].reshape(n_db, lb * N_HEADS, HEAD_DIM), mbs, n_batch=n_db, t_new=t_new)
    x1s = _merge(oas, obs, zqs, xs, gt1s, w_oa_b, w_ob_b, w_out_b, nb=n_db, t=t_new, tiles_per_batch=1)
    y_sample = _ffn(x1s, sh2s, sc2s, gt2s, g_ffn2, g_fin2, w_gate_b, w_up_b, w_down_b,
                    nb=n_db, t=t_new, tiles_per_batch=1).reshape(n_db, t_new, d)
    s_shape = (1, n_db, t_new, N_HEADS, HEAD_DIM)
    sample_out = (kas.reshape(s_shape), vas.reshape(s_shape), lfs.reshape(1, n_db, t_new, N_HEADS),
                  kbs.reshape(s_shape), vbs.reshape(s_shape))

    return (y_prompt, y_sample) + prompt_out + sample_out
```

```python
import functools
import math

import jax
import jax.numpy as jnp
from jax import lax
from jax.experimental import pallas as pl
from jax.experimental.pallas import tpu as pltpu

F32 = jnp.float32
BF16 = jnp.bfloat16

HEAD_DIM = 128
N_HEADS = 8
D_HEADS = N_HEADS * HEAD_DIM
CHUNK = 64
N_PREV_CHUNKS = 8
BAND_PAD = N_PREV_CHUNKS * CHUNK
REL_CLIP = 128
RMS_EPS = 1e-6
NEG_INF = -1e30
LOG2E = math.log2(math.e)
LANES = 128
Z_COLS = 6 * D_HEADS

VMEM_CAP_BYTES = 60 * 1024 * 1024
MIB = 1024 * 1024

BAND_TQ = 256
BAND_W = BAND_PAD + BAND_TQ
BAND_TQ_S = 128
BAND_W_S = BAND_PAD + BAND_TQ_S
FOX_TQ = 512
FOX_TK = 256


def _vmem_limit(pipelined_bytes, resident_bytes=0):
    est = 2 * pipelined_bytes + resident_bytes + 8 * MIB
    return int(min(max(est, 16 * MIB), VMEM_CAP_BYTES))


def _nbytes(shape, dtype):
    return math.prod(shape) * jnp.dtype(dtype).itemsize


def _dot(a, b):
    return jnp.dot(a, b, preferred_element_type=F32)


def _dot_nt(a, b):
    return lax.dot_general(a, b, (((1,), (1,)), ((), ())), preferred_element_type=F32)


def _split3(x):
    hi = x.astype(BF16)
    r1 = x - hi.astype(F32)
    mid = r1.astype(BF16)
    lo = (r1 - mid.astype(F32)).astype(BF16)
    return hi, mid, lo


def _sigmoid(x):
    return 1.0 / (1.0 + jnp.exp(-x))


def _rms_scale(x):
    return x * lax.rsqrt(jnp.mean(x * x, axis=-1, keepdims=True) + RMS_EPS)


def _per_batch(y, ref, nb, t):
    if nb == 1:
        return y, ref[0]
    return y.reshape(nb, t, y.shape[-1]), ref[...]


def _mod_norm(x, g_ref, sc_ref, sh_ref, nb, t):
    y = _rms_scale(x) * g_ref[...]
    y, sc = _per_batch(y, sc_ref, nb, t)
    _, sh = _per_batch(x, sh_ref, nb, t)
    h = y * (1.0 + sc) + sh
    return h.reshape(x.shape)


def _col_bcast(frow, n):
    r = lax.broadcasted_iota(jnp.int32, (n, n), 0)
    c = lax.broadcasted_iota(jnp.int32, (n, n), 1)
    d = jnp.where(r == c, jnp.broadcast_to(frow, (n, n)), 0.0)
    ones = jnp.ones((n, LANES), BF16)
    hi, mid, lo = _split3(d)
    return _dot(hi, ones) + _dot(mid, ones) + _dot(lo, ones)


def _tile_lanes(x, n):
    reps = n // x.shape[1]
    return x if reps == 1 else jnp.concatenate([x] * reps, axis=1)


def _transpose_bf16(x):
    return x.astype(F32).T.astype(BF16)


def _ada_kernel(c_ref, w_ref, b_ref, o_ref):
    c = c_ref[...]
    a = (c * _sigmoid(c)).astype(BF16)
    o_ref[...] = _dot(a, w_ref[...].astype(BF16)) + b_ref[...]


def _ada(c, w, b):
    n, d = c.shape
    nout = w.shape[1]
    tn = 1024 if nout % 1024 == 0 else 512
    return pl.pallas_call(
        _ada_kernel,
        out_shape=jax.ShapeDtypeStruct((n, nout), F32),
        grid=(nout // tn,),
        in_specs=[pl.BlockSpec((n, d), lambda j: (0, 0)),
                  pl.BlockSpec((d, tn), lambda j: (0, j)),
                  pl.BlockSpec((1, tn), lambda j: (0, j))],
        out_specs=pl.BlockSpec((n, tn), lambda j: (0, j)),
        compiler_params=pltpu.CompilerParams(
            dimension_semantics=("arbitrary",),
            vmem_limit_bytes=_vmem_limit(_nbytes((d, tn), F32) + _nbytes((n, tn), F32),
                                         _nbytes((n, d), F32) + _nbytes((d, tn), BF16))),
        name="ada",
    )(c, w, b.reshape(1, nout))


def _mixer_kernel(x_ref, sh_ref, sc_ref, g_ref, w_ref, cs_ref, wf_ref, bf_ref,
                  zq_ref, kva_ref, kvb_ref, lf_ref, h_scr,
                  *, nb, t, tiles_per_batch, tail, npg):
    i = pl.program_id(0)
    j = pl.program_id(1)

    @pl.when(j == 0)
    def _():
        h = _mod_norm(x_ref[...], g_ref, sc_ref, sh_ref, nb, t).astype(BF16)
        h_scr[...] = h
        fa = _dot(h, wf_ref[...])[:, :N_HEADS] + bf_ref[...]
        lf_ref[...] = jnp.minimum(fa, 0.0) - jnp.log1p(jnp.exp(-jnp.abs(fa)))

    acc = _dot(h_scr[...], w_ref[...])
    zq_ref[...] = (acc * cs_ref[...]).astype(BF16)

    @pl.when((j >= npg) & (j < 3 * npg))
    def _():
        kva_ref[...] = acc

    is_tail = (i % tiles_per_batch) == (tiles_per_batch - 1)
    rows = acc.shape[0]

    @pl.when(is_tail & (j >= 4 * npg) & (j < 6 * npg))
    def _():
        kvb_ref[...] = acc[rows - tail:, :]


def _mixer_in(x2d, shift, scale, g, w_main, col_scale, w_f, b_f, *, nb, t, tiles_per_batch, tail):
    m, d = x2d.shape
    n = w_main.shape[1]
    tm = nb * t
    tn = 1024 if n % 1024 == 0 else 512
    npg = D_HEADS // tn
    n_batches = (m // tm) // tiles_per_batch
    if nb == 1:
        mod_map = lambda i, j: (i // tiles_per_batch, 0, 0)
    else:
        mod_map = lambda i, j: (i, 0, 0)

    def kva_map(i, j):
        return (i, jnp.clip(j - npg, 0, 2 * npg - 1))

    def kvb_map(i, j):
        is_tail = (i % tiles_per_batch) == (tiles_per_batch - 1)
        return (i // tiles_per_batch, jnp.where(is_tail, jnp.clip(j - 4 * npg, 0, 2 * npg - 1), 0))

    kern = functools.partial(_mixer_kernel, nb=nb, t=t, tiles_per_batch=tiles_per_batch, tail=tail, npg=npg)
    pipelined = (_nbytes((tm, d), F32) + _nbytes((d, tn), BF16) + _nbytes((tm, tn), BF16)
                 + _nbytes((tm, tn), F32) + _nbytes((tail, tn), F32))
    resident = _nbytes((tm, d), BF16) + 2 * _nbytes((tm, tn), F32)
    return pl.pallas_call(
        kern,
        out_shape=(jax.ShapeDtypeStruct((m, n), BF16),
                   jax.ShapeDtypeStruct((m, 2 * D_HEADS), F32),
                   jax.ShapeDtypeStruct((n_batches * tail, 2 * D_HEADS), F32),
                   jax.ShapeDtypeStruct((m, N_HEADS), F32)),
        grid=(m // tm, n // tn),
        in_specs=[pl.BlockSpec((tm, d), lambda i, j: (i, 0)),
                  pl.BlockSpec((nb, 1, d), mod_map),
                  pl.BlockSpec((nb, 1, d), mod_map),
                  pl.BlockSpec((1, d), lambda i, j: (0, 0)),
                  pl.BlockSpec((d, tn), lambda i, j: (0, j)),
                  pl.BlockSpec((1, tn), lambda i, j: (0, j)),
                  pl.BlockSpec((d, LANES), lambda i, j: (0, 0)),
                  pl.BlockSpec((1, N_HEADS), lambda i, j: (0, 0))],
        out_specs=(pl.BlockSpec((tm, tn), lambda i, j: (i, j)),
                   pl.BlockSpec((tm, tn), kva_map),
                   pl.BlockSpec((tail, tn), kvb_map),
                   pl.BlockSpec((tm, N_HEADS), lambda i, j: (i, 0))),
        scratch_shapes=[pltpu.VMEM((tm, d), BF16)],
        compiler_params=pltpu.CompilerParams(
            dimension_semantics=("arbitrary", "arbitrary"),
            vmem_limit_bytes=_vmem_limit(pipelined, resident)),
        name="mixer_in",
    )(x2d, shift, scale, g, w_main, col_scale, w_f, b_f)


def _cumsum_kernel(x_ref, o_ref):
    grp, nr, _ = x_ref.shape
    r = lax.broadcasted_iota(jnp.int32, (LANES, LANES), 0)
    c = lax.broadcasted_iota(jnp.int32, (LANES, LANES), 1)
    upper = (r <= c).astype(BF16)
    rr = lax.broadcasted_iota(jnp.int32, (nr, nr), 0)
    cc = lax.broadcasted_iota(jnp.int32, (nr, nr), 1)
    lower = (cc < rr).astype(BF16)
    for gi in range(grp):
        hi, mid, lo = _split3(x_ref[gi])
        within = _dot(hi, upper) + _dot(mid, upper) + _dot(lo, upper)
        tot = jnp.broadcast_to(within[:, LANES - 1:LANES], (nr, LANES))
        hi, mid, lo = _split3(tot)
        before = _dot(lower, hi) + _dot(lower, mid) + _dot(lower, lo)
        o_ref[gi] = within + before


def _cumsum_rows(x):
    rows, length = x.shape
    nr = length // LANES
    grp = 8
    x3 = x.reshape(rows, nr, LANES)
    out = pl.pallas_call(
        _cumsum_kernel,
        out_shape=jax.ShapeDtypeStruct(x3.shape, F32),
        grid=(rows // grp,),
        in_specs=[pl.BlockSpec((grp, nr, LANES), lambda i: (i, 0, 0))],
        out_specs=pl.BlockSpec((grp, nr, LANES), lambda i: (i, 0, 0)),
        compiler_params=pltpu.CompilerParams(dimension_semantics=("arbitrary",)),
        name="cumsum_logf",
    )(x3)
    return out.reshape(rows, length)


def _fox_prompt_kernel(q_ref, k_ref, v_ref, f_ref, o_ref, qxt_scr, kx_scr, vt_scr, acc_scr, s_scr, pa_scr, pb_scr,
                       *, tq, tk):
    s_len = q_ref.shape[0]
    assert tq == 2 * tk
    sub = lax.broadcasted_iota(jnp.int32, (HEAD_DIM, tk), 0)
    minus_ones = jnp.where(sub < 3, -1.0, 0.0).astype(BF16)

    def setup(ci, carry):
        c0 = pl.multiple_of(ci * tk, tk)
        rows = pl.ds(c0, tk)
        qxt_scr[0:HEAD_DIM, rows] = _transpose_bf16(q_ref[rows, :])
        qxt_scr[HEAD_DIM:, rows] = minus_ones
        vt_scr[:, rows] = _transpose_bf16(v_ref[rows, :])
        hi, mid, lo = [part.astype(F32) for part in _split3(f_ref[:, rows] * LOG2E)]
        ext_t = jnp.where(sub == 0, hi, jnp.where(sub == 1, mid, jnp.where(sub == 2, lo, 0.0)))
        kx_scr[rows, 0:HEAD_DIM] = k_ref[rows, :]
        kx_scr[rows, HEAD_DIM:] = ext_t.T.astype(BF16)
        return carry

    lax.fori_loop(0, s_len // tk, setup, 0)

    key_ix = lax.broadcasted_iota(jnp.int32, (tk, tq), 0)
    qry_ix = lax.broadcasted_iota(jnp.int32, (tk, tq), 1)

    def q_body(qi, carry):
        r0 = pl.multiple_of(qi * tq, tq)
        f_q = f_ref[:, pl.ds(r0, tq)] * LOG2E

        def scores(c0):
            c0 = pl.multiple_of(c0, tk)
            return _dot(kx_scr[pl.ds(c0, tk), :], qxt_scr[:, pl.ds(r0, tq)])

        def softmax(s, m_old, l_old, p_ref):
            t_max = jnp.max(s, axis=0, keepdims=True) + f_q
            if m_old is None:
                m_new, alpha = t_max, None
            else:
                m_new = jnp.maximum(m_old, t_max)
                alpha = jnp.exp2(m_old - m_new)
            p = jnp.exp2(s - (m_new - f_q))
            p_sum = jnp.sum(p, axis=0, keepdims=True)
            p_ref[...] = p.astype(BF16)
            return m_new, (p_sum if m_old is None else alpha * l_old + p_sum), alpha

        def accumulate(c0, p_ref, alpha):
            c0 = pl.multiple_of(c0, tk)
            pv = _dot(vt_scr[:, pl.ds(c0, tk)], p_ref[...])
            acc_scr[...] = pv if alpha is None else alpha * acc_scr[...] + pv

        d0 = jnp.where(key_ix <= qry_ix, scores(r0), NEG_INF)
        d1 = jnp.where(key_ix + tk <= qry_ix, scores(r0 + tk), NEG_INF)
        s_scr[...] = scores(0)
        m, l, _ = softmax(d0, None, None, pa_scr)
        accumulate(r0, pa_scr, None)
        m, l, alpha = softmax(d1, m, l, pb_scr)

        def k_body(kb, state):
            c_pend, alpha_pend, m, l = state
            c0 = kb * tq
            s_1 = scores(c0 + tk)
            accumulate(c_pend, pb_scr, alpha_pend)
            m, l, alpha_0 = softmax(s_scr[...], m, l, pa_scr)
            s_scr[...] = scores(c0 + tq)
            accumulate(c0, pa_scr, alpha_0)
            m, l, alpha_1 = softmax(s_1, m, l, pb_scr)
            return c0 + tk, alpha_1, m, l

        c_pend, alpha_pend, m, l = lax.fori_loop(0, qi, k_body, (r0 + tk, alpha, m, l))
        accumulate(c_pend, pb_scr, alpha_pend)
        o_t = acc_scr[...] * (1.0 / l)
        o_ref[pl.ds(r0, tq), :] = o_t.T.astype(BF16)
        return carry

    lax.fori_loop(0, s_len // tq, q_body, 0)


def _fox_prompt(zq, f_rows, *, n_batch, s_len):
    tq = min(FOX_TQ, s_len)
    tk = min(FOX_TK, tq)
    blk = (s_len, HEAD_DIM)
    scratch = [pltpu.VMEM((2 * HEAD_DIM, s_len), BF16), pltpu.VMEM((s_len, 2 * HEAD_DIM), BF16),
               pltpu.VMEM((HEAD_DIM, s_len), BF16), pltpu.VMEM((HEAD_DIM, tq), F32),
               pltpu.VMEM((tk, tq), F32), pltpu.VMEM((tk, tq), BF16), pltpu.VMEM((tk, tq), BF16)]
    resident = 5 * _nbytes(blk, BF16) + 8 * _nbytes((tk, tq), F32)
    return pl.pallas_call(
        functools.partial(_fox_prompt_kernel, tq=tq, tk=tk),
        out_shape=jax.ShapeDtypeStruct((n_batch * s_len, D_HEADS), BF16),
        grid=(n_batch, N_HEADS),
        in_specs=[pl.BlockSpec(blk, lambda b, h: (b, h)),
                  pl.BlockSpec(blk, lambda b, h: (b, N_HEADS + h)),
                  pl.BlockSpec(blk, lambda b, h: (b, 2 * N_HEADS + h)),
                  pl.BlockSpec((None, 1, s_len), lambda b, h: (b * N_HEADS + h, 0, 0))],
        out_specs=pl.BlockSpec(blk, lambda b, h: (b, h)),
        scratch_shapes=scratch,
        compiler_params=pltpu.CompilerParams(
            dimension_semantics=("arbitrary", "arbitrary"),
            vmem_limit_bytes=_vmem_limit(4 * _nbytes(blk, BF16) + _nbytes((8, s_len), F32), resident)),
        name="fox_prompt",
    )(zq, zq, zq, f_rows)


BIAS_ROWS = 4 * REL_CLIP
BIAS_C0 = BAND_PAD - 2 * REL_CLIP


def _band_bias_kernel(tbl_ref, mbt_ref, mbs_ref, g_scr):
    h = pl.program_id(0)
    far = tbl_ref[h, 2 * REL_CLIP]
    diff = (lax.broadcasted_iota(jnp.int32, (BIAS_ROWS, LANES), 0) + BIAS_C0
            - lax.broadcasted_iota(jnp.int32, (BIAS_ROWS, LANES), 1))
    g_scr[...] = jnp.full((BIAS_ROWS, LANES), far, F32)
    for tix in range(1, 2 * REL_CLIP):
        d = BAND_PAD + REL_CLIP - tix
        lo = d - BIAS_C0
        a8 = max(0, lo // 8 * 8)
        b8 = min(BIAS_ROWS, -(-(lo + LANES) // 8) * 8)
        g_scr[a8:b8, :] = jnp.where(diff[a8:b8, :] == d, tbl_ref[h, tix], g_scr[a8:b8, :])
    g = g_scr[...]

    def far_rows(n):
        return jnp.full((n, LANES), far, F32)

    def masked(bias, q_ax, k_ax):
        rf = lax.broadcasted_iota(jnp.int32, bias.shape, q_ax) // CHUNK
        cf = lax.broadcasted_iota(jnp.int32, bias.shape, k_ax) // CHUNK
        visible = (cf >= rf) & (cf <= rf + N_PREV_CHUNKS)
        return jnp.where(visible, bias * LOG2E, NEG_INF)

    blocks = [jnp.concatenate([far_rows(BIAS_C0 + LANES * j), g[0:BAND_W - BIAS_C0 - LANES * j]], axis=0)
              for j in range(BAND_TQ // LANES)]
    mbt_ref[...] = masked(jnp.concatenate(blocks, axis=1), 1, 0)
    near = g[0:BAND_W_S - BIAS_C0].T
    mbs_ref[...] = masked(jnp.concatenate([jnp.full((BAND_TQ_S, BIAS_C0), far, F32), near], axis=1), 0, 1)


def _band_bias(table):
    assert BAND_TQ_S == LANES and BAND_W <= BIAS_C0 + BIAS_ROWS
    return pl.pallas_call(
        _band_bias_kernel,
        out_shape=(jax.ShapeDtypeStruct((N_HEADS, BAND_W, BAND_TQ), F32),
                   jax.ShapeDtypeStruct((N_HEADS, BAND_TQ_S, BAND_W_S), F32)),
        grid=(N_HEADS,),
        in_specs=[pl.BlockSpec(memory_space=pltpu.SMEM)],
        out_specs=(pl.BlockSpec((None, BAND_W, BAND_TQ), lambda h: (h, 0, 0)),
                   pl.BlockSpec((None, BAND_TQ_S, BAND_W_S), lambda h: (h, 0, 0))),
        scratch_shapes=[pltpu.VMEM((BIAS_ROWS, LANES), F32)],
        compiler_params=pltpu.CompilerParams(dimension_semantics=("arbitrary",)),
        name="band_bias",
    )(table.T)


def _band_prompt_kernel(q_ref, k_ref, v_ref, mb_ref, o_ref, qt_scr, kpad, vtpad, s_scr, pa_scr, pb_scr):
    s_len = q_ref.shape[0]
    kpad[0:BAND_PAD, :] = jnp.zeros((BAND_PAD, HEAD_DIM), BF16)
    kpad[BAND_PAD:, :] = k_ref[...]
    vtpad[:, 0:BAND_PAD] = jnp.zeros((HEAD_DIM, BAND_PAD), BF16)

    def setup(ci, carry):
        c0 = pl.multiple_of(ci * BAND_TQ, BAND_TQ)
        qt_scr[:, pl.ds(c0, BAND_TQ)] = _transpose_bf16(q_ref[pl.ds(c0, BAND_TQ), :])
        vtpad[:, pl.ds(pl.multiple_of(BAND_PAD + c0, BAND_TQ), BAND_TQ)] = _transpose_bf16(v_ref[pl.ds(c0, BAND_TQ), :])
        return carry

    n_tiles = s_len // BAND_TQ
    lax.fori_loop(0, n_tiles, setup, 0)
    key_ix = lax.broadcasted_iota(jnp.int32, (BAND_W, BAND_TQ), 0)

    def scores(ti):
        r0 = pl.multiple_of(ti * BAND_TQ, BAND_TQ)
        return _dot(kpad[pl.ds(r0, BAND_W), :], qt_scr[:, pl.ds(r0, BAND_TQ)]) + mb_ref[...]

    def softmax(s, p_ref):
        m = jnp.max(s, axis=0, keepdims=True)
        p = jnp.exp2(s - m)
        p_ref[...] = p.astype(BF16)
        return 1.0 / jnp.sum(p, axis=0, keepdims=True)

    def emit(ti, p_ref, r_den):
        r0 = pl.multiple_of(ti * BAND_TQ, BAND_TQ)
        o_t = _dot(vtpad[:, pl.ds(r0, BAND_W)], p_ref[...]) * r_den
        o_ref[pl.ds(r0, BAND_TQ), :] = o_t.T.astype(BF16)

    assert BAND_PAD // BAND_TQ == 2
    h0 = jnp.where(key_ix >= BAND_PAD, scores(0), NEG_INF)
    h1 = jnp.where(BAND_TQ + key_ix >= BAND_PAD, scores(1), NEG_INF)
    s_scr[...] = scores(2)
    r_0 = softmax(h0, pa_scr)
    emit(0, pa_scr, r_0)
    r_1 = softmax(h1, pb_scr)

    def pair(pi, r_pend):
        a = 2 + 2 * pi
        s_b = scores(a + 1)
        emit(a - 1, pb_scr, r_pend)
        r_a = softmax(s_scr[...], pa_scr)
        s_scr[...] = scores(jnp.minimum(a + 2, n_tiles - 1))
        emit(a, pa_scr, r_a)
        return softmax(s_b, pb_scr)

    r_last = lax.fori_loop(0, (n_tiles - 2) // 2, pair, r_1)
    emit(n_tiles - 1, pb_scr, r_last)


def _band_prompt(zq, mbt, *, n_batch, s_len):
    blk = (s_len, HEAD_DIM)
    off = 3 * N_HEADS
    assert (s_len // BAND_TQ - BAND_PAD // BAND_TQ) % 2 == 0
    scratch = [pltpu.VMEM((HEAD_DIM, s_len), BF16), pltpu.VMEM((BAND_PAD + s_len, HEAD_DIM), BF16),
               pltpu.VMEM((HEAD_DIM, BAND_PAD + s_len), BF16), pltpu.VMEM((BAND_W, BAND_TQ), F32),
               pltpu.VMEM((BAND_W, BAND_TQ), BF16), pltpu.VMEM((BAND_W, BAND_TQ), BF16)]
    resident = 3 * _nbytes((BAND_PAD + s_len, HEAD_DIM), BF16) + 8 * _nbytes((BAND_W, BAND_TQ), F32)
    return pl.pallas_call(
        _band_prompt_kernel,
        out_shape=jax.ShapeDtypeStruct((n_batch * s_len, D_HEADS), BF16),
        grid=(n_batch, N_HEADS),
        in_specs=[pl.BlockSpec(blk, lambda b, h: (b, off + h)),
                  pl.BlockSpec(blk, lambda b, h: (b, off + N_HEADS + h)),
                  pl.BlockSpec(blk, lambda b, h: (b, off + 2 * N_HEADS + h)),
                  pl.BlockSpec((None, BAND_W, BAND_TQ), lambda b, h: (h, 0, 0))],
        out_specs=pl.BlockSpec(blk, lambda b, h: (b, h)),
        scratch_shapes=scratch,
        compiler_params=pltpu.CompilerParams(
            dimension_semantics=("arbitrary", "arbitrary"),
            vmem_limit_bytes=_vmem_limit(4 * _nbytes(blk, BF16) + _nbytes((BAND_W, BAND_TQ), F32), resident)),
        name="band_prompt",
    )(zq, zq, zq, mbt)


def _fox_sample_kernel(q_ref, kn_ref, vn_ref, ck_ref, cv_ref, f_ref, o_ref,
                       m_scr, l_scr, acc_scr, fq_scr, *, past, tk):
    kt = pl.program_id(1)
    t_new = q_ref.shape[0]

    def heads(h):
        return slice(h * HEAD_DIM, (h + 1) * HEAD_DIM)

    @pl.when(kt == 0)
    def _():
        for h in range(N_HEADS):
            fq_scr[h] = _col_bcast(f_ref[h:h + 1, past:past + t_new] * LOG2E, t_new)
        m_scr[...] = jnp.full(m_scr.shape, NEG_INF, F32)
        l_scr[...] = jnp.zeros(l_scr.shape, F32)
        acc_scr[...] = jnp.zeros(acc_scr.shape, F32)

    def update(h, s, v):
        m_old = m_scr[h]
        m_new = jnp.maximum(m_old, jnp.max(s, axis=1, keepdims=True))
        alpha = jnp.exp2(m_old - m_new)
        p = jnp.exp2(s - m_new)
        l_scr[h] = alpha * l_scr[h] + jnp.sum(p, axis=1, keepdims=True)
        acc_scr[h] = alpha * acc_scr[h] + _dot(p.astype(BF16), v)
        m_scr[h] = m_new

    c0 = pl.multiple_of(kt * tk, tk)
    scores = []
    for h in range(N_HEADS):
        k = ck_ref[pl.ds(h, tk, stride=N_HEADS), :].astype(BF16)
        scores.append(_dot_nt(q_ref[:, heads(h)], k) + _tile_lanes(fq_scr[h], tk)
                      - f_ref[h:h + 1, pl.ds(c0, tk)] * LOG2E)
    for h in range(N_HEADS):
        update(h, scores[h], cv_ref[pl.ds(h, tk, stride=N_HEADS), :].astype(BF16))

    @pl.when(kt == pl.num_programs(1) - 1)
    def _():
        r = lax.broadcasted_iota(jnp.int32, (t_new, t_new), 0)
        c = lax.broadcasted_iota(jnp.int32, (t_new, t_new), 1)
        for h in range(N_HEADS):
            q = q_ref[:, heads(h)]
            f_new = f_ref[h:h + 1, past:past + t_new] * LOG2E
            s = _dot_nt(q, kn_ref[:, heads(h)]) + fq_scr[h][:, :t_new] - f_new
            s = jnp.where(c <= r, s, NEG_INF)
            update(h, s, vn_ref[:, heads(h)])
            o_ref[:, heads(h)] = (acc_scr[h] / l_scr[h]).astype(BF16)


def _fox_sample(zq, cache_k, cache_v, f_all, *, n_batch, t_new, past):
    tk = min(1024, past)
    lp = f_all.shape[-1]
    blk = (t_new, D_HEADS)
    cblk = (None, tk * N_HEADS, HEAD_DIM)
    return pl.pallas_call(
        functools.partial(_fox_sample_kernel, past=past, tk=tk),
        out_shape=jax.ShapeDtypeStruct((n_batch * t_new, D_HEADS), BF16),
        grid=(n_batch, past // tk),
        in_specs=[pl.BlockSpec(blk, lambda b, k: (b, 0)),
                  pl.BlockSpec(blk, lambda b, k: (b, 1)),
                  pl.BlockSpec(blk, lambda b, k: (b, 2)),
                  pl.BlockSpec(cblk, lambda b, k: (b, k, 0)),
                  pl.BlockSpec(cblk, lambda b, k: (b, k, 0)),
                  pl.BlockSpec((None, N_HEADS, lp), lambda b, k: (b, 0, 0))],
        out_specs=pl.BlockSpec(blk, lambda b, k: (b, 0)),
        scratch_shapes=[pltpu.VMEM((N_HEADS, t_new, 1), F32), pltpu.VMEM((N_HEADS, t_new, 1), F32),
                        pltpu.VMEM((N_HEADS, t_new, HEAD_DIM), F32),
                        pltpu.VMEM((N_HEADS, t_new, LANES), F32)],
        compiler_params=pltpu.CompilerParams(
            dimension_semantics=("arbitrary", "arbitrary"),
            vmem_limit_bytes=_vmem_limit(2 * _nbytes((tk, D_HEADS), F32) + 4 * _nbytes(blk, BF16)
                                         + _nbytes((N_HEADS, lp), F32), 8 * MIB)),
        name="fox_sample",
    )(zq, zq, zq, cache_k, cache_v, f_all)


def _band_sample_kernel(q_ref, kn_ref, vn_ref, ck_ref, cv_ref, mb_ref, o_ref):
    t_new = q_ref.shape[0]
    lb = ck_ref.shape[0] // N_HEADS
    for h in range(N_HEADS):
        hs = slice(h * HEAD_DIM, (h + 1) * HEAD_DIM)
        q = q_ref[:, hs]
        s1 = _dot_nt(q, ck_ref[pl.ds(h, lb, stride=N_HEADS), :].astype(BF16)) + mb_ref[h, :, 0:lb]
        s2 = _dot_nt(q, kn_ref[:, hs]) + mb_ref[h, :, lb:lb + t_new]
        m = jnp.maximum(jnp.max(s1, axis=1, keepdims=True), jnp.max(s2, axis=1, keepdims=True))
        p1 = jnp.exp2(s1 - m)
        p2 = jnp.exp2(s2 - m)
        den = jnp.sum(p1, axis=1, keepdims=True) + jnp.sum(p2, axis=1, keepdims=True)
        o = _dot(p1.astype(BF16), cv_ref[pl.ds(h, lb, stride=N_HEADS), :].astype(BF16)) + _dot(p2.astype(BF16), vn_ref[:, hs])
        o_ref[:, hs] = (o / den).astype(BF16)


def _band_sample(zq, cache_k, cache_v, mb, *, n_batch, t_new):
    lb = cache_k.shape[1] // N_HEADS
    blk = (t_new, D_HEADS)
    cblk = (None, lb * N_HEADS, HEAD_DIM)
    return pl.pallas_call(
        _band_sample_kernel,
        out_shape=jax.ShapeDtypeStruct((n_batch * t_new, D_HEADS), BF16),
        grid=(n_batch,),
        in_specs=[pl.BlockSpec(blk, lambda b: (b, 3)),
                  pl.BlockSpec(blk, lambda b: (b, 4)),
                  pl.BlockSpec(blk, lambda b: (b, 5)),
                  pl.BlockSpec(cblk, lambda b: (b, 0, 0)),
                  pl.BlockSpec(cblk, lambda b: (b, 0, 0)),
                  pl.BlockSpec((N_HEADS, t_new, BAND_W_S), lambda b: (0, 0, 0))],
        out_specs=pl.BlockSpec(blk, lambda b: (b, 0)),
        compiler_params=pltpu.CompilerParams(
            dimension_semantics=("arbitrary",),
            vmem_limit_bytes=_vmem_limit(2 * _nbytes((lb, D_HEADS), F32) + 4 * _nbytes(blk, BF16),
                                         _nbytes((N_HEADS, t_new, BAND_W_S), F32) + 4 * MIB)),
        name="band_sample",
    )(zq, zq, zq, cache_k, cache_v, mb)


def _merge_kernel(oa_ref, ob_ref, za_ref, zb_ref, x_ref, gt_ref, woa_ref, wob_ref, wout_ref, o_ref, *, nb, t):
    a = _dot(oa_ref[...], woa_ref[...])
    b = _dot(ob_ref[...], wob_ref[...])
    mix = _sigmoid(za_ref[...].astype(F32)) * a + _sigmoid(zb_ref[...].astype(F32)) * b
    y = _dot(mix.astype(BF16), wout_ref[...])
    x = x_ref[...]
    y, gt = _per_batch(y, gt_ref, nb, t)
    x3, _ = _per_batch(x, gt_ref, nb, t)
    o_ref[...] = (x3 + gt * y).reshape(x.shape)


def _merge(oa, ob, zq, x2d, gate, w_oa, w_ob, w_out, *, nb, t, tiles_per_batch):
    m, d = x2d.shape
    tm = nb * t
    za_blk = Z_COLS // d
    if nb == 1:
        mod_map = lambda i: (i // tiles_per_batch, 0, 0)
    else:
        mod_map = lambda i: (i, 0, 0)
    const = lambda i: (0, 0)
    pipelined = (2 * _nbytes((tm, D_HEADS), BF16) + 2 * _nbytes((tm, d), BF16) + 2 * _nbytes((tm, d), F32))
    resident = (2 * (2 * _nbytes((D_HEADS, d), BF16) + _nbytes((d, d), BF16)) + 4 * _nbytes((tm, d), F32))
    return pl.pallas_call(
        functools.partial(_merge_kernel, nb=nb, t=t),
        out_shape=jax.ShapeDtypeStruct((m, d), F32),
        grid=(m // tm,),
        in_specs=[pl.BlockSpec((tm, D_HEADS), lambda i: (i, 0)),
                  pl.BlockSpec((tm, D_HEADS), lambda i: (i, 0)),
                  pl.BlockSpec((tm, d), lambda i: (i, za_blk)),
                  pl.BlockSpec((tm, d), lambda i: (i, za_blk + 1)),
                  pl.BlockSpec((tm, d), lambda i: (i, 0)),
                  pl.BlockSpec((nb, 1, d), mod_map),
                  pl.BlockSpec((D_HEADS, d), const),
                  pl.BlockSpec((D_HEADS, d), const),
                  pl.BlockSpec((d, d), const)],
        out_specs=pl.BlockSpec((tm, d), lambda i: (i, 0)),
        compiler_params=pltpu.CompilerParams(
            dimension_semantics=("arbitrary",),
            vmem_limit_bytes=_vmem_limit(pipelined, resident)),
        name="merge_out",
    )(oa, ob, zq, zq, x2d, gate, w_oa, w_ob, w_out)


def _ffn_kernel(x_ref, sh_ref, sc_ref, gt_ref, g_ref, gf_ref, wg_ref, wu_ref, wd_ref, o_ref,
                h_scr, acc_scr, *, nb, t):
    f = pl.program_id(1)

    @pl.when(f == 0)
    def _():
        h_scr[...] = _mod_norm(x_ref[...], g_ref, sc_ref, sh_ref, nb, t).astype(BF16)
        acc_scr[...] = jnp.zeros(acc_scr.shape, F32)

    h = h_scr[...]
    gate = _dot(h, wg_ref[...])
    up = _dot(h, wu_ref[...])
    act = (gate * _sigmoid(gate) * up).astype(BF16)
    acc_scr[...] += _dot(act, wd_ref[...])

    @pl.when(f == pl.num_programs(1) - 1)
    def _():
        x = x_ref[...]
        y, gt = _per_batch(acc_scr[...], gt_ref, nb, t)
        x3, _ = _per_batch(x, gt_ref, nb, t)
        x2 = (x3 + gt * y).reshape(x.shape)
        o_ref[...] = _rms_scale(x2) * gf_ref[...]


def _ffn(x2d, shift, scale, gate, g_ffn, g_final, w_gate, w_up, w_down, *, nb, t, tiles_per_batch):
    m, d = x2d.shape
    dff = w_gate.shape[1]
    tm = nb * t
    tf = 512
    if nb == 1:
        mod_map = lambda i, f: (i // tiles_per_batch, 0, 0)
    else:
        mod_map = lambda i, f: (i, 0, 0)
    pipelined = 2 * _nbytes((tm, d), F32) + 3 * _nbytes((d, tf), BF16)
    resident = _nbytes((tm, d), BF16) + _nbytes((tm, d), F32) + 3 * _nbytes((tm, tf), F32) + 2 * _nbytes((tm, d), F32)
    return pl.pallas_call(
        functools.partial(_ffn_kernel, nb=nb, t=t),
        out_shape=jax.ShapeDtypeStruct((m, d), F32),
        grid=(m // tm, dff // tf),
        in_specs=[pl.BlockSpec((tm, d), lambda i, f: (i, 0)),
                  pl.BlockSpec((nb, 1, d), mod_map),
                  pl.BlockSpec((nb, 1, d), mod_map),
                  pl.BlockSpec((nb, 1, d), mod_map),
                  pl.BlockSpec((1, d), lambda i, f: (0, 0)),
                  pl.BlockSpec((1, d), lambda i, f: (0, 0)),
                  pl.BlockSpec((d, tf), lambda i, f: (0, f)),
                  pl.BlockSpec((d, tf), lambda i, f: (0, f)),
                  pl.BlockSpec((tf, d), lambda i, f: (f, 0))],
        out_specs=pl.BlockSpec((tm, d), lambda i, f: (i, 0)),
        scratch_shapes=[pltpu.VMEM((tm, d), BF16), pltpu.VMEM((tm, d), F32)],
        compiler_params=pltpu.CompilerParams(
            dimension_semantics=("arbitrary", "arbitrary"),
            vmem_limit_bytes=_vmem_limit(pipelined, resident)),
        name="ffn_final",
    )(x2d, shift, scale, gate, g_ffn, g_final, w_gate, w_up, w_down)


def _row_tile(s_len, cap):
    tile = min(cap, s_len)
    assert s_len % tile == 0
    return tile


def kernel(x_prompt, x_sample, cache_fox_k, cache_fox_v, cache_fox_logf, cache_band_k, cache_band_v,
           c_prompt, c_sample, w_ada, b_ada, g_mix, w_in, b_f, rel_bias, w_oa, w_ob, w_out,
           g_ffn, w_gate, w_up, w_down, g_final):
    n_b, s_len, d = x_prompt.shape
    n_db, t_new, _ = x_sample.shape
    past = cache_fox_k.shape[2]
    lb = cache_band_k.shape[2]
    assert w_ada.shape[0] == 1, "single-layer trunk"
    assert lb == BAND_PAD and s_len >= BAND_PAD and s_len % FOX_TQ == 0
    assert Z_COLS % d == 0 and t_new % 8 == 0 and t_new + lb <= BAND_W_S and t_new <= CHUNK
    layer = 0
    q_scale = LOG2E / math.sqrt(HEAD_DIM)

    w_l = w_in[layer]
    w_main = jnp.concatenate([w_l[:, :3 * D_HEADS], w_l[:, 3 * D_HEADS + N_HEADS:]], axis=1).astype(BF16)
    w_f = jnp.pad(w_l[:, 3 * D_HEADS:3 * D_HEADS + N_HEADS], ((0, 0), (0, LANES - N_HEADS))).astype(BF16)
    n_main = w_main.shape[1]
    col = jnp.arange(n_main)
    is_q = (col < D_HEADS) | ((col >= 3 * D_HEADS) & (col < 4 * D_HEADS))
    col_scale = jnp.where(is_q, q_scale, 1.0).astype(F32).reshape(1, n_main)
    b_f2 = b_f[layer].reshape(1, N_HEADS)
    g_mix2 = g_mix[layer].reshape(1, d)
    g_ffn2 = g_ffn[layer].reshape(1, d)
    g_fin2 = g_final.reshape(1, d)
    w_oa_b, w_ob_b, w_out_b = w_oa[layer].astype(BF16), w_ob[layer].astype(BF16), w_out[layer].astype(BF16)
    w_gate_b, w_up_b, w_down_b = w_gate[layer].astype(BF16), w_up[layer].astype(BF16), w_down[layer].astype(BF16)

    mods = _ada(jnp.concatenate([c_prompt, c_sample], axis=0), w_ada[layer], b_ada[layer])

    def chunks(rows):
        return [rows[:, k * d:(k + 1) * d][:, None, :] for k in range(6)]

    sh1p, sc1p, gt1p, sh2p, sc2p, gt2p = chunks(mods[:n_b])
    sh1s, sc1s, gt1s, sh2s, sc2s, gt2s = chunks(mods[n_b:])
    mbt, mbs = _band_bias(rel_bias[layer])

    xp = x_prompt.reshape(n_b * s_len, d)
    tm = _row_tile(s_len, 1024)
    zq, kva, kvb, lf = _mixer_in(xp, sh1p, sc1p, g_mix2, w_main, col_scale, w_f, b_f2,
                                       nb=1, t=tm, tiles_per_batch=s_len // tm, tail=BAND_PAD)
    lf_rows = lf.reshape(n_b, s_len, N_HEADS).transpose(0, 2, 1).reshape(n_b * N_HEADS, s_len)
    f_rows = _cumsum_rows(lf_rows).reshape(n_b * N_HEADS, 1, s_len)
    oa = _fox_prompt(zq, f_rows, n_batch=n_b, s_len=s_len)
    ob = _band_prompt(zq, mbt, n_batch=n_b, s_len=s_len)
    tm = _row_tile(s_len, 256)
    x1 = _merge(oa, ob, zq, xp, gt1p, w_oa_b, w_ob_b, w_out_b, nb=1, t=tm, tiles_per_batch=s_len // tm)
    tm = _row_tile(s_len, 512)
    y_prompt = _ffn(x1, sh2p, sc2p, gt2p, g_ffn2, g_fin2, w_gate_b, w_up_b, w_down_b,
                    nb=1, t=tm, tiles_per_batch=s_len // tm).reshape(n_b, s_len, d)
    fox_shape = (1, n_b, s_len, N_HEADS, HEAD_DIM)
    band_shape = (1, n_b, BAND_PAD, N_HEADS, HEAD_DIM)
    prompt_out = (kva[:, :D_HEADS].reshape(fox_shape), kva[:, D_HEADS:].reshape(fox_shape),
                  lf.reshape(1, n_b, s_len, N_HEADS),
                  kvb[:, :D_HEADS].reshape(band_shape), kvb[:, D_HEADS:].reshape(band_shape))

    xs = x_sample.reshape(n_db * t_new, d)
    zqs, kvas, kvbs, lfs = _mixer_in(xs, sh1s, sc1s, g_mix2, w_main, col_scale, w_f, b_f2,
                                             nb=n_db, t=t_new, tiles_per_batch=1, tail=n_db * t_new)
    lp = -(-(past + t_new) // 2048) * 2048
    lf_all = jnp.concatenate([cache_fox_logf[layer].transpose(0, 2, 1),
                              lfs.reshape(n_db, t_new, N_HEADS).transpose(0, 2, 1),
                              jnp.zeros((n_db, N_HEADS, lp - past - t_new), F32)], axis=-1)
    f_all = _cumsum_rows(lf_all.reshape(n_db * N_HEADS, lp)).reshape(n_db, N_HEADS, lp)
    oas = _fox_sample(zqs, cache_fox_k[layer].reshape(n_db, past * N_HEADS, HEAD_DIM),
                      cache_fox_v[layer].reshape(n_db, past * N_HEADS, HEAD_DIM), f_all,
                      n_batch=n_db, t_new=t_new, past=past)
    obs = _band_sample(zqs, cache_band_k[layer].reshape(n_db, lb * N_HEADS, HEAD_DIM),
                       cache_band_v[layer].reshape(n_db, lb * N_HEADS, HEAD_DIM), mbs, n_batch=n_db, t_new=t_new)
    x1s = _merge(oas, obs, zqs, xs, gt1s, w_oa_b, w_ob_b, w_out_b, nb=n_db, t=t_new, tiles_per_batch=1)
    y_sample = _ffn(x1s, sh2s, sc2s, gt2s, g_ffn2, g_fin2, w_gate_b, w_up_b, w_down_b,
                    nb=n_db, t=t_new, tiles_per_batch=1).reshape(n_db, t_new, d)
    s_shape = (1, n_db, t_new, N_HEADS, HEAD_DIM)
    sample_out = (kvas[:, :D_HEADS].reshape(s_shape), kvas[:, D_HEADS:].reshape(s_shape),
                  lfs.reshape(1, n_db, t_new, N_HEADS),
                  kvbs[:, :D_HEADS].reshape(s_shape), kvbs[:, D_HEADS:].reshape(s_shape))

    return (y_prompt, y_sample) + prompt_out + sample_out
```

```python
import functools
import math

import jax
import jax.numpy as jnp
from jax import lax
from jax.experimental import pallas as pl
from jax.experimental.pallas import tpu as pltpu

F32 = jnp.float32
BF16 = jnp.bfloat16

HEAD_DIM = 128
N_HEADS = 8
D_HEADS = N_HEADS * HEAD_DIM
CHUNK = 64
N_PREV_CHUNKS = 8
BAND_PAD = N_PREV_CHUNKS * CHUNK
REL_CLIP = 128
RMS_EPS = 1e-6
NEG_INF = -1e30
LOG2E = math.log2(math.e)
LANES = 128
Z_COLS = 6 * D_HEADS

VMEM_CAP_BYTES = 60 * 1024 * 1024
MIB = 1024 * 1024

BAND_TQ = 256
BAND_W = BAND_PAD + BAND_TQ
BAND_TQ_S = 128
BAND_W_S = BAND_PAD + BAND_TQ_S
FOX_TQ = 512
FOX_TK = 256


def _vmem_limit(pipelined_bytes, resident_bytes=0):
    est = 2 * pipelined_bytes + resident_bytes + 8 * MIB
    return int(min(max(est, 16 * MIB), VMEM_CAP_BYTES))


def _nbytes(shape, dtype):
    return math.prod(shape) * jnp.dtype(dtype).itemsize


def _dot(a, b):
    return jnp.dot(a, b, preferred_element_type=F32)


def _dot_nt(a, b):
    return lax.dot_general(a, b, (((1,), (1,)), ((), ())), preferred_element_type=F32)


def _split3(x):
    hi = x.astype(BF16)
    r1 = x - hi.astype(F32)
    mid = r1.astype(BF16)
    lo = (r1 - mid.astype(F32)).astype(BF16)
    return hi, mid, lo


def _sigmoid(x):
    return 1.0 / (1.0 + jnp.exp(-x))


def _rms_scale(x):
    return x * lax.rsqrt(jnp.mean(x * x, axis=-1, keepdims=True) + RMS_EPS)


def _per_batch(y, ref, nb, t):
    if nb == 1:
        return y, ref[0]
    return y.reshape(nb, t, y.shape[-1]), ref[...]


def _mod_norm(x, g_ref, sc_ref, sh_ref, nb, t):
    y = _rms_scale(x) * g_ref[...]
    y, sc = _per_batch(y, sc_ref, nb, t)
    _, sh = _per_batch(x, sh_ref, nb, t)
    h = y * (1.0 + sc) + sh
    return h.reshape(x.shape)


def _col_bcast(frow, n):
    r = lax.broadcasted_iota(jnp.int32, (n, n), 0)
    c = lax.broadcasted_iota(jnp.int32, (n, n), 1)
    d = jnp.where(r == c, jnp.broadcast_to(frow, (n, n)), 0.0)
    ones = jnp.ones((n, LANES), BF16)
    hi, mid, lo = _split3(d)
    return _dot(hi, ones) + _dot(mid, ones) + _dot(lo, ones)


def _tile_lanes(x, n):
    reps = n // x.shape[1]
    return x if reps == 1 else jnp.concatenate([x] * reps, axis=1)


def _transpose_bf16(x):
    return x.astype(F32).T.astype(BF16)


def _ada_kernel(c_ref, w_ref, b_ref, o_ref):
    c = c_ref[...]
    a = (c * _sigmoid(c)).astype(BF16)
    o_ref[...] = _dot(a, w_ref[...].astype(BF16)) + b_ref[...]


def _ada(c, w, b):
    n, d = c.shape
    nout = w.shape[1]
    tn = 1024 if nout % 1024 == 0 else 512
    return pl.pallas_call(
        _ada_kernel,
        out_shape=jax.ShapeDtypeStruct((n, nout), F32),
        grid=(nout // tn,),
        in_specs=[pl.BlockSpec((n, d), lambda j: (0, 0)),
                  pl.BlockSpec((d, tn), lambda j: (0, j)),
                  pl.BlockSpec((1, tn), lambda j: (0, j))],
        out_specs=pl.BlockSpec((n, tn), lambda j: (0, j)),
        compiler_params=pltpu.CompilerParams(
            dimension_semantics=("arbitrary",),
            vmem_limit_bytes=_vmem_limit(_nbytes((d, tn), F32) + _nbytes((n, tn), F32),
                                         _nbytes((n, d), F32) + _nbytes((d, tn), BF16))),
        name="ada",
    )(c, w, b.reshape(1, nout))


def _mixer_kernel(x_ref, sh_ref, sc_ref, g_ref, w_ref, cs_ref, wf_ref, bf_ref,
                  zq_ref, ka_ref, va_ref, kb_ref, vb_ref, lf_ref, h_scr,
                  *, nb, t, tiles_per_batch, tail, npg):
    i = pl.program_id(0)
    j = pl.program_id(1)

    @pl.when(j == 0)
    def _():
        h = _mod_norm(x_ref[...], g_ref, sc_ref, sh_ref, nb, t).astype(BF16)
        h_scr[...] = h
        fa = _dot(h, wf_ref[...])[:, :N_HEADS] + bf_ref[...]
        lf_ref[...] = jnp.minimum(fa, 0.0) - jnp.log1p(jnp.exp(-jnp.abs(fa)))

    acc = _dot(h_scr[...], w_ref[...])
    zq_ref[...] = (acc * cs_ref[...]).astype(BF16)

    def _group(g):
        return (j >= g * npg) & (j < (g + 1) * npg)

    @pl.when(_group(1))
    def _():
        ka_ref[...] = acc

    @pl.when(_group(2))
    def _():
        va_ref[...] = acc

    is_tail = (i % tiles_per_batch) == (tiles_per_batch - 1)
    rows = acc.shape[0]

    @pl.when(is_tail & _group(4))
    def _():
        kb_ref[...] = acc[rows - tail:, :]

    @pl.when(is_tail & _group(5))
    def _():
        vb_ref[...] = acc[rows - tail:, :]


def _mixer_in(x2d, shift, scale, g, w_main, col_scale, w_f, b_f, *, nb, t, tiles_per_batch, tail):
    m, d = x2d.shape
    n = w_main.shape[1]
    tm = nb * t
    tn = 1024 if n % 1024 == 0 else 512
    npg = D_HEADS // tn
    n_batches = (m // tm) // tiles_per_batch
    if nb == 1:
        mod_map = lambda i, j: (i // tiles_per_batch, 0, 0)
    else:
        mod_map = lambda i, j: (i, 0, 0)

    def grp_map(gidx):
        return lambda i, j: (i, jnp.clip(j - gidx * npg, 0, npg - 1))

    def tail_map(gidx):
        def f(i, j):
            is_tail = (i % tiles_per_batch) == (tiles_per_batch - 1)
            return (i // tiles_per_batch, jnp.where(is_tail, jnp.clip(j - gidx * npg, 0, npg - 1), 0))
        return f

    kern = functools.partial(_mixer_kernel, nb=nb, t=t, tiles_per_batch=tiles_per_batch, tail=tail, npg=npg)
    pipelined = _nbytes((tm, d), F32) + _nbytes((d, tn), BF16) + _nbytes((tm, tn), BF16)
    resident = (_nbytes((tm, d), BF16) + 2 * _nbytes((tm, tn), F32) + 2 * _nbytes((tail, tn), F32)
                + 2 * _nbytes((tm, tn), F32))
    return pl.pallas_call(
        kern,
        out_shape=(jax.ShapeDtypeStruct((m, n), BF16),
                   jax.ShapeDtypeStruct((m, D_HEADS), F32),
                   jax.ShapeDtypeStruct((m, D_HEADS), F32),
                   jax.ShapeDtypeStruct((n_batches * tail, D_HEADS), F32),
                   jax.ShapeDtypeStruct((n_batches * tail, D_HEADS), F32),
                   jax.ShapeDtypeStruct((m, N_HEADS), F32)),
        grid=(m // tm, n // tn),
        in_specs=[pl.BlockSpec((tm, d), lambda i, j: (i, 0)),
                  pl.BlockSpec((nb, 1, d), mod_map),
                  pl.BlockSpec((nb, 1, d), mod_map),
                  pl.BlockSpec((1, d), lambda i, j: (0, 0)),
                  pl.BlockSpec((d, tn), lambda i, j: (0, j)),
                  pl.BlockSpec((1, tn), lambda i, j: (0, j)),
                  pl.BlockSpec((d, LANES), lambda i, j: (0, 0)),
                  pl.BlockSpec((1, N_HEADS), lambda i, j: (0, 0))],
        out_specs=(pl.BlockSpec((tm, tn), lambda i, j: (i, j)),
                   pl.BlockSpec((tm, tn), grp_map(1), pipeline_mode=pl.Buffered(1)),
                   pl.BlockSpec((tm, tn), grp_map(2), pipeline_mode=pl.Buffered(1)),
                   pl.BlockSpec((tail, tn), tail_map(4), pipeline_mode=pl.Buffered(1)),
                   pl.BlockSpec((tail, tn), tail_map(5), pipeline_mode=pl.Buffered(1)),
                   pl.BlockSpec((tm, N_HEADS), lambda i, j: (i, 0))),
        scratch_shapes=[pltpu.VMEM((tm, d), BF16)],
        compiler_params=pltpu.CompilerParams(
            dimension_semantics=("arbitrary", "arbitrary"),
            vmem_limit_bytes=_vmem_limit(pipelined, resident)),
        name="mixer_in",
    )(x2d, shift, scale, g, w_main, col_scale, w_f, b_f)


def _cumsum_kernel(x_ref, o_ref):
    grp, nr, _ = x_ref.shape
    rows = grp * nr
    r = lax.broadcasted_iota(jnp.int32, (LANES, LANES), 0)
    c = lax.broadcasted_iota(jnp.int32, (LANES, LANES), 1)
    upper = (r <= c).astype(BF16)
    rr = lax.broadcasted_iota(jnp.int32, (rows, rows), 0)
    cc = lax.broadcasted_iota(jnp.int32, (rows, rows), 1)

    def group(ix):
        return sum((ix >= gi * nr).astype(jnp.int32) for gi in range(1, grp))

    lower = ((cc < rr) & (group(cc) == group(rr))).astype(BF16)
    hi, mid, lo = _split3(x_ref[...].reshape(rows, LANES))
    within = _dot(hi, upper) + _dot(mid, upper) + _dot(lo, upper)
    tot = jnp.broadcast_to(within[:, LANES - 1:LANES], (rows, LANES))
    hi, mid, lo = _split3(tot)
    before = _dot(lower, hi) + _dot(lower, mid) + _dot(lower, lo)
    o_ref[...] = (within + before).reshape(grp, nr, LANES)


def _cumsum_rows(x):
    rows, length = x.shape
    nr = length // LANES
    grp = 8
    x3 = x.reshape(rows, nr, LANES)
    out = pl.pallas_call(
        _cumsum_kernel,
        out_shape=jax.ShapeDtypeStruct(x3.shape, F32),
        grid=(rows // grp,),
        in_specs=[pl.BlockSpec((grp, nr, LANES), lambda i: (i, 0, 0))],
        out_specs=pl.BlockSpec((grp, nr, LANES), lambda i: (i, 0, 0)),
        compiler_params=pltpu.CompilerParams(dimension_semantics=("arbitrary",)),
        name="cumsum_logf",
    )(x3)
    return out.reshape(rows, length)


def _fox_prompt_kernel(q_ref, k_ref, v_ref, f_ref, o_ref, qxt_scr, kx_scr, vt_scr, acc_scr, s_scr, pa_scr, pb_scr,
                       *, tq, tk):
    s_len = q_ref.shape[0]
    assert tq == 2 * tk
    sub = lax.broadcasted_iota(jnp.int32, (HEAD_DIM, tk), 0)
    minus_ones = jnp.where(sub < 3, -1.0, 0.0).astype(BF16)

    def setup(ci, carry):
        c0 = pl.multiple_of(ci * tk, tk)
        rows = pl.ds(c0, tk)
        qxt_scr[0:HEAD_DIM, rows] = _transpose_bf16(q_ref[rows, :])
        qxt_scr[HEAD_DIM:, rows] = minus_ones
        vt_scr[:, rows] = _transpose_bf16(v_ref[rows, :])
        hi, mid, lo = [part.astype(F32) for part in _split3(f_ref[:, rows] * LOG2E)]
        ext_t = jnp.where(sub == 0, hi, jnp.where(sub == 1, mid, jnp.where(sub == 2, lo, 0.0)))
        kx_scr[rows, 0:HEAD_DIM] = k_ref[rows, :]
        kx_scr[rows, HEAD_DIM:] = ext_t.T.astype(BF16)
        return carry

    lax.fori_loop(0, s_len // tk, setup, 0)

    key_ix = lax.broadcasted_iota(jnp.int32, (tk, tq), 0)
    qry_ix = lax.broadcasted_iota(jnp.int32, (tk, tq), 1)

    def scores(r0, c0):
        c0 = pl.multiple_of(c0, tk)
        return _dot(kx_scr[pl.ds(c0, tk), :], qxt_scr[:, pl.ds(r0, tq)])

    def softmax(s, f_q, m_old, l_old, p_ref):
        t_max = jnp.max(s, axis=0, keepdims=True) + f_q
        if m_old is None:
            m_new, alpha = t_max, None
        else:
            m_new = jnp.maximum(m_old, t_max)
            alpha = jnp.exp2(m_old - m_new)
        p = jnp.exp2(s - (m_new - f_q))
        p_sum = jnp.sum(p, axis=0, keepdims=True)
        p_ref[...] = p.astype(BF16)
        return m_new, (p_sum if m_old is None else alpha * l_old + p_sum), alpha

    def accumulate(c0, p_ref, alpha):
        c0 = pl.multiple_of(c0, tk)
        pv = _dot(vt_scr[:, pl.ds(c0, tk)], p_ref[...])
        acc_scr[...] = pv if alpha is None else alpha * acc_scr[...] + pv

    def open_block(r0):
        d0 = jnp.where(key_ix <= qry_ix, scores(r0, r0), NEG_INF)
        d1 = jnp.where(key_ix + tk <= qry_ix, scores(r0, r0 + tk), NEG_INF)
        s_scr[...] = scores(r0, 0)
        return d0, d1

    def diag_block(r0, f_q, d0, d1):
        m, l, _ = softmax(d0, f_q, None, None, pa_scr)
        accumulate(r0, pa_scr, None)
        m, l, alpha = softmax(d1, f_q, m, l, pb_scr)
        return r0 + tk, alpha, m, l

    def close_block(r0, c_pend, alpha_pend, l):
        accumulate(c_pend, pb_scr, alpha_pend)
        o_t = acc_scr[...] * (1.0 / l)
        o_ref[pl.ds(pl.multiple_of(r0, tq), tq), :] = o_t.T.astype(BF16)

    def f_query(r0):
        return f_ref[:, pl.ds(r0, tq)] * LOG2E

    d0, d1 = open_block(0)
    c_pend, alpha_pend, _, l = diag_block(0, f_query(0), d0, d1)

    def q_body(qi, prev):
        r0 = pl.multiple_of(qi * tq, tq)
        f_q = f_query(r0)
        d0, d1 = open_block(r0)
        close_block(r0 - tq, *prev)
        state = diag_block(r0, f_q, d0, d1)

        def k_body(kb, state):
            c_pend, alpha_pend, m, l = state
            c0 = kb * tq
            s_1 = scores(r0, c0 + tk)
            accumulate(c_pend, pb_scr, alpha_pend)
            m, l, alpha_0 = softmax(s_scr[...], f_q, m, l, pa_scr)
            s_scr[...] = scores(r0, c0 + tq)
            accumulate(c0, pa_scr, alpha_0)
            m, l, alpha_1 = softmax(s_1, f_q, m, l, pb_scr)
            return c0 + tk, alpha_1, m, l

        c_pend, alpha_pend, _, l = lax.fori_loop(0, qi, k_body, state)
        return c_pend, alpha_pend, l

    n_q = s_len // tq
    last = lax.fori_loop(1, n_q, q_body, (c_pend, alpha_pend, l))
    close_block((n_q - 1) * tq, *last)


def _fox_prompt(zq, f_rows, *, n_batch, s_len):
    tq = min(FOX_TQ, s_len)
    tk = min(FOX_TK, tq)
    blk = (s_len, HEAD_DIM)
    scratch = [pltpu.VMEM((2 * HEAD_DIM, s_len), BF16), pltpu.VMEM((s_len, 2 * HEAD_DIM), BF16),
               pltpu.VMEM((HEAD_DIM, s_len), BF16), pltpu.VMEM((HEAD_DIM, tq), F32),
               pltpu.VMEM((tk, tq), F32), pltpu.VMEM((tk, tq), BF16), pltpu.VMEM((tk, tq), BF16)]
    resident = 5 * _nbytes(blk, BF16) + 8 * _nbytes((tk, tq), F32)
    return pl.pallas_call(
        functools.partial(_fox_prompt_kernel, tq=tq, tk=tk),
        out_shape=jax.ShapeDtypeStruct((n_batch * s_len, D_HEADS), BF16),
        grid=(n_batch, N_HEADS),
        in_specs=[pl.BlockSpec(blk, lambda b, h: (b, h)),
                  pl.BlockSpec(blk, lambda b, h: (b, N_HEADS + h)),
                  pl.BlockSpec(blk, lambda b, h: (b, 2 * N_HEADS + h)),
                  pl.BlockSpec((None, 1, s_len), lambda b, h: (b * N_HEADS + h, 0, 0))],
        out_specs=pl.BlockSpec(blk, lambda b, h: (b, h)),
        scratch_shapes=scratch,
        compiler_params=pltpu.CompilerParams(
            dimension_semantics=("arbitrary", "arbitrary"),
            vmem_limit_bytes=_vmem_limit(4 * _nbytes(blk, BF16) + _nbytes((8, s_len), F32), resident)),
        name="fox_prompt",
    )(zq, zq, zq, f_rows)


BIAS_ROWS = 4 * REL_CLIP
BIAS_C0 = BAND_PAD - 2 * REL_CLIP


def _band_bias_kernel(tbl_ref, mbt_ref, mbs_ref, g_scr):
    h = pl.program_id(0)
    far = tbl_ref[h, 2 * REL_CLIP]
    diff = (lax.broadcasted_iota(jnp.int32, (BIAS_ROWS, LANES), 0) + BIAS_C0
            - lax.broadcasted_iota(jnp.int32, (BIAS_ROWS, LANES), 1))
    g_scr[...] = jnp.full((BIAS_ROWS, LANES), far, F32)
    for tix in range(1, 2 * REL_CLIP):
        d = BAND_PAD + REL_CLIP - tix
        lo = d - BIAS_C0
        a8 = max(0, lo // 8 * 8)
        b8 = min(BIAS_ROWS, -(-(lo + LANES) // 8) * 8)
        g_scr[a8:b8, :] = jnp.where(diff[a8:b8, :] == d, tbl_ref[h, tix], g_scr[a8:b8, :])
    g = g_scr[...]

    def far_rows(n):
        return jnp.full((n, LANES), far, F32)

    def masked(bias, q_ax, k_ax):
        rf = lax.broadcasted_iota(jnp.int32, bias.shape, q_ax) // CHUNK
        cf = lax.broadcasted_iota(jnp.int32, bias.shape, k_ax) // CHUNK
        visible = (cf >= rf) & (cf <= rf + N_PREV_CHUNKS)
        return jnp.where(visible, bias * LOG2E, NEG_INF)

    blocks = [jnp.concatenate([far_rows(BIAS_C0 + LANES * j), g[0:BAND_W - BIAS_C0 - LANES * j]], axis=0)
              for j in range(BAND_TQ // LANES)]
    mbt_ref[...] = masked(jnp.concatenate(blocks, axis=1), 1, 0)
    near = g[0:BAND_W_S - BIAS_C0].T
    mbs_ref[...] = masked(jnp.concatenate([jnp.full((BAND_TQ_S, BIAS_C0), far, F32), near], axis=1), 0, 1)


def _band_bias(table):
    assert BAND_TQ_S == LANES and BAND_W <= BIAS_C0 + BIAS_ROWS
    return pl.pallas_call(
        _band_bias_kernel,
        out_shape=(jax.ShapeDtypeStruct((N_HEADS, BAND_W, BAND_TQ), F32),
                   jax.ShapeDtypeStruct((N_HEADS, BAND_TQ_S, BAND_W_S), F32)),
        grid=(N_HEADS,),
        in_specs=[pl.BlockSpec(memory_space=pltpu.SMEM)],
        out_specs=(pl.BlockSpec((None, BAND_W, BAND_TQ), lambda h: (h, 0, 0)),
                   pl.BlockSpec((None, BAND_TQ_S, BAND_W_S), lambda h: (h, 0, 0))),
        scratch_shapes=[pltpu.VMEM((BIAS_ROWS, LANES), F32)],
        compiler_params=pltpu.CompilerParams(dimension_semantics=("arbitrary",)),
        name="band_bias",
    )(table.T)


def _band_prompt_kernel(q_ref, k_ref, v_ref, mb_ref, o_ref, qt_scr, kpad, vtpad, s_scr, pa_scr, pb_scr):
    s_len = q_ref.shape[0]
    kpad[0:BAND_PAD, :] = jnp.zeros((BAND_PAD, HEAD_DIM), BF16)
    kpad[BAND_PAD:, :] = k_ref[...]
    vtpad[:, 0:BAND_PAD] = jnp.zeros((HEAD_DIM, BAND_PAD), BF16)

    def setup(ci, carry):
        c0 = pl.multiple_of(ci * BAND_TQ, BAND_TQ)
        qt_scr[:, pl.ds(c0, BAND_TQ)] = _transpose_bf16(q_ref[pl.ds(c0, BAND_TQ), :])
        vtpad[:, pl.ds(pl.multiple_of(BAND_PAD + c0, BAND_TQ), BAND_TQ)] = _transpose_bf16(v_ref[pl.ds(c0, BAND_TQ), :])
        return carry

    n_tiles = s_len // BAND_TQ
    lax.fori_loop(0, n_tiles, setup, 0)
    key_ix = lax.broadcasted_iota(jnp.int32, (BAND_W, BAND_TQ), 0)

    def scores(ti):
        r0 = pl.multiple_of(ti * BAND_TQ, BAND_TQ)
        return _dot(kpad[pl.ds(r0, BAND_W), :], qt_scr[:, pl.ds(r0, BAND_TQ)]) + mb_ref[...]

    def softmax(s, p_ref):
        m = jnp.max(s, axis=0, keepdims=True)
        p = jnp.exp2(s - m)
        p_ref[...] = p.astype(BF16)
        return 1.0 / jnp.sum(p, axis=0, keepdims=True)

    def emit(ti, p_ref, r_den):
        r0 = pl.multiple_of(ti * BAND_TQ, BAND_TQ)
        o_t = _dot(vtpad[:, pl.ds(r0, BAND_W)], p_ref[...]) * r_den
        o_ref[pl.ds(r0, BAND_TQ), :] = o_t.T.astype(BF16)

    assert BAND_PAD // BAND_TQ == 2
    h0 = jnp.where(key_ix >= BAND_PAD, scores(0), NEG_INF)
    h1 = jnp.where(BAND_TQ + key_ix >= BAND_PAD, scores(1), NEG_INF)
    s_scr[...] = scores(2)
    r_0 = softmax(h0, pa_scr)
    emit(0, pa_scr, r_0)
    r_1 = softmax(h1, pb_scr)

    def pair(pi, r_pend):
        a = 2 + 2 * pi
        s_b = scores(a + 1)
        emit(a - 1, pb_scr, r_pend)
        r_a = softmax(s_scr[...], pa_scr)
        s_scr[...] = scores(jnp.minimum(a + 2, n_tiles - 1))
        emit(a, pa_scr, r_a)
        return softmax(s_b, pb_scr)

    r_last = lax.fori_loop(0, (n_tiles - 2) // 2, pair, r_1)
    emit(n_tiles - 1, pb_scr, r_last)


def _band_prompt(zq, mbt, *, n_batch, s_len):
    blk = (s_len, HEAD_DIM)
    off = 3 * N_HEADS
    assert (s_len // BAND_TQ - BAND_PAD // BAND_TQ) % 2 == 0
    scratch = [pltpu.VMEM((HEAD_DIM, s_len), BF16), pltpu.VMEM((BAND_PAD + s_len, HEAD_DIM), BF16),
               pltpu.VMEM((HEAD_DIM, BAND_PAD + s_len), BF16), pltpu.VMEM((BAND_W, BAND_TQ), F32),
               pltpu.VMEM((BAND_W, BAND_TQ), BF16), pltpu.VMEM((BAND_W, BAND_TQ), BF16)]
    resident = 3 * _nbytes((BAND_PAD + s_len, HEAD_DIM), BF16) + 8 * _nbytes((BAND_W, BAND_TQ), F32)
    return pl.pallas_call(
        _band_prompt_kernel,
        out_shape=jax.ShapeDtypeStruct((n_batch * s_len, D_HEADS), BF16),
        grid=(n_batch, N_HEADS),
        in_specs=[pl.BlockSpec(blk, lambda b, h: (b, off + h)),
                  pl.BlockSpec(blk, lambda b, h: (b, off + N_HEADS + h)),
                  pl.BlockSpec(blk, lambda b, h: (b, off + 2 * N_HEADS + h)),
                  pl.BlockSpec((None, BAND_W, BAND_TQ), lambda b, h: (h, 0, 0))],
        out_specs=pl.BlockSpec(blk, lambda b, h: (b, h)),
        scratch_shapes=scratch,
        compiler_params=pltpu.CompilerParams(
            dimension_semantics=("arbitrary", "arbitrary"),
            vmem_limit_bytes=_vmem_limit(4 * _nbytes(blk, BF16) + _nbytes((BAND_W, BAND_TQ), F32), resident)),
        name="band_prompt",
    )(zq, zq, zq, mbt)


def _fox_sample_kernel(q_ref, kn_ref, vn_ref, ck_ref, cv_ref, f_ref, o_ref,
                       m_scr, l_scr, acc_scr, fq_scr, *, past, tk):
    kt = pl.program_id(1)
    t_new = q_ref.shape[0]

    def heads(h):
        return slice(h * HEAD_DIM, (h + 1) * HEAD_DIM)

    @pl.when(kt == 0)
    def _():
        for h in range(N_HEADS):
            fq_scr[h] = _col_bcast(f_ref[h:h + 1, past:past + t_new] * LOG2E, t_new)
        m_scr[...] = jnp.full(m_scr.shape, NEG_INF, F32)
        l_scr[...] = jnp.zeros(l_scr.shape, F32)
        acc_scr[...] = jnp.zeros(acc_scr.shape, F32)

    def update(h, s, v):
        m_old = m_scr[h]
        m_new = jnp.maximum(m_old, jnp.max(s, axis=1, keepdims=True))
        alpha = jnp.exp2(m_old - m_new)
        p = jnp.exp2(s - m_new)
        l_scr[h] = alpha * l_scr[h] + jnp.sum(p, axis=1, keepdims=True)
        acc_scr[h] = alpha * acc_scr[h] + _dot(p.astype(BF16), v)
        m_scr[h] = m_new

    c0 = pl.multiple_of(kt * tk, tk)
    scores = []
    for h in range(N_HEADS):
        k = ck_ref[pl.ds(h, tk, stride=N_HEADS), :].astype(BF16)
        scores.append(_dot_nt(q_ref[:, heads(h)], k) + _tile_lanes(fq_scr[h], tk)
                      - f_ref[h:h + 1, pl.ds(c0, tk)] * LOG2E)
    for h in range(N_HEADS):
        update(h, scores[h], cv_ref[pl.ds(h, tk, stride=N_HEADS), :].astype(BF16))

    @pl.when(kt == pl.num_programs(1) - 1)
    def _():
        r = lax.broadcasted_iota(jnp.int32, (t_new, t_new), 0)
        c = lax.broadcasted_iota(jnp.int32, (t_new, t_new), 1)
        for h in range(N_HEADS):
            q = q_ref[:, heads(h)]
            f_new = f_ref[h:h + 1, past:past + t_new] * LOG2E
            s = _dot_nt(q, kn_ref[:, heads(h)]) + fq_scr[h][:, :t_new] - f_new
            s = jnp.where(c <= r, s, NEG_INF)
            update(h, s, vn_ref[:, heads(h)])
            o_ref[:, heads(h)] = (acc_scr[h] / l_scr[h]).astype(BF16)


def _fox_sample(zq, cache_k, cache_v, f_all, *, n_batch, t_new, past):
    tk = min(2048, past)
    lp = f_all.shape[-1]
    blk = (t_new, D_HEADS)
    cblk = (None, tk * N_HEADS, HEAD_DIM)
    return pl.pallas_call(
        functools.partial(_fox_sample_kernel, past=past, tk=tk),
        out_shape=jax.ShapeDtypeStruct((n_batch * t_new, D_HEADS), BF16),
        grid=(n_batch, past // tk),
        in_specs=[pl.BlockSpec(blk, lambda b, k: (b, 0)),
                  pl.BlockSpec(blk, lambda b, k: (b, 1)),
                  pl.BlockSpec(blk, lambda b, k: (b, 2)),
                  pl.BlockSpec(cblk, lambda b, k: (b, k, 0)),
                  pl.BlockSpec(cblk, lambda b, k: (b, k, 0)),
                  pl.BlockSpec((None, N_HEADS, lp), lambda b, k: (b, 0, 0))],
        out_specs=pl.BlockSpec(blk, lambda b, k: (b, 0)),
        scratch_shapes=[pltpu.VMEM((N_HEADS, t_new, 1), F32), pltpu.VMEM((N_HEADS, t_new, 1), F32),
                        pltpu.VMEM((N_HEADS, t_new, HEAD_DIM), F32),
                        pltpu.VMEM((N_HEADS, t_new, LANES), F32)],
        compiler_params=pltpu.CompilerParams(
            dimension_semantics=("arbitrary", "arbitrary"),
            vmem_limit_bytes=_vmem_limit(2 * _nbytes((tk, D_HEADS), F32) + 4 * _nbytes(blk, BF16)
                                         + _nbytes((N_HEADS, lp), F32), 8 * MIB)),
        name="fox_sample",
    )(zq, zq, zq, cache_k, cache_v, f_all)


def _band_sample_kernel(q_ref, kn_ref, vn_ref, ck_ref, cv_ref, mb_ref, o_ref):
    t_new = q_ref.shape[0]
    lb = ck_ref.shape[0] // N_HEADS
    for h in range(N_HEADS):
        hs = slice(h * HEAD_DIM, (h + 1) * HEAD_DIM)
        q = q_ref[:, hs]
        s1 = _dot_nt(q, ck_ref[pl.ds(h, lb, stride=N_HEADS), :].astype(BF16)) + mb_ref[h, :, 0:lb]
        s2 = _dot_nt(q, kn_ref[:, hs]) + mb_ref[h, :, lb:lb + t_new]
        m = jnp.maximum(jnp.max(s1, axis=1, keepdims=True), jnp.max(s2, axis=1, keepdims=True))
        p1 = jnp.exp2(s1 - m)
        p2 = jnp.exp2(s2 - m)
        den = jnp.sum(p1, axis=1, keepdims=True) + jnp.sum(p2, axis=1, keepdims=True)
        o = _dot(p1.astype(BF16), cv_ref[pl.ds(h, lb, stride=N_HEADS), :].astype(BF16)) + _dot(p2.astype(BF16), vn_ref[:, hs])
        o_ref[:, hs] = (o / den).astype(BF16)


def _band_sample(zq, cache_k, cache_v, mb, *, n_batch, t_new):
    lb = cache_k.shape[1] // N_HEADS
    blk = (t_new, D_HEADS)
    cblk = (None, lb * N_HEADS, HEAD_DIM)
    return pl.pallas_call(
        _band_sample_kernel,
        out_shape=jax.ShapeDtypeStruct((n_batch * t_new, D_HEADS), BF16),
        grid=(n_batch,),
        in_specs=[pl.BlockSpec(blk, lambda b: (b, 3)),
                  pl.BlockSpec(blk, lambda b: (b, 4)),
                  pl.BlockSpec(blk, lambda b: (b, 5)),
                  pl.BlockSpec(cblk, lambda b: (b, 0, 0)),
                  pl.BlockSpec(cblk, lambda b: (b, 0, 0)),
                  pl.BlockSpec((N_HEADS, t_new, BAND_W_S), lambda b: (0, 0, 0))],
        out_specs=pl.BlockSpec(blk, lambda b: (b, 0)),
        compiler_params=pltpu.CompilerParams(
            dimension_semantics=("arbitrary",),
            vmem_limit_bytes=_vmem_limit(2 * _nbytes((lb, D_HEADS), F32) + 4 * _nbytes(blk, BF16),
                                         _nbytes((N_HEADS, t_new, BAND_W_S), F32) + 4 * MIB)),
        name="band_sample",
    )(zq, zq, zq, cache_k, cache_v, mb)


def _merge_kernel(oa_ref, ob_ref, za_ref, zb_ref, x_ref, gt_ref, woa_ref, wob_ref, wout_ref, o_ref, *, nb, t):
    a = _dot(oa_ref[...], woa_ref[...])
    b = _dot(ob_ref[...], wob_ref[...])
    mix = _sigmoid(za_ref[...].astype(F32)) * a + _sigmoid(zb_ref[...].astype(F32)) * b
    y = _dot(mix.astype(BF16), wout_ref[...])
    x = x_ref[...]
    y, gt = _per_batch(y, gt_ref, nb, t)
    x3, _ = _per_batch(x, gt_ref, nb, t)
    o_ref[...] = (x3 + gt * y).reshape(x.shape)


def _merge(oa, ob, zq, x2d, gate, w_oa, w_ob, w_out, *, nb, t, tiles_per_batch):
    m, d = x2d.shape
    tm = nb * t
    za_blk = Z_COLS // d
    if nb == 1:
        mod_map = lambda i: (i // tiles_per_batch, 0, 0)
    else:
        mod_map = lambda i: (i, 0, 0)
    const = lambda i: (0, 0)
    pipelined = (2 * _nbytes((tm, D_HEADS), BF16) + 2 * _nbytes((tm, d), BF16) + 2 * _nbytes((tm, d), F32))
    resident = (2 * (2 * _nbytes((D_HEADS, d), BF16) + _nbytes((d, d), BF16)) + 4 * _nbytes((tm, d), F32))
    return pl.pallas_call(
        functools.partial(_merge_kernel, nb=nb, t=t),
        out_shape=jax.ShapeDtypeStruct((m, d), F32),
        grid=(m // tm,),
        in_specs=[pl.BlockSpec((tm, D_HEADS), lambda i: (i, 0)),
                  pl.BlockSpec((tm, D_HEADS), lambda i: (i, 0)),
                  pl.BlockSpec((tm, d), lambda i: (i, za_blk)),
                  pl.BlockSpec((tm, d), lambda i: (i, za_blk + 1)),
                  pl.BlockSpec((tm, d), lambda i: (i, 0)),
                  pl.BlockSpec((nb, 1, d), mod_map),
                  pl.BlockSpec((D_HEADS, d), const),
                  pl.BlockSpec((D_HEADS, d), const),
                  pl.BlockSpec((d, d), const)],
        out_specs=pl.BlockSpec((tm, d), lambda i: (i, 0)),
        compiler_params=pltpu.CompilerParams(
            dimension_semantics=("arbitrary",),
            vmem_limit_bytes=_vmem_limit(pipelined, resident)),
        name="merge_out",
    )(oa, ob, zq, zq, x2d, gate, w_oa, w_ob, w_out)


def _ffn_kernel(x_ref, sh_ref, sc_ref, gt_ref, g_ref, gf_ref, wg_ref, wu_ref, wd_ref, o_ref,
                h_scr, acc_scr, *, nb, t):
    f = pl.program_id(1)

    @pl.when(f == 0)
    def _():
        h_scr[...] = _mod_norm(x_ref[...], g_ref, sc_ref, sh_ref, nb, t).astype(BF16)
        acc_scr[...] = jnp.zeros(acc_scr.shape, F32)

    h = h_scr[...]
    gate = _dot(h, wg_ref[...])
    up = _dot(h, wu_ref[...])
    act = (gate * _sigmoid(gate) * up).astype(BF16)
    acc_scr[...] += _dot(act, wd_ref[...])

    @pl.when(f == pl.num_programs(1) - 1)
    def _():
        x = x_ref[...]
        y, gt = _per_batch(acc_scr[...], gt_ref, nb, t)
        x3, _ = _per_batch(x, gt_ref, nb, t)
        x2 = (x3 + gt * y).reshape(x.shape)
        o_ref[...] = _rms_scale(x2) * gf_ref[...]


def _ffn(x2d, shift, scale, gate, g_ffn, g_final, w_gate, w_up, w_down, *, nb, t, tiles_per_batch):
    m, d = x2d.shape
    dff = w_gate.shape[1]
    tm = nb * t
    tf = 512
    if nb == 1:
        mod_map = lambda i, f: (i // tiles_per_batch, 0, 0)
    else:
        mod_map = lambda i, f: (i, 0, 0)
    pipelined = 2 * _nbytes((tm, d), F32) + 3 * _nbytes((d, tf), BF16)
    resident = _nbytes((tm, d), BF16) + _nbytes((tm, d), F32) + 3 * _nbytes((tm, tf), F32) + 2 * _nbytes((tm, d), F32)
    return pl.pallas_call(
        functools.partial(_ffn_kernel, nb=nb, t=t),
        out_shape=jax.ShapeDtypeStruct((m, d), F32),
        grid=(m // tm, dff // tf),
        in_specs=[pl.BlockSpec((tm, d), lambda i, f: (i, 0)),
                  pl.BlockSpec((nb, 1, d), mod_map),
                  pl.BlockSpec((nb, 1, d), mod_map),
                  pl.BlockSpec((nb, 1, d), mod_map),
                  pl.BlockSpec((1, d), lambda i, f: (0, 0)),
                  pl.BlockSpec((1, d), lambda i, f: (0, 0)),
                  pl.BlockSpec((d, tf), lambda i, f: (0, f)),
                  pl.BlockSpec((d, tf), lambda i, f: (0, f)),
                  pl.BlockSpec((tf, d), lambda i, f: (f, 0))],
        out_specs=pl.BlockSpec((tm, d), lambda i, f: (i, 0)),
        scratch_shapes=[pltpu.VMEM((tm, d), BF16), pltpu.VMEM((tm, d), F32)],
        compiler_params=pltpu.CompilerParams(
            dimension_semantics=("arbitrary", "arbitrary"),
            vmem_limit_bytes=_vmem_limit(pipelined, resident)),
        name="ffn_final",
    )(x2d, shift, scale, gate, g_ffn, g_final, w_gate, w_up, w_down)


def _row_tile(s_len, cap):
    tile = min(cap, s_len)
    assert s_len % tile == 0
    return tile


def kernel(x_prompt, x_sample, cache_fox_k, cache_fox_v, cache_fox_logf, cache_band_k, cache_band_v,
           c_prompt, c_sample, w_ada, b_ada, g_mix, w_in, b_f, rel_bias, w_oa, w_ob, w_out,
           g_ffn, w_gate, w_up, w_down, g_final):
    n_b, s_len, d = x_prompt.shape
    n_db, t_new, _ = x_sample.shape
    past = cache_fox_k.shape[2]
    lb = cache_band_k.shape[2]
    assert w_ada.shape[0] == 1, "single-layer trunk"
    assert lb == BAND_PAD and s_len >= BAND_PAD and s_len % FOX_TQ == 0
    assert Z_COLS % d == 0 and t_new % 8 == 0 and t_new + lb <= BAND_W_S and t_new <= CHUNK
    layer = 0
    q_scale = LOG2E / math.sqrt(HEAD_DIM)

    w_l = w_in[layer]
    w_main = jnp.concatenate([w_l[:, :3 * D_HEADS], w_l[:, 3 * D_HEADS + N_HEADS:]], axis=1).astype(BF16)
    w_f = jnp.pad(w_l[:, 3 * D_HEADS:3 * D_HEADS + N_HEADS], ((0, 0), (0, LANES - N_HEADS))).astype(BF16)
    n_main = w_main.shape[1]
    col = jnp.arange(n_main)
    is_q = (col < D_HEADS) | ((col >= 3 * D_HEADS) & (col < 4 * D_HEADS))
    col_scale = jnp.where(is_q, q_scale, 1.0).astype(F32).reshape(1, n_main)
    b_f2 = b_f[layer].reshape(1, N_HEADS)
    g_mix2 = g_mix[layer].reshape(1, d)
    g_ffn2 = g_ffn[layer].reshape(1, d)
    g_fin2 = g_final.reshape(1, d)
    w_oa_b, w_ob_b, w_out_b = w_oa[layer].astype(BF16), w_ob[layer].astype(BF16), w_out[layer].astype(BF16)
    w_gate_b, w_up_b, w_down_b = w_gate[layer].astype(BF16), w_up[layer].astype(BF16), w_down[layer].astype(BF16)

    mods = _ada(jnp.concatenate([c_prompt, c_sample], axis=0), w_ada[layer], b_ada[layer])

    def chunks(rows):
        return [rows[:, k * d:(k + 1) * d][:, None, :] for k in range(6)]

    sh1p, sc1p, gt1p, sh2p, sc2p, gt2p = chunks(mods[:n_b])
    sh1s, sc1s, gt1s, sh2s, sc2s, gt2s = chunks(mods[n_b:])
    mbt, mbs = _band_bias(rel_bias[layer])

    xp = x_prompt.reshape(n_b * s_len, d)
    tm = _row_tile(s_len, 1024)
    zq, ka, va, kb, vb, lf = _mixer_in(xp, sh1p, sc1p, g_mix2, w_main, col_scale, w_f, b_f2,
                                       nb=1, t=tm, tiles_per_batch=s_len // tm, tail=BAND_PAD)
    lf_rows = lf.reshape(n_b, s_len, N_HEADS).transpose(0, 2, 1).reshape(n_b * N_HEADS, s_len)
    f_rows = _cumsum_rows(lf_rows).reshape(n_b * N_HEADS, 1, s_len)
    oa = _fox_prompt(zq, f_rows, n_batch=n_b, s_len=s_len)
    ob = _band_prompt(zq, mbt, n_batch=n_b, s_len=s_len)
    tm = _row_tile(s_len, 256)
    x1 = _merge(oa, ob, zq, xp, gt1p, w_oa_b, w_ob_b, w_out_b, nb=1, t=tm, tiles_per_batch=s_len // tm)
    tm = _row_tile(s_len, 512)
    y_prompt = _ffn(x1, sh2p, sc2p, gt2p, g_ffn2, g_fin2, w_gate_b, w_up_b, w_down_b,
                    nb=1, t=tm, tiles_per_batch=s_len // tm).reshape(n_b, s_len, d)
    fox_shape = (1, n_b, s_len, N_HEADS, HEAD_DIM)
    band_shape = (1, n_b, BAND_PAD, N_HEADS, HEAD_DIM)
    prompt_out = (ka.reshape(fox_shape), va.reshape(fox_shape), lf.reshape(1, n_b, s_len, N_HEADS),
                  kb.reshape(band_shape), vb.reshape(band_shape))

    xs = x_sample.reshape(n_db * t_new, d)
    zqs, kas, vas, kbs, vbs, lfs = _mixer_in(xs, sh1s, sc1s, g_mix2, w_main, col_scale, w_f, b_f2,
                                             nb=n_db, t=t_new, tiles_per_batch=1, tail=n_db * t_new)
    lp = -(-(past + t_new) // 2048) * 2048
    lf_all = jnp.concatenate([cache_fox_logf[layer].transpose(0, 2, 1),
                              lfs.reshape(n_db, t_new, N_HEADS).transpose(0, 2, 1),
                              jnp.zeros((n_db, N_HEADS, lp - past - t_new), F32)], axis=-1)
    f_all = _cumsum_rows(lf_all.reshape(n_db * N_HEADS, lp)).reshape(n_db, N_HEADS, lp)
    oas = _fox_sample(zqs, cache_fox_k[layer].reshape(n_db, past * N_HEADS, HEAD_DIM),
                      cache_fox_v[layer].reshape(n_db, past * N_HEADS, HEAD_DIM), f_all,
                      n_batch=n_db, t_new=t_new, past=past)
    obs = _band_sample(zqs, cache_band_k[layer].reshape(n_db, lb * N_HEADS, HEAD_DIM),
                       cache_band_v[layer].reshape(n_db, lb * N_HEADS, HEAD_DIM), mbs, n_batch=n_db, t_new=t_new)
    x1s = _merge(oas, obs, zqs, xs, gt1s, w_oa_b, w_ob_b, w_out_b, nb=n_db, t=t_new, tiles_per_batch=1)
    y_sample = _ffn(x1s, sh2s, sc2s, gt2s, g_ffn2, g_fin2, w_gate_b, w_up_b, w_down_b,
                    nb=n_db, t=t_new, tiles_per_batch=1).reshape(n_db, t_new, d)
    s_shape = (1, n_db, t_new, N_HEADS, HEAD_DIM)
    sample_out = (kas.reshape(s_shape), vas.reshape(s_shape), lfs.reshape(1, n_db, t_new, N_HEADS),
                  kbs.reshape(s_shape), vbs.reshape(s_shape))

    return (y_prompt, y_sample) + prompt_out + sample_out
```

```python
import functools
import math

import jax
import jax.numpy as jnp
from jax import lax
from jax.experimental import pallas as pl
from jax.experimental.pallas import tpu as pltpu

F32 = jnp.float32
BF16 = jnp.bfloat16

HEAD_DIM = 128
N_HEADS = 8
D_HEADS = N_HEADS * HEAD_DIM
CHUNK = 64
N_PREV_CHUNKS = 8
BAND_PAD = N_PREV_CHUNKS * CHUNK
REL_CLIP = 128
RMS_EPS = 1e-6
NEG_INF = -1e30
LOG2E = math.log2(math.e)
LANES = 128
Z_COLS = 6 * D_HEADS

VMEM_CAP_BYTES = 60 * 1024 * 1024
MIB = 1024 * 1024

BAND_TQ = 256
BAND_W = BAND_PAD + BAND_TQ
BAND_TQ_S = 128
BAND_W_S = BAND_PAD + BAND_TQ_S
FOX_TQ = 512
FOX_TK = 256


def _vmem_limit(pipelined_bytes, resident_bytes=0):
    est = 2 * pipelined_bytes + resident_bytes + 8 * MIB
    return int(min(max(est, 16 * MIB), VMEM_CAP_BYTES))


def _nbytes(shape, dtype):
    return math.prod(shape) * jnp.dtype(dtype).itemsize


def _dot(a, b):
    return jnp.dot(a, b, preferred_element_type=F32)


def _dot_nt(a, b):
    return lax.dot_general(a, b, (((1,), (1,)), ((), ())), preferred_element_type=F32)


def _split3(x):
    hi = x.astype(BF16)
    r1 = x - hi.astype(F32)
    mid = r1.astype(BF16)
    lo = (r1 - mid.astype(F32)).astype(BF16)
    return hi, mid, lo


def _sigmoid(x):
    return 1.0 / (1.0 + jnp.exp(-x))


def _rms_scale(x):
    return x * lax.rsqrt(jnp.mean(x * x, axis=-1, keepdims=True) + RMS_EPS)


def _per_batch(y, ref, nb, t):
    if nb == 1:
        return y, ref[0]
    return y.reshape(nb, t, y.shape[-1]), ref[...]


def _mod_norm(x, g_ref, sc_ref, sh_ref, nb, t):
    y = _rms_scale(x) * g_ref[...]
    y, sc = _per_batch(y, sc_ref, nb, t)
    _, sh = _per_batch(x, sh_ref, nb, t)
    h = y * (1.0 + sc) + sh
    return h.reshape(x.shape)


def _col_bcast(frow, n):
    r = lax.broadcasted_iota(jnp.int32, (n, n), 0)
    c = lax.broadcasted_iota(jnp.int32, (n, n), 1)
    d = jnp.where(r == c, jnp.broadcast_to(frow, (n, n)), 0.0)
    ones = jnp.ones((n, LANES), BF16)
    hi, mid, lo = _split3(d)
    return _dot(hi, ones) + _dot(mid, ones) + _dot(lo, ones)


def _tile_lanes(x, n):
    reps = n // x.shape[1]
    return x if reps == 1 else jnp.concatenate([x] * reps, axis=1)


def _transpose_bf16(x):
    return x.astype(F32).T.astype(BF16)


def _ada_kernel(c_ref, w_ref, b_ref, o_ref):
    c = c_ref[...]
    a = (c * _sigmoid(c)).astype(BF16)
    o_ref[...] = _dot(a, w_ref[...].astype(BF16)) + b_ref[...]


def _ada(c, w, b):
    n, d = c.shape
    nout = w.shape[1]
    tn = 1024 if nout % 1024 == 0 else 512
    return pl.pallas_call(
        _ada_kernel,
        out_shape=jax.ShapeDtypeStruct((n, nout), F32),
        grid=(nout // tn,),
        in_specs=[pl.BlockSpec((n, d), lambda j: (0, 0)),
                  pl.BlockSpec((d, tn), lambda j: (0, j)),
                  pl.BlockSpec((1, tn), lambda j: (0, j))],
        out_specs=pl.BlockSpec((n, tn), lambda j: (0, j)),
        compiler_params=pltpu.CompilerParams(
            dimension_semantics=("arbitrary",),
            vmem_limit_bytes=_vmem_limit(_nbytes((d, tn), F32) + _nbytes((n, tn), F32),
                                         _nbytes((n, d), F32) + _nbytes((d, tn), BF16))),
        name="ada",
    )(c, w, b.reshape(1, nout))


SHIFT = N_HEADS


def _w_prep_kernel(a_ref, b_ref, o_ref, *, tn, first_shifted):
    j = pl.program_id(1)
    a = a_ref[...]
    a_rot = pltpu.roll(a, tn - SHIFT, axis=1)
    b_rot = pltpu.roll(b_ref[...], LANES - SHIFT, axis=1)
    lane = lax.broadcasted_iota(jnp.int32, b_rot.shape, 1)
    tail = jnp.where(lane >= LANES - SHIFT, b_rot, a_rot[:, tn - LANES:])
    shifted = jnp.concatenate([a_rot[:, :tn - LANES], tail], axis=1)
    o_ref[...] = jnp.where(j >= first_shifted, shifted, a).astype(BF16)


def _w_prep(w):
    d, n_in = w.shape
    n = n_in - SHIFT
    tn = 1024 if n % 1024 == 0 else 512
    rt = min(512, d)
    first = 3 * D_HEADS // tn
    assert n % tn == 0 and (3 * D_HEADS) % tn == 0 and d % rt == 0
    nxt = jnp.stack([w[:, (j + 1) * tn:(j + 1) * tn + SHIFT] for j in range(first, n // tn)], axis=1)
    nxt = jnp.pad(nxt, ((0, 0), (0, 0), (0, LANES - SHIFT))).reshape(d, (n // tn - first) * LANES)
    return pl.pallas_call(
        functools.partial(_w_prep_kernel, tn=tn, first_shifted=first),
        out_shape=jax.ShapeDtypeStruct((d, n), BF16),
        grid=(d // rt, n // tn),
        in_specs=[pl.BlockSpec((rt, tn), lambda i, j: (i, j)),
                  pl.BlockSpec((rt, LANES), lambda i, j: (i, jnp.maximum(j - first, 0)))],
        out_specs=pl.BlockSpec((rt, tn), lambda i, j: (i, j)),
        compiler_params=pltpu.CompilerParams(dimension_semantics=("arbitrary", "arbitrary")),
        name="w_in_prep",
    )(w, nxt)


def _mixer_kernel(x_ref, sh_ref, sc_ref, g_ref, w_ref, cs_ref, wf_ref, bf_ref,
                  zq_ref, ka_ref, va_ref, kb_ref, vb_ref, lf_ref, h_scr,
                  *, nb, t, tiles_per_batch, tail, npg):
    i = pl.program_id(0)
    j = pl.program_id(1)

    @pl.when(j == 0)
    def _():
        h = _mod_norm(x_ref[...], g_ref, sc_ref, sh_ref, nb, t).astype(BF16)
        h_scr[...] = h
        fa = _dot(h, wf_ref[...])[:, :N_HEADS] + bf_ref[...]
        lf_ref[...] = jnp.minimum(fa, 0.0) - jnp.log1p(jnp.exp(-jnp.abs(fa)))

    acc = _dot(h_scr[...], w_ref[...])
    zq_ref[...] = (acc * cs_ref[...]).astype(BF16)

    def _group(g):
        return (j >= g * npg) & (j < (g + 1) * npg)

    @pl.when(_group(1))
    def _():
        ka_ref[...] = acc

    @pl.when(_group(2))
    def _():
        va_ref[...] = acc

    is_tail = (i % tiles_per_batch) == (tiles_per_batch - 1)
    rows = acc.shape[0]

    @pl.when(is_tail & _group(4))
    def _():
        kb_ref[...] = acc[rows - tail:, :]

    @pl.when(is_tail & _group(5))
    def _():
        vb_ref[...] = acc[rows - tail:, :]


def _mixer_in(x2d, shift, scale, g, w_main, col_scale, w_f, b_f, *, nb, t, tiles_per_batch, tail):
    m, d = x2d.shape
    n = w_main.shape[1]
    tm = nb * t
    tn = 1024 if n % 1024 == 0 else 512
    npg = D_HEADS // tn
    n_batches = (m // tm) // tiles_per_batch
    if nb == 1:
        mod_map = lambda i, j: (i // tiles_per_batch, 0, 0)
    else:
        mod_map = lambda i, j: (i, 0, 0)

    def grp_map(gidx):
        return lambda i, j: (i, jnp.clip(j - gidx * npg, 0, npg - 1))

    def tail_map(gidx):
        def f(i, j):
            is_tail = (i % tiles_per_batch) == (tiles_per_batch - 1)
            return (i // tiles_per_batch, jnp.where(is_tail, jnp.clip(j - gidx * npg, 0, npg - 1), 0))
        return f

    kern = functools.partial(_mixer_kernel, nb=nb, t=t, tiles_per_batch=tiles_per_batch, tail=tail, npg=npg)
    pipelined = _nbytes((tm, d), F32) + _nbytes((d, tn), BF16) + _nbytes((tm, tn), BF16)
    resident = (_nbytes((tm, d), BF16) + 2 * _nbytes((tm, tn), F32) + 2 * _nbytes((tail, tn), F32)
                + 2 * _nbytes((tm, tn), F32))
    return pl.pallas_call(
        kern,
        out_shape=(jax.ShapeDtypeStruct((m, n), BF16),
                   jax.ShapeDtypeStruct((m, D_HEADS), F32),
                   jax.ShapeDtypeStruct((m, D_HEADS), F32),
                   jax.ShapeDtypeStruct((n_batches * tail, D_HEADS), F32),
                   jax.ShapeDtypeStruct((n_batches * tail, D_HEADS), F32),
                   jax.ShapeDtypeStruct((m, N_HEADS), F32)),
        grid=(m // tm, n // tn),
        in_specs=[pl.BlockSpec((tm, d), lambda i, j: (i, 0)),
                  pl.BlockSpec((nb, 1, d), mod_map),
                  pl.BlockSpec((nb, 1, d), mod_map),
                  pl.BlockSpec((1, d), lambda i, j: (0, 0)),
                  pl.BlockSpec((d, tn), lambda i, j: (0, j)),
                  pl.BlockSpec((1, tn), lambda i, j: (0, j)),
                  pl.BlockSpec((d, LANES), lambda i, j: (0, 0)),
                  pl.BlockSpec((1, N_HEADS), lambda i, j: (0, 0))],
        out_specs=(pl.BlockSpec((tm, tn), lambda i, j: (i, j)),
                   pl.BlockSpec((tm, tn), grp_map(1), pipeline_mode=pl.Buffered(1)),
                   pl.BlockSpec((tm, tn), grp_map(2), pipeline_mode=pl.Buffered(1)),
                   pl.BlockSpec((tail, tn), tail_map(4), pipeline_mode=pl.Buffered(1)),
                   pl.BlockSpec((tail, tn), tail_map(5), pipeline_mode=pl.Buffered(1)),
                   pl.BlockSpec((tm, N_HEADS), lambda i, j: (i, 0))),
        scratch_shapes=[pltpu.VMEM((tm, d), BF16)],
        compiler_params=pltpu.CompilerParams(
            dimension_semantics=("arbitrary", "arbitrary"),
            vmem_limit_bytes=_vmem_limit(pipelined, resident)),
        name="mixer_in",
    )(x2d, shift, scale, g, w_main, col_scale, w_f, b_f)


def _cumsum_kernel(x_ref, o_ref):
    grp, nr, _ = x_ref.shape
    rows = grp * nr
    r = lax.broadcasted_iota(jnp.int32, (LANES, LANES), 0)
    c = lax.broadcasted_iota(jnp.int32, (LANES, LANES), 1)
    upper = (r <= c).astype(BF16)
    rr = lax.broadcasted_iota(jnp.int32, (rows, rows), 0)
    cc = lax.broadcasted_iota(jnp.int32, (rows, rows), 1)

    def group(ix):
        return sum((ix >= gi * nr).astype(jnp.int32) for gi in range(1, grp))

    lower = ((cc < rr) & (group(cc) == group(rr))).astype(BF16)
    hi, mid, lo = _split3(x_ref[...].reshape(rows, LANES))
    within = _dot(hi, upper) + _dot(mid, upper) + _dot(lo, upper)
    tot = jnp.broadcast_to(within[:, LANES - 1:LANES], (rows, LANES))
    hi, mid, lo = _split3(tot)
    before = _dot(lower, hi) + _dot(lower, mid) + _dot(lower, lo)
    o_ref[...] = (within + before).reshape(grp, nr, LANES)


def _cumsum_rows(x):
    rows, length = x.shape
    nr = length // LANES
    grp = 8
    x3 = x.reshape(rows, nr, LANES)
    out = pl.pallas_call(
        _cumsum_kernel,
        out_shape=jax.ShapeDtypeStruct(x3.shape, F32),
        grid=(rows // grp,),
        in_specs=[pl.BlockSpec((grp, nr, LANES), lambda i: (i, 0, 0))],
        out_specs=pl.BlockSpec((grp, nr, LANES), lambda i: (i, 0, 0)),
        compiler_params=pltpu.CompilerParams(dimension_semantics=("arbitrary",)),
        name="cumsum_logf",
    )(x3)
    return out.reshape(rows, length)


def _fox_prompt_kernel(q_ref, k_ref, v_ref, f_ref, o_ref, qxt_scr, kx_scr, vt_scr, acc_scr, sa_scr, sb_scr,
                       pa_scr, pb_scr,
                       *, tq, tk):
    s_len = q_ref.shape[0]
    assert tq == 2 * tk
    sub = lax.broadcasted_iota(jnp.int32, (HEAD_DIM, tk), 0)
    minus_ones = jnp.where(sub < 3, -1.0, 0.0).astype(BF16)

    def setup(ci, carry):
        c0 = pl.multiple_of(ci * tk, tk)
        rows = pl.ds(c0, tk)
        qxt_scr[0:HEAD_DIM, rows] = _transpose_bf16(q_ref[rows, :])
        qxt_scr[HEAD_DIM:, rows] = minus_ones
        vt_scr[:, rows] = _transpose_bf16(v_ref[rows, :])
        hi, mid, lo = [part.astype(F32) for part in _split3(f_ref[:, rows] * LOG2E)]
        ext_t = jnp.where(sub == 0, hi, jnp.where(sub == 1, mid, jnp.where(sub == 2, lo, 0.0)))
        kx_scr[rows, 0:HEAD_DIM] = k_ref[rows, :]
        kx_scr[rows, HEAD_DIM:] = ext_t.T.astype(BF16)
        return carry

    lax.fori_loop(0, s_len // tk, setup, 0)

    key_ix = lax.broadcasted_iota(jnp.int32, (tq, tq), 0)
    qry_ix = lax.broadcasted_iota(jnp.int32, (tq, tq), 1)

    def scores(r0, c0):
        c0 = pl.multiple_of(c0, tq)
        return _dot(kx_scr[pl.ds(c0, tq), :], qxt_scr[:, pl.ds(r0, tq)])

    def softmax(s, f_q, m_old, l_old, p_ref):
        t_max = jnp.max(s, axis=0, keepdims=True) + f_q
        if m_old is None:
            m_new, alpha = t_max, None
        else:
            m_new = jnp.maximum(m_old, t_max)
            alpha = jnp.exp2(m_old - m_new)
        p = jnp.exp2(s - (m_new - f_q))
        p_sum = jnp.sum(p, axis=0, keepdims=True)
        p_ref[...] = p.astype(BF16)
        return m_new, (p_sum if m_old is None else alpha * l_old + p_sum), alpha

    def accumulate(c0, p_ref, alpha):
        c0 = pl.multiple_of(c0, tk)
        pv = _dot(vt_scr[:, pl.ds(c0, tk)], p_ref[...])
        acc_scr[...] = pv if alpha is None else alpha * acc_scr[...] + pv

    def open_block(r0):
        d = jnp.where(key_ix <= qry_ix, scores(r0, r0), NEG_INF)
        sa_scr[...] = scores(r0, 0)
        return d[0:tk], d[tk:]

    def diag_block(r0, f_q, d0, d1):
        m, l, _ = softmax(d0, f_q, None, None, pa_scr)
        accumulate(r0, pa_scr, None)
        m, l, alpha = softmax(d1, f_q, m, l, pb_scr)
        return r0 + tk, alpha, m, l

    def key_block(s_ref, c0, f_q, state):
        c_pend, alpha_pend, m, l = state
        accumulate(c_pend, pb_scr, alpha_pend)
        m, l, alpha_0 = softmax(s_ref[0:tk, :], f_q, m, l, pa_scr)
        accumulate(c0, pa_scr, alpha_0)
        m, l, alpha_1 = softmax(s_ref[tk:, :], f_q, m, l, pb_scr)
        return c0 + tk, alpha_1, m, l

    def close_block(r0, c_pend, alpha_pend, l):
        accumulate(c_pend, pb_scr, alpha_pend)
        o_t = acc_scr[...] * (1.0 / l)
        o_ref[pl.ds(pl.multiple_of(r0, tq), tq), :] = o_t.T.astype(BF16)

    def f_query(r0):
        return f_ref[:, pl.ds(r0, tq)] * LOG2E

    d0, d1 = open_block(0)
    c_pend, alpha_pend, _, l = diag_block(0, f_query(0), d0, d1)

    def q_body(qi, prev):
        r0 = pl.multiple_of(qi * tq, tq)
        f_q = f_query(r0)
        d0, d1 = open_block(r0)
        close_block(r0 - tq, *prev)
        state = diag_block(r0, f_q, d0, d1)

        def pair_body(pi, state):
            c0 = 2 * pi * tq
            sb_scr[...] = scores(r0, c0 + tq)
            state = key_block(sa_scr, c0, f_q, state)
            sa_scr[...] = scores(r0, c0 + 2 * tq)
            return key_block(sb_scr, c0 + tq, f_q, state)

        def last_body(_, state):
            return key_block(sa_scr, (qi - 1) * tq, f_q, state)

        state = lax.fori_loop(0, lax.shift_right_logical(qi, 1), pair_body, state)
        c_pend, alpha_pend, _, l = lax.fori_loop(0, qi & 1, last_body, state)
        return c_pend, alpha_pend, l

    n_q = s_len // tq
    last = lax.fori_loop(1, n_q, q_body, (c_pend, alpha_pend, l))
    close_block((n_q - 1) * tq, *last)


def _fox_prompt(zq, f_rows, *, n_batch, s_len):
    tq = min(FOX_TQ, s_len)
    tk = min(FOX_TK, tq)
    blk = (s_len, HEAD_DIM)
    scratch = [pltpu.VMEM((2 * HEAD_DIM, s_len), BF16), pltpu.VMEM((s_len, 2 * HEAD_DIM), BF16),
               pltpu.VMEM((HEAD_DIM, s_len), BF16), pltpu.VMEM((HEAD_DIM, tq), F32),
               pltpu.VMEM((tq, tq), F32), pltpu.VMEM((tq, tq), F32),
               pltpu.VMEM((tk, tq), BF16), pltpu.VMEM((tk, tq), BF16)]
    resident = 5 * _nbytes(blk, BF16) + 8 * _nbytes((tk, tq), F32)
    return pl.pallas_call(
        functools.partial(_fox_prompt_kernel, tq=tq, tk=tk),
        out_shape=jax.ShapeDtypeStruct((n_batch * s_len, D_HEADS), BF16),
        grid=(n_batch, N_HEADS),
        in_specs=[pl.BlockSpec(blk, lambda b, h: (b, h)),
                  pl.BlockSpec(blk, lambda b, h: (b, N_HEADS + h)),
                  pl.BlockSpec(blk, lambda b, h: (b, 2 * N_HEADS + h)),
                  pl.BlockSpec((None, 1, s_len), lambda b, h: (b * N_HEADS + h, 0, 0))],
        out_specs=pl.BlockSpec(blk, lambda b, h: (b, h)),
        scratch_shapes=scratch,
        compiler_params=pltpu.CompilerParams(
            dimension_semantics=("arbitrary", "arbitrary"),
            vmem_limit_bytes=_vmem_limit(4 * _nbytes(blk, BF16) + _nbytes((8, s_len), F32), resident)),
        name="fox_prompt",
    )(zq, zq, zq, f_rows)


BIAS_ROWS = 4 * REL_CLIP
BIAS_C0 = BAND_PAD - 2 * REL_CLIP


def _band_bias_kernel(tbl_ref, mbt_ref, mbs_ref, g_scr):
    h = pl.program_id(0)
    far = tbl_ref[h, 2 * REL_CLIP]
    diff = (lax.broadcasted_iota(jnp.int32, (BIAS_ROWS, LANES), 0) + BIAS_C0
            - lax.broadcasted_iota(jnp.int32, (BIAS_ROWS, LANES), 1))
    g_scr[...] = jnp.full((BIAS_ROWS, LANES), far, F32)
    for tix in range(1, 2 * REL_CLIP):
        d = BAND_PAD + REL_CLIP - tix
        lo = d - BIAS_C0
        a8 = max(0, lo // 8 * 8)
        b8 = min(BIAS_ROWS, -(-(lo + LANES) // 8) * 8)
        g_scr[a8:b8, :] = jnp.where(diff[a8:b8, :] == d, tbl_ref[h, tix], g_scr[a8:b8, :])
    g = g_scr[...]

    def far_rows(n):
        return jnp.full((n, LANES), far, F32)

    def masked(bias, q_ax, k_ax):
        rf = lax.broadcasted_iota(jnp.int32, bias.shape, q_ax) // CHUNK
        cf = lax.broadcasted_iota(jnp.int32, bias.shape, k_ax) // CHUNK
        visible = (cf >= rf) & (cf <= rf + N_PREV_CHUNKS)
        return jnp.where(visible, bias * LOG2E, NEG_INF)

    blocks = [jnp.concatenate([far_rows(BIAS_C0 + LANES * j), g[0:BAND_W - BIAS_C0 - LANES * j]], axis=0)
              for j in range(BAND_TQ // LANES)]
    mbt_ref[...] = masked(jnp.concatenate(blocks, axis=1), 1, 0)
    near = g[0:BAND_W_S - BIAS_C0].T
    mbs_ref[...] = masked(jnp.concatenate([jnp.full((BAND_TQ_S, BIAS_C0), far, F32), near], axis=1), 0, 1)


def _band_bias(table):
    assert BAND_TQ_S == LANES and BAND_W <= BIAS_C0 + BIAS_ROWS
    return pl.pallas_call(
        _band_bias_kernel,
        out_shape=(jax.ShapeDtypeStruct((N_HEADS, BAND_W, BAND_TQ), F32),
                   jax.ShapeDtypeStruct((N_HEADS, BAND_TQ_S, BAND_W_S), F32)),
        grid=(N_HEADS,),
        in_specs=[pl.BlockSpec(memory_space=pltpu.SMEM)],
        out_specs=(pl.BlockSpec((None, BAND_W, BAND_TQ), lambda h: (h, 0, 0)),
                   pl.BlockSpec((None, BAND_TQ_S, BAND_W_S), lambda h: (h, 0, 0))),
        scratch_shapes=[pltpu.VMEM((BIAS_ROWS, LANES), F32)],
        compiler_params=pltpu.CompilerParams(dimension_semantics=("arbitrary",)),
        name="band_bias",
    )(table.T)


def _band_prompt_kernel(q_ref, k_ref, v_ref, mb_ref, o_ref, qt_scr, kpad, vtpad, s_scr, pa_scr, pb_scr):
    s_len = q_ref.shape[0]
    kpad[0:BAND_PAD, :] = jnp.zeros((BAND_PAD, HEAD_DIM), BF16)
    kpad[BAND_PAD:, :] = k_ref[...]
    vtpad[:, 0:BAND_PAD] = jnp.zeros((HEAD_DIM, BAND_PAD), BF16)

    def setup(ci, carry):
        c0 = pl.multiple_of(ci * BAND_TQ, BAND_TQ)
        qt_scr[:, pl.ds(c0, BAND_TQ)] = _transpose_bf16(q_ref[pl.ds(c0, BAND_TQ), :])
        vtpad[:, pl.ds(pl.multiple_of(BAND_PAD + c0, BAND_TQ), BAND_TQ)] = _transpose_bf16(v_ref[pl.ds(c0, BAND_TQ), :])
        return carry

    n_tiles = s_len // BAND_TQ
    lax.fori_loop(0, n_tiles, setup, 0)
    key_ix = lax.broadcasted_iota(jnp.int32, (BAND_W, BAND_TQ), 0)

    def scores(ti):
        r0 = pl.multiple_of(ti * BAND_TQ, BAND_TQ)
        return _dot(kpad[pl.ds(r0, BAND_W), :], qt_scr[:, pl.ds(r0, BAND_TQ)]) + mb_ref[...]

    def softmax(s, p_ref):
        m = jnp.max(s, axis=0, keepdims=True)
        p = jnp.exp2(s - m)
        p_ref[...] = p.astype(BF16)
        return 1.0 / jnp.sum(p, axis=0, keepdims=True)

    def emit(ti, p_ref, r_den):
        r0 = pl.multiple_of(ti * BAND_TQ, BAND_TQ)
        o_t = _dot(vtpad[:, pl.ds(r0, BAND_W)], p_ref[...]) * r_den
        o_ref[pl.ds(r0, BAND_TQ), :] = o_t.T.astype(BF16)

    assert BAND_PAD // BAND_TQ == 2
    h0 = jnp.where(key_ix >= BAND_PAD, scores(0), NEG_INF)
    h1 = jnp.where(BAND_TQ + key_ix >= BAND_PAD, scores(1), NEG_INF)
    s_scr[...] = scores(2)
    r_0 = softmax(h0, pa_scr)
    emit(0, pa_scr, r_0)
    r_1 = softmax(h1, pb_scr)

    def pair(pi, r_pend):
        a = 2 + 2 * pi
        s_b = scores(a + 1)
        emit(a - 1, pb_scr, r_pend)
        r_a = softmax(s_scr[...], pa_scr)
        s_scr[...] = scores(jnp.minimum(a + 2, n_tiles - 1))
        emit(a, pa_scr, r_a)
        return softmax(s_b, pb_scr)

    r_last = lax.fori_loop(0, (n_tiles - 2) // 2, pair, r_1)
    emit(n_tiles - 1, pb_scr, r_last)


def _band_prompt(zq, mbt, *, n_batch, s_len):
    blk = (s_len, HEAD_DIM)
    off = 3 * N_HEADS
    assert (s_len // BAND_TQ - BAND_PAD // BAND_TQ) % 2 == 0
    scratch = [pltpu.VMEM((HEAD_DIM, s_len), BF16), pltpu.VMEM((BAND_PAD + s_len, HEAD_DIM), BF16),
               pltpu.VMEM((HEAD_DIM, BAND_PAD + s_len), BF16), pltpu.VMEM((BAND_W, BAND_TQ), F32),
               pltpu.VMEM((BAND_W, BAND_TQ), BF16), pltpu.VMEM((BAND_W, BAND_TQ), BF16)]
    resident = 3 * _nbytes((BAND_PAD + s_len, HEAD_DIM), BF16) + 8 * _nbytes((BAND_W, BAND_TQ), F32)
    return pl.pallas_call(
        _band_prompt_kernel,
        out_shape=jax.ShapeDtypeStruct((n_batch * s_len, D_HEADS), BF16),
        grid=(n_batch, N_HEADS),
        in_specs=[pl.BlockSpec(blk, lambda b, h: (b, off + h)),
                  pl.BlockSpec(blk, lambda b, h: (b, off + N_HEADS + h)),
                  pl.BlockSpec(blk, lambda b, h: (b, off + 2 * N_HEADS + h)),
                  pl.BlockSpec((None, BAND_W, BAND_TQ), lambda b, h: (h, 0, 0))],
        out_specs=pl.BlockSpec(blk, lambda b, h: (b, h)),
        scratch_shapes=scratch,
        compiler_params=pltpu.CompilerParams(
            dimension_semantics=("arbitrary", "arbitrary"),
            vmem_limit_bytes=_vmem_limit(4 * _nbytes(blk, BF16) + _nbytes((BAND_W, BAND_TQ), F32), resident)),
        name="band_prompt",
    )(zq, zq, zq, mbt)


def _fox_sample_kernel(q_ref, kn_ref, vn_ref, ck_ref, cv_ref, f_ref, o_ref,
                       m_scr, l_scr, acc_scr, fq_scr, *, past, tk):
    kt = pl.program_id(1)
    t_new = q_ref.shape[0]

    def heads(h):
        return slice(h * HEAD_DIM, (h + 1) * HEAD_DIM)

    @pl.when(kt == 0)
    def _():
        for h in range(N_HEADS):
            fq_scr[h] = _col_bcast(f_ref[h:h + 1, past:past + t_new] * LOG2E, t_new)
        m_scr[...] = jnp.full(m_scr.shape, NEG_INF, F32)
        l_scr[...] = jnp.zeros(l_scr.shape, F32)
        acc_scr[...] = jnp.zeros(acc_scr.shape, F32)

    def update(h, s, v):
        m_old = m_scr[h]
        m_new = jnp.maximum(m_old, jnp.max(s, axis=1, keepdims=True))
        alpha = jnp.exp2(m_old - m_new)
        p = jnp.exp2(s - m_new)
        l_scr[h] = alpha * l_scr[h] + jnp.sum(p, axis=1, keepdims=True)
        acc_scr[h] = alpha * acc_scr[h] + _dot(p.astype(BF16), v)
        m_scr[h] = m_new

    c0 = pl.multiple_of(kt * tk, tk)
    scores = []
    for h in range(N_HEADS):
        k = ck_ref[pl.ds(h, tk, stride=N_HEADS), :].astype(BF16)
        scores.append(_dot_nt(q_ref[:, heads(h)], k) + _tile_lanes(fq_scr[h], tk)
                      - f_ref[h:h + 1, pl.ds(c0, tk)] * LOG2E)
    for h in range(N_HEADS):
        update(h, scores[h], cv_ref[pl.ds(h, tk, stride=N_HEADS), :].astype(BF16))

    @pl.when(kt == pl.num_programs(1) - 1)
    def _():
        r = lax.broadcasted_iota(jnp.int32, (t_new, t_new), 0)
        c = lax.broadcasted_iota(jnp.int32, (t_new, t_new), 1)
        for h in range(N_HEADS):
            q = q_ref[:, heads(h)]
            f_new = f_ref[h:h + 1, past:past + t_new] * LOG2E
            s = _dot_nt(q, kn_ref[:, heads(h)]) + fq_scr[h][:, :t_new] - f_new
            s = jnp.where(c <= r, s, NEG_INF)
            update(h, s, vn_ref[:, heads(h)])
            o_ref[:, heads(h)] = (acc_scr[h] / l_scr[h]).astype(BF16)


def _fox_sample(zq, cache_k, cache_v, f_all, *, n_batch, t_new, past):
    tk = min(2048, past)
    lp = f_all.shape[-1]
    blk = (t_new, D_HEADS)
    cblk = (None, tk * N_HEADS, HEAD_DIM)
    return pl.pallas_call(
        functools.partial(_fox_sample_kernel, past=past, tk=tk),
        out_shape=jax.ShapeDtypeStruct((n_batch * t_new, D_HEADS), BF16),
        grid=(n_batch, past // tk),
        in_specs=[pl.BlockSpec(blk, lambda b, k: (b, 0)),
                  pl.BlockSpec(blk, lambda b, k: (b, 1)),
                  pl.BlockSpec(blk, lambda b, k: (b, 2)),
                  pl.BlockSpec(cblk, lambda b, k: (b, k, 0)),
                  pl.BlockSpec(cblk, lambda b, k: (b, k, 0)),
                  pl.BlockSpec((None, N_HEADS, lp), lambda b, k: (b, 0, 0))],
        out_specs=pl.BlockSpec(blk, lambda b, k: (b, 0)),
        scratch_shapes=[pltpu.VMEM((N_HEADS, t_new, 1), F32), pltpu.VMEM((N_HEADS, t_new, 1), F32),
                        pltpu.VMEM((N_HEADS, t_new, HEAD_DIM), F32),
                        pltpu.VMEM((N_HEADS, t_new, LANES), F32)],
        compiler_params=pltpu.CompilerParams(
            dimension_semantics=("arbitrary", "arbitrary"),
            vmem_limit_bytes=_vmem_limit(2 * _nbytes((tk, D_HEADS), F32) + 4 * _nbytes(blk, BF16)
                                         + _nbytes((N_HEADS, lp), F32), 8 * MIB)),
        name="fox_sample",
    )(zq, zq, zq, cache_k, cache_v, f_all)


def _band_sample_kernel(q_ref, kn_ref, vn_ref, ck_ref, cv_ref, mb_ref, o_ref):
    t_new = q_ref.shape[0]
    lb = ck_ref.shape[0] // N_HEADS
    for h in range(N_HEADS):
        hs = slice(h * HEAD_DIM, (h + 1) * HEAD_DIM)
        q = q_ref[:, hs]
        s1 = _dot_nt(q, ck_ref[pl.ds(h, lb, stride=N_HEADS), :].astype(BF16)) + mb_ref[h, :, 0:lb]
        s2 = _dot_nt(q, kn_ref[:, hs]) + mb_ref[h, :, lb:lb + t_new]
        m = jnp.maximum(jnp.max(s1, axis=1, keepdims=True), jnp.max(s2, axis=1, keepdims=True))
        p1 = jnp.exp2(s1 - m)
        p2 = jnp.exp2(s2 - m)
        den = jnp.sum(p1, axis=1, keepdims=True) + jnp.sum(p2, axis=1, keepdims=True)
        o = _dot(p1.astype(BF16), cv_ref[pl.ds(h, lb, stride=N_HEADS), :].astype(BF16)) + _dot(p2.astype(BF16), vn_ref[:, hs])
        o_ref[:, hs] = (o / den).astype(BF16)


def _band_sample(zq, cache_k, cache_v, mb, *, n_batch, t_new):
    lb = cache_k.shape[1] // N_HEADS
    blk = (t_new, D_HEADS)
    cblk = (None, lb * N_HEADS, HEAD_DIM)
    return pl.pallas_call(
        _band_sample_kernel,
        out_shape=jax.ShapeDtypeStruct((n_batch * t_new, D_HEADS), BF16),
        grid=(n_batch,),
        in_specs=[pl.BlockSpec(blk, lambda b: (b, 3)),
                  pl.BlockSpec(blk, lambda b: (b, 4)),
                  pl.BlockSpec(blk, lambda b: (b, 5)),
                  pl.BlockSpec(cblk, lambda b: (b, 0, 0)),
                  pl.BlockSpec(cblk, lambda b: (b, 0, 0)),
                  pl.BlockSpec((N_HEADS, t_new, BAND_W_S), lambda b: (0, 0, 0))],
        out_specs=pl.BlockSpec(blk, lambda b: (b, 0)),
        compiler_params=pltpu.CompilerParams(
            dimension_semantics=("arbitrary",),
            vmem_limit_bytes=_vmem_limit(2 * _nbytes((lb, D_HEADS), F32) + 4 * _nbytes(blk, BF16),
                                         _nbytes((N_HEADS, t_new, BAND_W_S), F32) + 4 * MIB)),
        name="band_sample",
    )(zq, zq, zq, cache_k, cache_v, mb)


def _merge_kernel(oa_ref, ob_ref, za_ref, zb_ref, x_ref, gt_ref, woa_ref, wob_ref, wout_ref, o_ref, *, nb, t):
    a = _dot(oa_ref[...], woa_ref[...])
    b = _dot(ob_ref[...], wob_ref[...])
    mix = _sigmoid(za_ref[...].astype(F32)) * a + _sigmoid(zb_ref[...].astype(F32)) * b
    y = _dot(mix.astype(BF16), wout_ref[...])
    x = x_ref[...]
    y, gt = _per_batch(y, gt_ref, nb, t)
    x3, _ = _per_batch(x, gt_ref, nb, t)
    o_ref[...] = (x3 + gt * y).reshape(x.shape)


def _merge(oa, ob, zq, x2d, gate, w_oa, w_ob, w_out, *, nb, t, tiles_per_batch):
    m, d = x2d.shape
    tm = nb * t
    za_blk = Z_COLS // d
    if nb == 1:
        mod_map = lambda i: (i // tiles_per_batch, 0, 0)
    else:
        mod_map = lambda i: (i, 0, 0)
    const = lambda i: (0, 0)
    pipelined = (2 * _nbytes((tm, D_HEADS), BF16) + 2 * _nbytes((tm, d), BF16) + 2 * _nbytes((tm, d), F32))
    resident = (2 * (2 * _nbytes((D_HEADS, d), BF16) + _nbytes((d, d), BF16)) + 4 * _nbytes((tm, d), F32))
    return pl.pallas_call(
        functools.partial(_merge_kernel, nb=nb, t=t),
        out_shape=jax.ShapeDtypeStruct((m, d), F32),
        grid=(m // tm,),
        in_specs=[pl.BlockSpec((tm, D_HEADS), lambda i: (i, 0)),
                  pl.BlockSpec((tm, D_HEADS), lambda i: (i, 0)),
                  pl.BlockSpec((tm, d), lambda i: (i, za_blk)),
                  pl.BlockSpec((tm, d), lambda i: (i, za_blk + 1)),
                  pl.BlockSpec((tm, d), lambda i: (i, 0)),
                  pl.BlockSpec((nb, 1, d), mod_map),
                  pl.BlockSpec((D_HEADS, d), const),
                  pl.BlockSpec((D_HEADS, d), const),
                  pl.BlockSpec((d, d), const)],
        out_specs=pl.BlockSpec((tm, d), lambda i: (i, 0)),
        compiler_params=pltpu.CompilerParams(
            dimension_semantics=("arbitrary",),
            vmem_limit_bytes=_vmem_limit(pipelined, resident)),
        name="merge_out",
    )(oa, ob, zq, zq, x2d, gate, w_oa, w_ob, w_out)


def _ffn_kernel(x_ref, sh_ref, sc_ref, gt_ref, g_ref, gf_ref, wg_ref, wu_ref, wd_ref, o_ref,
                h_scr, acc_scr, *, nb, t):
    f = pl.program_id(1)

    @pl.when(f == 0)
    def _():
        h_scr[...] = _mod_norm(x_ref[...], g_ref, sc_ref, sh_ref, nb, t).astype(BF16)
        acc_scr[...] = jnp.zeros(acc_scr.shape, F32)

    h = h_scr[...]
    gate = _dot(h, wg_ref[...])
    up = _dot(h, wu_ref[...])
    act = (gate * _sigmoid(gate) * up).astype(BF16)
    acc_scr[...] += _dot(act, wd_ref[...])

    @pl.when(f == pl.num_programs(1) - 1)
    def _():
        x = x_ref[...]
        y, gt = _per_batch(acc_scr[...], gt_ref, nb, t)
        x3, _ = _per_batch(x, gt_ref, nb, t)
        x2 = (x3 + gt * y).reshape(x.shape)
        o_ref[...] = _rms_scale(x2) * gf_ref[...]


def _ffn(x2d, shift, scale, gate, g_ffn, g_final, w_gate, w_up, w_down, *, nb, t, tiles_per_batch):
    m, d = x2d.shape
    dff = w_gate.shape[1]
    tm = nb * t
    tf = 1408 if (nb > 1 and dff % 1408 == 0) else 512
    if nb == 1:
        mod_map = lambda i, f: (i // tiles_per_batch, 0, 0)
    else:
        mod_map = lambda i, f: (i, 0, 0)
    pipelined = 2 * _nbytes((tm, d), F32) + 3 * _nbytes((d, tf), BF16)
    resident = _nbytes((tm, d), BF16) + _nbytes((tm, d), F32) + 3 * _nbytes((tm, tf), F32) + 2 * _nbytes((tm, d), F32)
    return pl.pallas_call(
        functools.partial(_ffn_kernel, nb=nb, t=t),
        out_shape=jax.ShapeDtypeStruct((m, d), F32),
        grid=(m // tm, dff // tf),
        in_specs=[pl.BlockSpec((tm, d), lambda i, f: (i, 0)),
                  pl.BlockSpec((nb, 1, d), mod_map),
                  pl.BlockSpec((nb, 1, d), mod_map),
                  pl.BlockSpec((nb, 1, d), mod_map),
                  pl.BlockSpec((1, d), lambda i, f: (0, 0)),
                  pl.BlockSpec((1, d), lambda i, f: (0, 0)),
                  pl.BlockSpec((d, tf), lambda i, f: (0, f)),
                  pl.BlockSpec((d, tf), lambda i, f: (0, f)),
                  pl.BlockSpec((tf, d), lambda i, f: (f, 0))],
        out_specs=pl.BlockSpec((tm, d), lambda i, f: (i, 0)),
        scratch_shapes=[pltpu.VMEM((tm, d), BF16), pltpu.VMEM((tm, d), F32)],
        compiler_params=pltpu.CompilerParams(
            dimension_semantics=("arbitrary", "arbitrary"),
            vmem_limit_bytes=_vmem_limit(pipelined, resident)),
        name="ffn_final",
    )(x2d, shift, scale, gate, g_ffn, g_final, w_gate, w_up, w_down)


def _row_tile(s_len, cap):
    tile = min(cap, s_len)
    assert s_len % tile == 0
    return tile


def kernel(x_prompt, x_sample, cache_fox_k, cache_fox_v, cache_fox_logf, cache_band_k, cache_band_v,
           c_prompt, c_sample, w_ada, b_ada, g_mix, w_in, b_f, rel_bias, w_oa, w_ob, w_out,
           g_ffn, w_gate, w_up, w_down, g_final):
    n_b, s_len, d = x_prompt.shape
    n_db, t_new, _ = x_sample.shape
    past = cache_fox_k.shape[2]
    lb = cache_band_k.shape[2]
    assert w_ada.shape[0] == 1, "single-layer trunk"
    assert lb == BAND_PAD and s_len >= BAND_PAD and s_len % FOX_TQ == 0
    assert Z_COLS % d == 0 and t_new % 8 == 0 and t_new + lb <= BAND_W_S and t_new <= CHUNK
    layer = 0
    q_scale = LOG2E / math.sqrt(HEAD_DIM)

    w_l = w_in[layer]
    w_main = _w_prep(w_l)
    w_f = jnp.pad(w_l[:, 3 * D_HEADS:3 * D_HEADS + N_HEADS], ((0, 0), (0, LANES - N_HEADS))).astype(BF16)
    n_main = w_main.shape[1]
    col = jnp.arange(n_main)
    is_q = (col < D_HEADS) | ((col >= 3 * D_HEADS) & (col < 4 * D_HEADS))
    col_scale = jnp.where(is_q, q_scale, 1.0).astype(F32).reshape(1, n_main)
    b_f2 = b_f[layer].reshape(1, N_HEADS)
    g_mix2 = g_mix[layer].reshape(1, d)
    g_ffn2 = g_ffn[layer].reshape(1, d)
    g_fin2 = g_final.reshape(1, d)
    w_oa_b, w_ob_b, w_out_b = w_oa[layer].astype(BF16), w_ob[layer].astype(BF16), w_out[layer].astype(BF16)
    w_gate_b, w_up_b, w_down_b = w_gate[layer].astype(BF16), w_up[layer].astype(BF16), w_down[layer].astype(BF16)

    mods = _ada(jnp.concatenate([c_prompt, c_sample], axis=0), w_ada[layer], b_ada[layer])

    def chunks(rows):
        return [rows[:, k * d:(k + 1) * d][:, None, :] for k in range(6)]

    sh1p, sc1p, gt1p, sh2p, sc2p, gt2p = chunks(mods[:n_b])
    sh1s, sc1s, gt1s, sh2s, sc2s, gt2s = chunks(mods[n_b:])
    mbt, mbs = _band_bias(rel_bias[layer])

    xp = x_prompt.reshape(n_b * s_len, d)
    tm = _row_tile(s_len, 1024)
    zq, ka, va, kb, vb, lf = _mixer_in(xp, sh1p, sc1p, g_mix2, w_main, col_scale, w_f, b_f2,
                                       nb=1, t=tm, tiles_per_batch=s_len // tm, tail=BAND_PAD)
    lf_rows = lf.reshape(n_b, s_len, N_HEADS).transpose(0, 2, 1).reshape(n_b * N_HEADS, s_len)
    f_rows = _cumsum_rows(lf_rows).reshape(n_b * N_HEADS, 1, s_len)
    oa = _fox_prompt(zq, f_rows, n_batch=n_b, s_len=s_len)
    ob = _band_prompt(zq, mbt, n_batch=n_b, s_len=s_len)
    tm = _row_tile(s_len, 256)
    x1 = _merge(oa, ob, zq, xp, gt1p, w_oa_b, w_ob_b, w_out_b, nb=1, t=tm, tiles_per_batch=s_len // tm)
    tm = _row_tile(s_len, 512)
    y_prompt = _ffn(x1, sh2p, sc2p, gt2p, g_ffn2, g_fin2, w_gate_b, w_up_b, w_down_b,
                    nb=1, t=tm, tiles_per_batch=s_len // tm).reshape(n_b, s_len, d)
    fox_shape = (1, n_b, s_len, N_HEADS, HEAD_DIM)
    band_shape = (1, n_b, BAND_PAD, N_HEADS, HEAD_DIM)
    prompt_out = (ka.reshape(fox_shape), va.reshape(fox_shape), lf.reshape(1, n_b, s_len, N_HEADS),
                  kb.reshape(band_shape), vb.reshape(band_shape))

    xs = x_sample.reshape(n_db * t_new, d)
    zqs, kas, vas, kbs, vbs, lfs = _mixer_in(xs, sh1s, sc1s, g_mix2, w_main, col_scale, w_f, b_f2,
                                             nb=n_db, t=t_new, tiles_per_batch=1, tail=n_db * t_new)
    lp = -(-(past + t_new) // 2048) * 2048
    lf_all = jnp.concatenate([cache_fox_logf[layer].transpose(0, 2, 1),
                              lfs.reshape(n_db, t_new, N_HEADS).transpose(0, 2, 1),
                              jnp.zeros((n_db, N_HEADS, lp - past - t_new), F32)], axis=-1)
    f_all = _cumsum_rows(lf_all.reshape(n_db * N_HEADS, lp)).reshape(n_db, N_HEADS, lp)
    oas = _fox_sample(zqs, cache_fox_k[layer].reshape(n_db, past * N_HEADS, HEAD_DIM),
                      cache_fox_v[layer].reshape(n_db, past * N_HEADS, HEAD_DIM), f_all,
                      n_batch=n_db, t_new=t_new, past=past)
    obs = _band_sample(zqs, cache_band_k[layer].reshape(n_db, lb * N_HEADS, HEAD_DIM),
                       cache_band_v[layer].reshape(n_db, lb * N_HEADS, HEAD_DIM), mbs, n_batch=n_db, t_new=t_new)
    x1s = _merge(oas, obs, zqs, xs, gt1s, w_oa_b, w_ob_b, w_out_b, nb=n_db, t=t_new, tiles_per_batch=1)
    y_sample = _ffn(x1s, sh2s, sc2s, gt2s, g_ffn2, g_fin2, w_gate_b, w_up_b, w_down_b,
                    nb=n_db, t=t_new, tiles_per_batch=1).reshape(n_db, t_new, d)
    s_shape = (1, n_db, t_new, N_HEADS, HEAD_DIM)
    sample_out = (kas.reshape(s_shape), vas.reshape(s_shape), lfs.reshape(1, n_db, t_new, N_HEADS),
                  kbs.reshape(s_shape), vbs.reshape(s_shape))

    return (y_prompt, y_sample) + prompt_out + sample_out
```

```python
import functools
import math

import jax
import jax.numpy as jnp
from jax import lax
from jax.experimental import pallas as pl
from jax.experimental.pallas import tpu as pltpu

F32 = jnp.float32
BF16 = jnp.bfloat16

HEAD_DIM = 128
N_HEADS = 8
D_HEADS = N_HEADS * HEAD_DIM
CHUNK = 64
N_PREV_CHUNKS = 8
BAND_PAD = N_PREV_CHUNKS * CHUNK
REL_CLIP = 128
RMS_EPS = 1e-6
NEG_INF = -1e30
LOG2E = math.log2(math.e)
LANES = 128
Z_COLS = 6 * D_HEADS

VMEM_CAP_BYTES = 60 * 1024 * 1024
MIB = 1024 * 1024

BAND_TQ = 256
BAND_W = BAND_PAD + BAND_TQ
BAND_TQ_S = 128
BAND_W_S = BAND_PAD + BAND_TQ_S
FOX_TQ = 512
FOX_TK = 256


def _vmem_limit(pipelined_bytes, resident_bytes=0):
    est = 2 * pipelined_bytes + resident_bytes + 8 * MIB
    return int(min(max(est, 16 * MIB), VMEM_CAP_BYTES))


def _nbytes(shape, dtype):
    return math.prod(shape) * jnp.dtype(dtype).itemsize


def _dot(a, b):
    return jnp.dot(a, b, preferred_element_type=F32)


def _dot_nt(a, b):
    return lax.dot_general(a, b, (((1,), (1,)), ((), ())), preferred_element_type=F32)


def _split3(x):
    hi = x.astype(BF16)
    r1 = x - hi.astype(F32)
    mid = r1.astype(BF16)
    lo = (r1 - mid.astype(F32)).astype(BF16)
    return hi, mid, lo


def _sigmoid(x):
    return 1.0 / (1.0 + jnp.exp(-x))


def _rms_scale(x):
    return x * lax.rsqrt(jnp.mean(x * x, axis=-1, keepdims=True) + RMS_EPS)


def _per_batch(y, ref, nb, t):
    if nb == 1:
        return y, ref[0]
    return y.reshape(nb, t, y.shape[-1]), ref[...]


def _mod_norm(x, g_ref, sc_ref, sh_ref, nb, t):
    y = _rms_scale(x) * g_ref[...]
    y, sc = _per_batch(y, sc_ref, nb, t)
    _, sh = _per_batch(x, sh_ref, nb, t)
    h = y * (1.0 + sc) + sh
    return h.reshape(x.shape)


def _col_bcast(frow, n):
    r = lax.broadcasted_iota(jnp.int32, (n, n), 0)
    c = lax.broadcasted_iota(jnp.int32, (n, n), 1)
    d = jnp.where(r == c, jnp.broadcast_to(frow, (n, n)), 0.0)
    ones = jnp.ones((n, LANES), BF16)
    hi, mid, lo = _split3(d)
    return _dot(hi, ones) + _dot(mid, ones) + _dot(lo, ones)


def _tile_lanes(x, n):
    reps = n // x.shape[1]
    return x if reps == 1 else jnp.concatenate([x] * reps, axis=1)


def _transpose_bf16(x):
    return x.astype(F32).T.astype(BF16)


def _ada_kernel(c_ref, w_ref, b_ref, o_ref):
    c = c_ref[...]
    a = (c * _sigmoid(c)).astype(BF16)
    o_ref[...] = _dot(a, w_ref[...].astype(BF16)) + b_ref[...]


def _ada(c, w, b):
    n, d = c.shape
    nout = w.shape[1]
    tn = 1024 if nout % 1024 == 0 else 512
    return pl.pallas_call(
        _ada_kernel,
        out_shape=jax.ShapeDtypeStruct((n, nout), F32),
        grid=(nout // tn,),
        in_specs=[pl.BlockSpec((n, d), lambda j: (0, 0)),
                  pl.BlockSpec((d, tn), lambda j: (0, j)),
                  pl.BlockSpec((1, tn), lambda j: (0, j))],
        out_specs=pl.BlockSpec((n, tn), lambda j: (0, j)),
        compiler_params=pltpu.CompilerParams(
            dimension_semantics=("arbitrary",),
            vmem_limit_bytes=_vmem_limit(_nbytes((d, tn), F32) + _nbytes((n, tn), F32),
                                         _nbytes((n, d), F32) + _nbytes((d, tn), BF16))),
        name="ada",
    )(c, w, b.reshape(1, nout))


SHIFT = N_HEADS


def _w_prep_kernel(a_ref, b_ref, o_ref, *, tn, first_shifted):
    j = pl.program_id(1)
    a = a_ref[...]
    a_rot = pltpu.roll(a, tn - SHIFT, axis=1)
    b_rot = pltpu.roll(b_ref[...], LANES - SHIFT, axis=1)
    lane = lax.broadcasted_iota(jnp.int32, b_rot.shape, 1)
    tail = jnp.where(lane >= LANES - SHIFT, b_rot, a_rot[:, tn - LANES:])
    shifted = jnp.concatenate([a_rot[:, :tn - LANES], tail], axis=1)
    o_ref[...] = jnp.where(j >= first_shifted, shifted, a).astype(BF16)


def _w_prep(w):
    d, n_in = w.shape
    n = n_in - SHIFT
    tn = 1024 if n % 1024 == 0 else 512
    rt = min(512, d)
    first = 3 * D_HEADS // tn
    assert n % tn == 0 and (3 * D_HEADS) % tn == 0 and d % rt == 0
    return pl.pallas_call(
        functools.partial(_w_prep_kernel, tn=tn, first_shifted=first),
        out_shape=jax.ShapeDtypeStruct((d, n), BF16),
        grid=(d // rt, n // tn),
        in_specs=[pl.BlockSpec((rt, tn), lambda i, j: (i, j)),
                  pl.BlockSpec((rt, LANES), lambda i, j: (i, (j + 1) * (tn // LANES)))],
        out_specs=pl.BlockSpec((rt, tn), lambda i, j: (i, j)),
        compiler_params=pltpu.CompilerParams(dimension_semantics=("arbitrary", "arbitrary")),
        name="w_in_prep",
    )(w, w)


def _mixer_kernel(x_ref, sh_ref, sc_ref, g_ref, w_ref, cs_ref, wf_ref, bf_ref,
                  zq_ref, ka_ref, va_ref, kb_ref, vb_ref, lf_ref, h_scr,
                  *, nb, t, tiles_per_batch, tail, npg):
    i = pl.program_id(0)
    j = pl.program_id(1)

    @pl.when(j == 0)
    def _():
        h = _mod_norm(x_ref[...], g_ref, sc_ref, sh_ref, nb, t).astype(BF16)
        h_scr[...] = h
        fa = _dot(h, wf_ref[...])[:, :N_HEADS] + bf_ref[...]
        lf_ref[...] = jnp.minimum(fa, 0.0) - jnp.log1p(jnp.exp(-jnp.abs(fa)))

    acc = _dot(h_scr[...], w_ref[...])
    zq_ref[...] = (acc * cs_ref[...]).astype(BF16)

    def _group(g):
        return (j >= g * npg) & (j < (g + 1) * npg)

    @pl.when(_group(1))
    def _():
        ka_ref[...] = acc

    @pl.when(_group(2))
    def _():
        va_ref[...] = acc

    is_tail = (i % tiles_per_batch) == (tiles_per_batch - 1)
    rows = acc.shape[0]

    @pl.when(is_tail & _group(4))
    def _():
        kb_ref[...] = acc[rows - tail:, :]

    @pl.when(is_tail & _group(5))
    def _():
        vb_ref[...] = acc[rows - tail:, :]


def _mixer_in(x2d, shift, scale, g, w_main, col_scale, w_f, b_f, *, nb, t, tiles_per_batch, tail):
    m, d = x2d.shape
    n = w_main.shape[1]
    tm = nb * t
    tn = 1024 if n % 1024 == 0 else 512
    npg = D_HEADS // tn
    n_batches = (m // tm) // tiles_per_batch
    if nb == 1:
        mod_map = lambda i, j: (i // tiles_per_batch, 0, 0)
    else:
        mod_map = lambda i, j: (i, 0, 0)

    def grp_map(gidx):
        return lambda i, j: (i, jnp.clip(j - gidx * npg, 0, npg - 1))

    def tail_map(gidx):
        def f(i, j):
            is_tail = (i % tiles_per_batch) == (tiles_per_batch - 1)
            return (i // tiles_per_batch, jnp.where(is_tail, jnp.clip(j - gidx * npg, 0, npg - 1), 0))
        return f

    kern = functools.partial(_mixer_kernel, nb=nb, t=t, tiles_per_batch=tiles_per_batch, tail=tail, npg=npg)
    pipelined = _nbytes((tm, d), F32) + _nbytes((d, tn), BF16) + _nbytes((tm, tn), BF16)
    resident = (_nbytes((tm, d), BF16) + 2 * _nbytes((tm, tn), F32) + 2 * _nbytes((tail, tn), F32)
                + 2 * _nbytes((tm, tn), F32))
    return pl.pallas_call(
        kern,
        out_shape=(jax.ShapeDtypeStruct((m, n), BF16),
                   jax.ShapeDtypeStruct((m, D_HEADS), F32),
                   jax.ShapeDtypeStruct((m, D_HEADS), F32),
                   jax.ShapeDtypeStruct((n_batches * tail, D_HEADS), F32),
                   jax.ShapeDtypeStruct((n_batches * tail, D_HEADS), F32),
                   jax.ShapeDtypeStruct((m, N_HEADS), F32)),
        grid=(m // tm, n // tn),
        in_specs=[pl.BlockSpec((tm, d), lambda i, j: (i, 0)),
                  pl.BlockSpec((nb, 1, d), mod_map),
                  pl.BlockSpec((nb, 1, d), mod_map),
                  pl.BlockSpec((1, d), lambda i, j: (0, 0)),
                  pl.BlockSpec((d, tn), lambda i, j: (0, j)),
                  pl.BlockSpec((1, tn), lambda i, j: (0, j)),
                  pl.BlockSpec((d, LANES), lambda i, j: (0, 0)),
                  pl.BlockSpec((1, N_HEADS), lambda i, j: (0, 0))],
        out_specs=(pl.BlockSpec((tm, tn), lambda i, j: (i, j)),
                   pl.BlockSpec((tm, tn), grp_map(1), pipeline_mode=pl.Buffered(1)),
                   pl.BlockSpec((tm, tn), grp_map(2), pipeline_mode=pl.Buffered(1)),
                   pl.BlockSpec((tail, tn), tail_map(4), pipeline_mode=pl.Buffered(1)),
                   pl.BlockSpec((tail, tn), tail_map(5), pipeline_mode=pl.Buffered(1)),
                   pl.BlockSpec((tm, N_HEADS), lambda i, j: (i, 0))),
        scratch_shapes=[pltpu.VMEM((tm, d), BF16)],
        compiler_params=pltpu.CompilerParams(
            dimension_semantics=("arbitrary", "arbitrary"),
            vmem_limit_bytes=_vmem_limit(pipelined, resident)),
        name="mixer_in",
    )(x2d, shift, scale, g, w_main, col_scale, w_f, b_f)


def _cumsum_kernel(x_ref, o_ref):
    grp, nr, _ = x_ref.shape
    rows = grp * nr
    r = lax.broadcasted_iota(jnp.int32, (LANES, LANES), 0)
    c = lax.broadcasted_iota(jnp.int32, (LANES, LANES), 1)
    upper = (r <= c).astype(BF16)
    rr = lax.broadcasted_iota(jnp.int32, (rows, rows), 0)
    cc = lax.broadcasted_iota(jnp.int32, (rows, rows), 1)

    def group(ix):
        return sum((ix >= gi * nr).astype(jnp.int32) for gi in range(1, grp))

    lower = ((cc < rr) & (group(cc) == group(rr))).astype(BF16)
    hi, mid, lo = _split3(x_ref[...].reshape(rows, LANES))
    within = _dot(hi, upper) + _dot(mid, upper) + _dot(lo, upper)
    tot = jnp.broadcast_to(within[:, LANES - 1:LANES], (rows, LANES))
    hi, mid, lo = _split3(tot)
    before = _dot(lower, hi) + _dot(lower, mid) + _dot(lower, lo)
    o_ref[...] = (within + before).reshape(grp, nr, LANES)


def _cumsum_rows(x):
    rows, length = x.shape
    nr = length // LANES
    grp = 8
    x3 = x.reshape(rows, nr, LANES)
    out = pl.pallas_call(
        _cumsum_kernel,
        out_shape=jax.ShapeDtypeStruct(x3.shape, F32),
        grid=(rows // grp,),
        in_specs=[pl.BlockSpec((grp, nr, LANES), lambda i: (i, 0, 0))],
        out_specs=pl.BlockSpec((grp, nr, LANES), lambda i: (i, 0, 0)),
        compiler_params=pltpu.CompilerParams(dimension_semantics=("arbitrary",)),
        name="cumsum_logf",
    )(x3)
    return out.reshape(rows, length)


def _fox_prompt_kernel(q_ref, k_ref, v_ref, f_ref, o_ref, qxt_scr, kx_scr, vt_scr, acc_scr, sa_scr, sb_scr,
                       pa_scr, pb_scr,
                       *, tq, tk):
    s_len = q_ref.shape[0]
    assert tq == 2 * tk
    sub = lax.broadcasted_iota(jnp.int32, (HEAD_DIM, tk), 0)
    minus_ones = jnp.where(sub < 3, -1.0, 0.0).astype(BF16)

    def setup(ci, carry):
        c0 = pl.multiple_of(ci * tk, tk)
        rows = pl.ds(c0, tk)
        qxt_scr[0:HEAD_DIM, rows] = _transpose_bf16(q_ref[rows, :])
        qxt_scr[HEAD_DIM:, rows] = minus_ones
        vt_scr[:, rows] = _transpose_bf16(v_ref[rows, :])
        hi, mid, lo = [part.astype(F32) for part in _split3(f_ref[:, rows] * LOG2E)]
        ext_t = jnp.where(sub == 0, hi, jnp.where(sub == 1, mid, jnp.where(sub == 2, lo, 0.0)))
        kx_scr[rows, 0:HEAD_DIM] = k_ref[rows, :]
        kx_scr[rows, HEAD_DIM:] = ext_t.T.astype(BF16)
        return carry

    lax.fori_loop(0, s_len // tk, setup, 0)

    key_ix = lax.broadcasted_iota(jnp.int32, (tq, tq), 0)
    qry_ix = lax.broadcasted_iota(jnp.int32, (tq, tq), 1)

    def scores(r0, c0):
        c0 = pl.multiple_of(c0, tq)
        return _dot(kx_scr[pl.ds(c0, tq), :], qxt_scr[:, pl.ds(r0, tq)])

    def softmax(s, f_q, m_old, l_old, p_ref):
        t_max = jnp.max(s, axis=0, keepdims=True) + f_q
        if m_old is None:
            m_new, alpha = t_max, None
        else:
            m_new = jnp.maximum(m_old, t_max)
            alpha = jnp.exp2(m_old - m_new)
        p = jnp.exp2(s - (m_new - f_q))
        p_sum = jnp.sum(p, axis=0, keepdims=True)
        p_ref[...] = p.astype(BF16)
        return m_new, (p_sum if m_old is None else alpha * l_old + p_sum), alpha

    def accumulate(c0, p_ref, alpha):
        c0 = pl.multiple_of(c0, tk)
        pv = _dot(vt_scr[:, pl.ds(c0, tk)], p_ref[...])
        acc_scr[...] = pv if alpha is None else alpha * acc_scr[...] + pv

    def open_block(r0):
        d = jnp.where(key_ix <= qry_ix, scores(r0, r0), NEG_INF)
        sa_scr[...] = scores(r0, 0)
        return d[0:tk], d[tk:]

    def diag_block(r0, f_q, d0, d1):
        m, l, _ = softmax(d0, f_q, None, None, pa_scr)
        accumulate(r0, pa_scr, None)
        m, l, alpha = softmax(d1, f_q, m, l, pb_scr)
        return r0 + tk, alpha, m, l

    def key_block(s_ref, c0, f_q, state):
        c_pend, alpha_pend, m, l = state
        accumulate(c_pend, pb_scr, alpha_pend)
        m, l, alpha_0 = softmax(s_ref[0:tk, :], f_q, m, l, pa_scr)
        accumulate(c0, pa_scr, alpha_0)
        m, l, alpha_1 = softmax(s_ref[tk:, :], f_q, m, l, pb_scr)
        return c0 + tk, alpha_1, m, l

    def close_block(r0, c_pend, alpha_pend, l):
        accumulate(c_pend, pb_scr, alpha_pend)
        o_t = acc_scr[...] * (1.0 / l)
        o_ref[pl.ds(pl.multiple_of(r0, tq), tq), :] = o_t.T.astype(BF16)

    def f_query(r0):
        return f_ref[:, pl.ds(r0, tq)] * LOG2E

    d0, d1 = open_block(0)
    c_pend, alpha_pend, _, l = diag_block(0, f_query(0), d0, d1)

    def q_body(qi, prev):
        r0 = pl.multiple_of(qi * tq, tq)
        f_q = f_query(r0)
        d0, d1 = open_block(r0)
        close_block(r0 - tq, *prev)
        state = diag_block(r0, f_q, d0, d1)

        def pair_body(pi, state):
            c0 = 2 * pi * tq
            sb_scr[...] = scores(r0, c0 + tq)
            state = key_block(sa_scr, c0, f_q, state)
            sa_scr[...] = scores(r0, c0 + 2 * tq)
            return key_block(sb_scr, c0 + tq, f_q, state)

        def last_body(_, state):
            return key_block(sa_scr, (qi - 1) * tq, f_q, state)

        state = lax.fori_loop(0, lax.shift_right_logical(qi, 1), pair_body, state)
        c_pend, alpha_pend, _, l = lax.fori_loop(0, qi & 1, last_body, state)
        return c_pend, alpha_pend, l

    n_q = s_len // tq
    last = lax.fori_loop(1, n_q, q_body, (c_pend, alpha_pend, l))
    close_block((n_q - 1) * tq, *last)


def _fox_prompt(zq, f_rows, *, n_batch, s_len):
    tq = min(FOX_TQ, s_len)
    tk = min(FOX_TK, tq)
    blk = (s_len, HEAD_DIM)
    scratch = [pltpu.VMEM((2 * HEAD_DIM, s_len), BF16), pltpu.VMEM((s_len, 2 * HEAD_DIM), BF16),
               pltpu.VMEM((HEAD_DIM, s_len), BF16), pltpu.VMEM((HEAD_DIM, tq), F32),
               pltpu.VMEM((tq, tq), F32), pltpu.VMEM((tq, tq), F32),
               pltpu.VMEM((tk, tq), BF16), pltpu.VMEM((tk, tq), BF16)]
    resident = 5 * _nbytes(blk, BF16) + 8 * _nbytes((tk, tq), F32)
    return pl.pallas_call(
        functools.partial(_fox_prompt_kernel, tq=tq, tk=tk),
        out_shape=jax.ShapeDtypeStruct((n_batch * s_len, D_HEADS), BF16),
        grid=(n_batch, N_HEADS),
        in_specs=[pl.BlockSpec(blk, lambda b, h: (b, h)),
                  pl.BlockSpec(blk, lambda b, h: (b, N_HEADS + h)),
                  pl.BlockSpec(blk, lambda b, h: (b, 2 * N_HEADS + h)),
                  pl.BlockSpec((None, 1, s_len), lambda b, h: (b * N_HEADS + h, 0, 0))],
        out_specs=pl.BlockSpec(blk, lambda b, h: (b, h)),
        scratch_shapes=scratch,
        compiler_params=pltpu.CompilerParams(
            dimension_semantics=("arbitrary", "arbitrary"),
            vmem_limit_bytes=_vmem_limit(4 * _nbytes(blk, BF16) + _nbytes((8, s_len), F32), resident)),
        name="fox_prompt",
    )(zq, zq, zq, f_rows)


BIAS_ROWS = 4 * REL_CLIP
BIAS_C0 = BAND_PAD - 2 * REL_CLIP


def _band_bias_kernel(tbl_ref, mbt_ref, mbs_ref, g_scr):
    h = pl.program_id(0)
    far = tbl_ref[h, 2 * REL_CLIP]
    diff = (lax.broadcasted_iota(jnp.int32, (BIAS_ROWS, LANES), 0) + BIAS_C0
            - lax.broadcasted_iota(jnp.int32, (BIAS_ROWS, LANES), 1))
    g_scr[...] = jnp.full((BIAS_ROWS, LANES), far, F32)
    for tix in range(1, 2 * REL_CLIP):
        d = BAND_PAD + REL_CLIP - tix
        lo = d - BIAS_C0
        a8 = max(0, lo // 8 * 8)
        b8 = min(BIAS_ROWS, -(-(lo + LANES) // 8) * 8)
        g_scr[a8:b8, :] = jnp.where(diff[a8:b8, :] == d, tbl_ref[h, tix], g_scr[a8:b8, :])
    g = g_scr[...]

    def far_rows(n):
        return jnp.full((n, LANES), far, F32)

    def masked(bias, q_ax, k_ax):
        rf = lax.broadcasted_iota(jnp.int32, bias.shape, q_ax) // CHUNK
        cf = lax.broadcasted_iota(jnp.int32, bias.shape, k_ax) // CHUNK
        visible = (cf >= rf) & (cf <= rf + N_PREV_CHUNKS)
        return jnp.where(visible, bias * LOG2E, NEG_INF)

    blocks = [jnp.concatenate([far_rows(BIAS_C0 + LANES * j), g[0:BAND_W - BIAS_C0 - LANES * j]], axis=0)
              for j in range(BAND_TQ // LANES)]
    mbt_ref[...] = masked(jnp.concatenate(blocks, axis=1), 1, 0)
    near = g[0:BAND_W_S - BIAS_C0].T
    mbs_ref[...] = masked(jnp.concatenate([jnp.full((BAND_TQ_S, BIAS_C0), far, F32), near], axis=1), 0, 1)


def _band_bias(table):
    assert BAND_TQ_S == LANES and BAND_W <= BIAS_C0 + BIAS_ROWS
    return pl.pallas_call(
        _band_bias_kernel,
        out_shape=(jax.ShapeDtypeStruct((N_HEADS, BAND_W, BAND_TQ), F32),
                   jax.ShapeDtypeStruct((N_HEADS, BAND_TQ_S, BAND_W_S), F32)),
        grid=(N_HEADS,),
        in_specs=[pl.BlockSpec(memory_space=pltpu.SMEM)],
        out_specs=(pl.BlockSpec((None, BAND_W, BAND_TQ), lambda h: (h, 0, 0)),
                   pl.BlockSpec((None, BAND_TQ_S, BAND_W_S), lambda h: (h, 0, 0))),
        scratch_shapes=[pltpu.VMEM((BIAS_ROWS, LANES), F32)],
        compiler_params=pltpu.CompilerParams(dimension_semantics=("arbitrary",)),
        name="band_bias",
    )(table.T)


def _band_prompt_kernel(q_ref, k_ref, v_ref, mb_ref, o_ref, qt_scr, kpad, vtpad, s_scr, pa_scr, pb_scr):
    s_len = q_ref.shape[0]
    kpad[0:BAND_PAD, :] = jnp.zeros((BAND_PAD, HEAD_DIM), BF16)
    kpad[BAND_PAD:, :] = k_ref[...]
    vtpad[:, 0:BAND_PAD] = jnp.zeros((HEAD_DIM, BAND_PAD), BF16)

    def setup(ci, carry):
        c0 = pl.multiple_of(ci * BAND_TQ, BAND_TQ)
        qt_scr[:, pl.ds(c0, BAND_TQ)] = _transpose_bf16(q_ref[pl.ds(c0, BAND_TQ), :])
        vtpad[:, pl.ds(pl.multiple_of(BAND_PAD + c0, BAND_TQ), BAND_TQ)] = _transpose_bf16(v_ref[pl.ds(c0, BAND_TQ), :])
        return carry

    n_tiles = s_len // BAND_TQ
    lax.fori_loop(0, n_tiles, setup, 0)
    key_ix = lax.broadcasted_iota(jnp.int32, (BAND_W, BAND_TQ), 0)

    def scores(ti):
        r0 = pl.multiple_of(ti * BAND_TQ, BAND_TQ)
        return _dot(kpad[pl.ds(r0, BAND_W), :], qt_scr[:, pl.ds(r0, BAND_TQ)]) + mb_ref[...]

    def softmax(s, p_ref):
        m = jnp.max(s, axis=0, keepdims=True)
        p = jnp.exp2(s - m)
        p_ref[...] = p.astype(BF16)
        return 1.0 / jnp.sum(p, axis=0, keepdims=True)

    def emit(ti, p_ref, r_den):
        r0 = pl.multiple_of(ti * BAND_TQ, BAND_TQ)
        o_t = _dot(vtpad[:, pl.ds(r0, BAND_W)], p_ref[...]) * r_den
        o_ref[pl.ds(r0, BAND_TQ), :] = o_t.T.astype(BF16)

    assert BAND_PAD // BAND_TQ == 2
    h0 = jnp.where(key_ix >= BAND_PAD, scores(0), NEG_INF)
    h1 = jnp.where(BAND_TQ + key_ix >= BAND_PAD, scores(1), NEG_INF)
    s_scr[...] = scores(2)
    r_0 = softmax(h0, pa_scr)
    emit(0, pa_scr, r_0)
    r_1 = softmax(h1, pb_scr)

    def pair(pi, r_pend):
        a = 2 + 2 * pi
        s_b = scores(a + 1)
        emit(a - 1, pb_scr, r_pend)
        r_a = softmax(s_scr[...], pa_scr)
        s_scr[...] = scores(jnp.minimum(a + 2, n_tiles - 1))
        emit(a, pa_scr, r_a)
        return softmax(s_b, pb_scr)

    r_last = lax.fori_loop(0, (n_tiles - 2) // 2, pair, r_1)
    emit(n_tiles - 1, pb_scr, r_last)


def _band_prompt(zq, mbt, *, n_batch, s_len):
    blk = (s_len, HEAD_DIM)
    off = 3 * N_HEADS
    assert (s_len // BAND_TQ - BAND_PAD // BAND_TQ) % 2 == 0
    scratch = [pltpu.VMEM((HEAD_DIM, s_len), BF16), pltpu.VMEM((BAND_PAD + s_len, HEAD_DIM), BF16),
               pltpu.VMEM((HEAD_DIM, BAND_PAD + s_len), BF16), pltpu.VMEM((BAND_W, BAND_TQ), F32),
               pltpu.VMEM((BAND_W, BAND_TQ), BF16), pltpu.VMEM((BAND_W, BAND_TQ), BF16)]
    resident = 3 * _nbytes((BAND_PAD + s_len, HEAD_DIM), BF16) + 8 * _nbytes((BAND_W, BAND_TQ), F32)
    return pl.pallas_call(
        _band_prompt_kernel,
        out_shape=jax.ShapeDtypeStruct((n_batch * s_len, D_HEADS), BF16),
        grid=(n_batch, N_HEADS),
        in_specs=[pl.BlockSpec(blk, lambda b, h: (b, off + h)),
                  pl.BlockSpec(blk, lambda b, h: (b, off + N_HEADS + h)),
                  pl.BlockSpec(blk, lambda b, h: (b, off + 2 * N_HEADS + h)),
                  pl.BlockSpec((None, BAND_W, BAND_TQ), lambda b, h: (h, 0, 0))],
        out_specs=pl.BlockSpec(blk, lambda b, h: (b, h)),
        scratch_shapes=scratch,
        compiler_params=pltpu.CompilerParams(
            dimension_semantics=("arbitrary", "arbitrary"),
            vmem_limit_bytes=_vmem_limit(4 * _nbytes(blk, BF16) + _nbytes((BAND_W, BAND_TQ), F32), resident)),
        name="band_prompt",
    )(zq, zq, zq, mbt)


def _fox_sample_kernel(q_ref, kn_ref, vn_ref, ck_ref, cv_ref, f_ref, o_ref,
                       m_scr, l_scr, acc_scr, fq_scr, *, past, tk):
    kt = pl.program_id(1)
    t_new = q_ref.shape[0]

    def heads(h):
        return slice(h * HEAD_DIM, (h + 1) * HEAD_DIM)

    @pl.when(kt == 0)
    def _():
        for h in range(N_HEADS):
            fq_scr[h] = _col_bcast(f_ref[h:h + 1, past:past + t_new] * LOG2E, t_new)
        m_scr[...] = jnp.full(m_scr.shape, NEG_INF, F32)
        l_scr[...] = jnp.zeros(l_scr.shape, F32)
        acc_scr[...] = jnp.zeros(acc_scr.shape, F32)

    def update(h, s, v):
        m_old = m_scr[h]
        m_new = jnp.maximum(m_old, jnp.max(s, axis=1, keepdims=True))
        alpha = jnp.exp2(m_old - m_new)
        p = jnp.exp2(s - m_new)
        l_scr[h] = alpha * l_scr[h] + jnp.sum(p, axis=1, keepdims=True)
        acc_scr[h] = alpha * acc_scr[h] + _dot(p.astype(BF16), v)
        m_scr[h] = m_new

    c0 = pl.multiple_of(kt * tk, tk)
    scores = []
    for h in range(N_HEADS):
        k = ck_ref[pl.ds(h, tk, stride=N_HEADS), :].astype(BF16)
        scores.append(_dot_nt(q_ref[:, heads(h)], k) + _tile_lanes(fq_scr[h], tk)
                      - f_ref[h:h + 1, pl.ds(c0, tk)] * LOG2E)
    for h in range(N_HEADS):
        update(h, scores[h], cv_ref[pl.ds(h, tk, stride=N_HEADS), :].astype(BF16))

    @pl.when(kt == pl.num_programs(1) - 1)
    def _():
        r = lax.broadcasted_iota(jnp.int32, (t_new, t_new), 0)
        c = lax.broadcasted_iota(jnp.int32, (t_new, t_new), 1)
        for h in range(N_HEADS):
            q = q_ref[:, heads(h)]
            f_new = f_ref[h:h + 1, past:past + t_new] * LOG2E
            s = _dot_nt(q, kn_ref[:, heads(h)]) + fq_scr[h][:, :t_new] - f_new
            s = jnp.where(c <= r, s, NEG_INF)
            update(h, s, vn_ref[:, heads(h)])
            o_ref[:, heads(h)] = (acc_scr[h] / l_scr[h]).astype(BF16)


def _fox_sample(zq, cache_k, cache_v, f_all, *, n_batch, t_new, past):
    tk = min(2048, past)
    lp = f_all.shape[-1]
    blk = (t_new, D_HEADS)
    cblk = (None, tk * N_HEADS, HEAD_DIM)
    return pl.pallas_call(
        functools.partial(_fox_sample_kernel, past=past, tk=tk),
        out_shape=jax.ShapeDtypeStruct((n_batch * t_new, D_HEADS), BF16),
        grid=(n_batch, past // tk),
        in_specs=[pl.BlockSpec(blk, lambda b, k: (b, 0)),
                  pl.BlockSpec(blk, lambda b, k: (b, 1)),
                  pl.BlockSpec(blk, lambda b, k: (b, 2)),
                  pl.BlockSpec(cblk, lambda b, k: (b, k, 0)),
                  pl.BlockSpec(cblk, lambda b, k: (b, k, 0)),
                  pl.BlockSpec((None, N_HEADS, lp), lambda b, k: (b, 0, 0))],
        out_specs=pl.BlockSpec(blk, lambda b, k: (b, 0)),
        scratch_shapes=[pltpu.VMEM((N_HEADS, t_new, 1), F32), pltpu.VMEM((N_HEADS, t_new, 1), F32),
                        pltpu.VMEM((N_HEADS, t_new, HEAD_DIM), F32),
                        pltpu.VMEM((N_HEADS, t_new, LANES), F32)],
        compiler_params=pltpu.CompilerParams(
            dimension_semantics=("arbitrary", "arbitrary"),
            vmem_limit_bytes=_vmem_limit(2 * _nbytes((tk, D_HEADS), F32) + 4 * _nbytes(blk, BF16)
                                         + _nbytes((N_HEADS, lp), F32), 8 * MIB)),
        name="fox_sample",
    )(zq, zq, zq, cache_k, cache_v, f_all)


def _band_sample_kernel(q_ref, kn_ref, vn_ref, ck_ref, cv_ref, mb_ref, o_ref):
    t_new = q_ref.shape[0]
    lb = ck_ref.shape[0] // N_HEADS
    for h in range(N_HEADS):
        hs = slice(h * HEAD_DIM, (h + 1) * HEAD_DIM)
        q = q_ref[:, hs]
        s1 = _dot_nt(q, ck_ref[pl.ds(h, lb, stride=N_HEADS), :].astype(BF16)) + mb_ref[h, :, 0:lb]
        s2 = _dot_nt(q, kn_ref[:, hs]) + mb_ref[h, :, lb:lb + t_new]
        m = jnp.maximum(jnp.max(s1, axis=1, keepdims=True), jnp.max(s2, axis=1, keepdims=True))
        p1 = jnp.exp2(s1 - m)
        p2 = jnp.exp2(s2 - m)
        den = jnp.sum(p1, axis=1, keepdims=True) + jnp.sum(p2, axis=1, keepdims=True)
        o = _dot(p1.astype(BF16), cv_ref[pl.ds(h, lb, stride=N_HEADS), :].astype(BF16)) + _dot(p2.astype(BF16), vn_ref[:, hs])
        o_ref[:, hs] = (o / den).astype(BF16)


def _band_sample(zq, cache_k, cache_v, mb, *, n_batch, t_new):
    lb = cache_k.shape[1] // N_HEADS
    blk = (t_new, D_HEADS)
    cblk = (None, lb * N_HEADS, HEAD_DIM)
    return pl.pallas_call(
        _band_sample_kernel,
        out_shape=jax.ShapeDtypeStruct((n_batch * t_new, D_HEADS), BF16),
        grid=(n_batch,),
        in_specs=[pl.BlockSpec(blk, lambda b: (b, 3)),
                  pl.BlockSpec(blk, lambda b: (b, 4)),
                  pl.BlockSpec(blk, lambda b: (b, 5)),
                  pl.BlockSpec(cblk, lambda b: (b, 0, 0)),
                  pl.BlockSpec(cblk, lambda b: (b, 0, 0)),
                  pl.BlockSpec((N_HEADS, t_new, BAND_W_S), lambda b: (0, 0, 0))],
        out_specs=pl.BlockSpec(blk, lambda b: (b, 0)),
        compiler_params=pltpu.CompilerParams(
            dimension_semantics=("arbitrary",),
            vmem_limit_bytes=_vmem_limit(2 * _nbytes((lb, D_HEADS), F32) + 4 * _nbytes(blk, BF16),
                                         _nbytes((N_HEADS, t_new, BAND_W_S), F32) + 4 * MIB)),
        name="band_sample",
    )(zq, zq, zq, cache_k, cache_v, mb)


def _merge_kernel(oa_ref, ob_ref, za_ref, zb_ref, x_ref, gt_ref, woa_ref, wob_ref, wout_ref, o_ref, mix_scr, *, nb, t):
    d = x_ref.shape[1]
    half = d // 2
    for c in range(2):
        cols = slice(c * half, (c + 1) * half)
        a = _dot(oa_ref[...], woa_ref[:, cols])
        b = _dot(ob_ref[...], wob_ref[:, cols])
        mix = _sigmoid(za_ref[:, cols].astype(F32)) * a + _sigmoid(zb_ref[:, cols].astype(F32)) * b
        mix_scr[:, cols] = mix.astype(BF16)
    for c in range(2):
        cols = slice(c * half, (c + 1) * half)
        y = _dot(mix_scr[...], wout_ref[:, cols])
        x = x_ref[:, cols]
        if nb == 1:
            o_ref[:, cols] = x + gt_ref[0][:, cols] * y
        else:
            gt = gt_ref[...][:, :, cols]
            o_ref[:, cols] = (x.reshape(nb, t, half) + gt * y.reshape(nb, t, half)).reshape(x.shape)


def _merge(oa, ob, zq, x2d, gate, w_oa, w_ob, w_out, *, nb, t, tiles_per_batch):
    m, d = x2d.shape
    tm = nb * t
    za_blk = Z_COLS // d
    if nb == 1:
        mod_map = lambda i: (i // tiles_per_batch, 0, 0)
    else:
        mod_map = lambda i: (i, 0, 0)
    const = lambda i: (0, 0)
    pipelined = (2 * _nbytes((tm, D_HEADS), BF16) + 2 * _nbytes((tm, d), BF16) + 2 * _nbytes((tm, d), F32))
    resident = (2 * _nbytes((D_HEADS, d), BF16) + _nbytes((d, d), BF16) + _nbytes((tm, d), BF16)
                + 2 * _nbytes((tm, d), F32))
    return pl.pallas_call(
        functools.partial(_merge_kernel, nb=nb, t=t),
        out_shape=jax.ShapeDtypeStruct((m, d), F32),
        grid=(m // tm,),
        in_specs=[pl.BlockSpec((tm, D_HEADS), lambda i: (i, 0)),
                  pl.BlockSpec((tm, D_HEADS), lambda i: (i, 0)),
                  pl.BlockSpec((tm, d), lambda i: (i, za_blk)),
                  pl.BlockSpec((tm, d), lambda i: (i, za_blk + 1)),
                  pl.BlockSpec((tm, d), lambda i: (i, 0)),
                  pl.BlockSpec((nb, 1, d), mod_map),
                  pl.BlockSpec((D_HEADS, d), const, pipeline_mode=pl.Buffered(1)),
                  pl.BlockSpec((D_HEADS, d), const, pipeline_mode=pl.Buffered(1)),
                  pl.BlockSpec((d, d), const, pipeline_mode=pl.Buffered(1))],
        out_specs=pl.BlockSpec((tm, d), lambda i: (i, 0)),
        scratch_shapes=[pltpu.VMEM((tm, d), BF16)],
        compiler_params=pltpu.CompilerParams(
            dimension_semantics=("arbitrary",),
            vmem_limit_bytes=_vmem_limit(pipelined, resident)),
        name="merge_out",
    )(oa, ob, zq, zq, x2d, gate, w_oa, w_ob, w_out)


def _ffn_kernel(x_ref, sh_ref, sc_ref, gt_ref, g_ref, gf_ref, wg_ref, wu_ref, wd_ref, o_ref,
                h_scr, acc_scr, *, nb, t):
    f = pl.program_id(1)

    @pl.when(f == 0)
    def _():
        h_scr[...] = _mod_norm(x_ref[...], g_ref, sc_ref, sh_ref, nb, t).astype(BF16)
        acc_scr[...] = jnp.zeros(acc_scr.shape, F32)

    h = h_scr[...]
    gate = _dot(h, wg_ref[...])
    up = _dot(h, wu_ref[...])
    act = (gate * _sigmoid(gate) * up).astype(BF16)
    acc_scr[...] += _dot(act, wd_ref[...])

    @pl.when(f == pl.num_programs(1) - 1)
    def _():
        x = x_ref[...]
        y, gt = _per_batch(acc_scr[...], gt_ref, nb, t)
        x3, _ = _per_batch(x, gt_ref, nb, t)
        x2 = (x3 + gt * y).reshape(x.shape)
        o_ref[...] = _rms_scale(x2) * gf_ref[...]


def _ffn(x2d, shift, scale, gate, g_ffn, g_final, w_gate, w_up, w_down, *, nb, t, tiles_per_batch):
    m, d = x2d.shape
    dff = w_gate.shape[1]
    tm = nb * t
    tf = 512
    if nb == 1:
        mod_map = lambda i, f: (i // tiles_per_batch, 0, 0)
    else:
        mod_map = lambda i, f: (i, 0, 0)
    pipelined = 2 * _nbytes((tm, d), F32) + 3 * _nbytes((d, tf), BF16)
    resident = _nbytes((tm, d), BF16) + _nbytes((tm, d), F32) + 3 * _nbytes((tm, tf), F32) + 2 * _nbytes((tm, d), F32)
    return pl.pallas_call(
        functools.partial(_ffn_kernel, nb=nb, t=t),
        out_shape=jax.ShapeDtypeStruct((m, d), F32),
        grid=(m // tm, dff // tf),
        in_specs=[pl.BlockSpec((tm, d), lambda i, f: (i, 0)),
                  pl.BlockSpec((nb, 1, d), mod_map),
                  pl.BlockSpec((nb, 1, d), mod_map),
                  pl.BlockSpec((nb, 1, d), mod_map),
                  pl.BlockSpec((1, d), lambda i, f: (0, 0)),
                  pl.BlockSpec((1, d), lambda i, f: (0, 0)),
                  pl.BlockSpec((d, tf), lambda i, f: (0, f)),
                  pl.BlockSpec((d, tf), lambda i, f: (0, f)),
                  pl.BlockSpec((tf, d), lambda i, f: (f, 0))],
        out_specs=pl.BlockSpec((tm, d), lambda i, f: (i, 0)),
        scratch_shapes=[pltpu.VMEM((tm, d), BF16), pltpu.VMEM((tm, d), F32)],
        compiler_params=pltpu.CompilerParams(
            dimension_semantics=("arbitrary", "arbitrary"),
            vmem_limit_bytes=_vmem_limit(pipelined, resident)),
        name="ffn_final",
    )(x2d, shift, scale, gate, g_ffn, g_final, w_gate, w_up, w_down)


def _row_tile(s_len, cap):
    tile = min(cap, s_len)
    assert s_len % tile == 0
    return tile


def kernel(x_prompt, x_sample, cache_fox_k, cache_fox_v, cache_fox_logf, cache_band_k, cache_band_v,
           c_prompt, c_sample, w_ada, b_ada, g_mix, w_in, b_f, rel_bias, w_oa, w_ob, w_out,
           g_ffn, w_gate, w_up, w_down, g_final):
    n_b, s_len, d = x_prompt.shape
    n_db, t_new, _ = x_sample.shape
    past = cache_fox_k.shape[2]
    lb = cache_band_k.shape[2]
    assert w_ada.shape[0] == 1, "single-layer trunk"
    assert lb == BAND_PAD and s_len >= BAND_PAD and s_len % FOX_TQ == 0
    assert Z_COLS % d == 0 and t_new % 8 == 0 and t_new + lb <= BAND_W_S and t_new <= CHUNK
    layer = 0
    q_scale = LOG2E / math.sqrt(HEAD_DIM)

    w_l = w_in[layer]
    w_main = _w_prep(w_l)
    w_f = jnp.pad(w_l[:, 3 * D_HEADS:3 * D_HEADS + N_HEADS], ((0, 0), (0, LANES - N_HEADS))).astype(BF16)
    n_main = w_main.shape[1]
    col = jnp.arange(n_main)
    is_q = (col < D_HEADS) | ((col >= 3 * D_HEADS) & (col < 4 * D_HEADS))
    col_scale = jnp.where(is_q, q_scale, 1.0).astype(F32).reshape(1, n_main)
    b_f2 = b_f[layer].reshape(1, N_HEADS)
    g_mix2 = g_mix[layer].reshape(1, d)
    g_ffn2 = g_ffn[layer].reshape(1, d)
    g_fin2 = g_final.reshape(1, d)
    w_oa_b, w_ob_b, w_out_b = w_oa[layer].astype(BF16), w_ob[layer].astype(BF16), w_out[layer].astype(BF16)
    w_gate_b, w_up_b, w_down_b = w_gate[layer].astype(BF16), w_up[layer].astype(BF16), w_down[layer].astype(BF16)

    mods = _ada(jnp.concatenate([c_prompt, c_sample], axis=0), w_ada[layer], b_ada[layer])

    def chunks(rows):
        return [rows[:, k * d:(k + 1) * d][:, None, :] for k in range(6)]

    sh1p, sc1p, gt1p, sh2p, sc2p, gt2p = chunks(mods[:n_b])
    sh1s, sc1s, gt1s, sh2s, sc2s, gt2s = chunks(mods[n_b:])
    mbt, mbs = _band_bias(rel_bias[layer])

    xp = x_prompt.reshape(n_b * s_len, d)
    tm = _row_tile(s_len, 1024)
    zq, ka, va, kb, vb, lf = _mixer_in(xp, sh1p, sc1p, g_mix2, w_main, col_scale, w_f, b_f2,
                                       nb=1, t=tm, tiles_per_batch=s_len // tm, tail=BAND_PAD)
    lf_rows = lf.reshape(n_b, s_len, N_HEADS).transpose(0, 2, 1).reshape(n_b * N_HEADS, s_len)
    f_rows = _cumsum_rows(lf_rows).reshape(n_b * N_HEADS, 1, s_len)
    oa = _fox_prompt(zq, f_rows, n_batch=n_b, s_len=s_len)
    ob = _band_prompt(zq, mbt, n_batch=n_b, s_len=s_len)
    tm = _row_tile(s_len, 512)
    x1 = _merge(oa, ob, zq, xp, gt1p, w_oa_b, w_ob_b, w_out_b, nb=1, t=tm, tiles_per_batch=s_len // tm)
    tm = _row_tile(s_len, 512)
    y_prompt = _ffn(x1, sh2p, sc2p, gt2p, g_ffn2, g_fin2, w_gate_b, w_up_b, w_down_b,
                    nb=1, t=tm, tiles_per_batch=s_len // tm).reshape(n_b, s_len, d)
    fox_shape = (1, n_b, s_len, N_HEADS, HEAD_DIM)
    band_shape = (1, n_b, BAND_PAD, N_HEADS, HEAD_DIM)
    prompt_out = (ka.reshape(fox_shape), va.reshape(fox_shape), lf.reshape(1, n_b, s_len, N_HEADS),
                  kb.reshape(band_shape), vb.reshape(band_shape))

    xs = x_sample.reshape(n_db * t_new, d)
    zqs, kas, vas, kbs, vbs, lfs = _mixer_in(xs, sh1s, sc1s, g_mix2, w_main, col_scale, w_f, b_f2,
                                             nb=n_db, t=t_new, tiles_per_batch=1, tail=n_db * t_new)
    lp = -(-(past + t_new) // 2048) * 2048
    lf_all = jnp.concatenate([cache_fox_logf[layer].transpose(0, 2, 1),
                              lfs.reshape(n_db, t_new, N_HEADS).transpose(0, 2, 1),
                              jnp.zeros((n_db, N_HEADS, lp - past - t_new), F32)], axis=-1)
    f_all = _cumsum_rows(lf_all.reshape(n_db * N_HEADS, lp)).reshape(n_db, N_HEADS, lp)
    oas = _fox_sample(zqs, cache_fox_k[layer].reshape(n_db, past * N_HEADS, HEAD_DIM),
                      cache_fox_v[layer].reshape(n_db, past * N_HEADS, HEAD_DIM), f_all,
                      n_batch=n_db, t_new=t_new, past=past)
    obs = _band_sample(zqs, cache_band_k[layer].reshape(n_db, lb * N_HEADS, HEAD_DIM),
                       cache_band_v[layer].reshape(n_db, lb * N_HEADS, HEAD_DIM), mbs, n_batch=n_db, t_new=t_new)
    x1s = _merge(oas, obs, zqs, xs, gt1s, w_oa_b, w_ob_b, w_out_b, nb=n_db, t=t_new, tiles_per_batch=1)
    y_sample = _ffn(x1s, sh2s, sc2s, gt2s, g_ffn2, g_fin2, w_gate_b, w_up_b, w_down_b,
                    nb=n_db, t=t_new, tiles_per_batch=1).reshape(n_db, t_new, d)
    s_shape = (1, n_db, t_new, N_HEADS, HEAD_DIM)
    sample_out = (kas.reshape(s_shape), vas.reshape(s_shape), lfs.reshape(1, n_db, t_new, N_HEADS),
                  kbs.reshape(s_shape), vbs.reshape(s_shape))

    return (y_prompt, y_sample) + prompt_out + sample_out
```

```python
import functools
import math

import jax
import jax.numpy as jnp
from jax import lax
from jax.experimental import pallas as pl
from jax.experimental.pallas import tpu as pltpu

F32 = jnp.float32
BF16 = jnp.bfloat16

HEAD_DIM = 128
N_HEADS = 8
D_HEADS = N_HEADS * HEAD_DIM
CHUNK = 64
N_PREV_CHUNKS = 8
BAND_PAD = N_PREV_CHUNKS * CHUNK
REL_CLIP = 128
RMS_EPS = 1e-6
NEG_INF = -1e30
LOG2E = math.log2(math.e)
LANES = 128
Z_COLS = 6 * D_HEADS

VMEM_CAP_BYTES = 60 * 1024 * 1024
MIB = 1024 * 1024

BAND_TQ = 256
BAND_W = BAND_PAD + BAND_TQ
BAND_TQ_S = 128
BAND_W_S = BAND_PAD + BAND_TQ_S
FOX_TQ = 512
FOX_TK = 256


def _vmem_limit(pipelined_bytes, resident_bytes=0):
    est = 2 * pipelined_bytes + resident_bytes + 8 * MIB
    return int(min(max(est, 16 * MIB), VMEM_CAP_BYTES))


def _nbytes(shape, dtype):
    return math.prod(shape) * jnp.dtype(dtype).itemsize


def _dot(a, b):
    return jnp.dot(a, b, preferred_element_type=F32)


def _dot_nt(a, b):
    return lax.dot_general(a, b, (((1,), (1,)), ((), ())), preferred_element_type=F32)


def _split3(x):
    hi = x.astype(BF16)
    r1 = x - hi.astype(F32)
    mid = r1.astype(BF16)
    lo = (r1 - mid.astype(F32)).astype(BF16)
    return hi, mid, lo


def _sigmoid(x):
    return 1.0 / (1.0 + jnp.exp(-x))


def _rms_scale(x):
    return x * lax.rsqrt(jnp.mean(x * x, axis=-1, keepdims=True) + RMS_EPS)


def _per_batch(y, ref, nb, t):
    if nb == 1:
        return y, ref[0]
    return y.reshape(nb, t, y.shape[-1]), ref[...]


def _mod_norm(x, g_ref, sc_ref, sh_ref, nb, t):
    y = _rms_scale(x) * g_ref[...]
    y, sc = _per_batch(y, sc_ref, nb, t)
    _, sh = _per_batch(x, sh_ref, nb, t)
    h = y * (1.0 + sc) + sh
    return h.reshape(x.shape)


def _col_bcast(frow, n):
    r = lax.broadcasted_iota(jnp.int32, (n, n), 0)
    c = lax.broadcasted_iota(jnp.int32, (n, n), 1)
    d = jnp.where(r == c, jnp.broadcast_to(frow, (n, n)), 0.0)
    ones = jnp.ones((n, LANES), BF16)
    hi, mid, lo = _split3(d)
    return _dot(hi, ones) + _dot(mid, ones) + _dot(lo, ones)


def _tile_lanes(x, n):
    reps = n // x.shape[1]
    return x if reps == 1 else jnp.concatenate([x] * reps, axis=1)


def _transpose_bf16(x):
    return x.astype(F32).T.astype(BF16)


def _ada_kernel(c_ref, w_ref, b_ref, o_ref):
    c = c_ref[...]
    a = (c * _sigmoid(c)).astype(BF16)
    o_ref[...] = _dot(a, w_ref[...].astype(BF16)) + b_ref[...]


def _ada(c, w, b):
    n, d = c.shape
    nout = w.shape[1]
    tn = 1024 if nout % 1024 == 0 else 512
    return pl.pallas_call(
        _ada_kernel,
        out_shape=jax.ShapeDtypeStruct((n, nout), F32),
        grid=(nout // tn,),
        in_specs=[pl.BlockSpec((n, d), lambda j: (0, 0)),
                  pl.BlockSpec((d, tn), lambda j: (0, j)),
                  pl.BlockSpec((1, tn), lambda j: (0, j))],
        out_specs=pl.BlockSpec((n, tn), lambda j: (0, j)),
        compiler_params=pltpu.CompilerParams(
            dimension_semantics=("arbitrary",),
            vmem_limit_bytes=_vmem_limit(_nbytes((d, tn), F32) + _nbytes((n, tn), F32),
                                         _nbytes((n, d), F32) + _nbytes((d, tn), BF16))),
        name="ada",
    )(c, w, b.reshape(1, nout))


def _mixer_kernel(x_ref, sh_ref, sc_ref, g_ref, w_ref, cs_ref, wf_ref, bf_ref,
                  zq_ref, ka_ref, va_ref, kb_ref, vb_ref, lf_ref, h_scr,
                  *, nb, t, tiles_per_batch, tail, npg):
    i = pl.program_id(0)
    j = pl.program_id(1)

    @pl.when(j == 0)
    def _():
        h = _mod_norm(x_ref[...], g_ref, sc_ref, sh_ref, nb, t).astype(BF16)
        h_scr[...] = h
        fa = _dot(h, wf_ref[...])[:, :N_HEADS] + bf_ref[...]
        lf_ref[...] = jnp.minimum(fa, 0.0) - jnp.log1p(jnp.exp(-jnp.abs(fa)))

    acc = _dot(h_scr[...], w_ref[...])
    zq_ref[...] = (acc * cs_ref[...]).astype(BF16)

    def _group(g):
        return (j >= g * npg) & (j < (g + 1) * npg)

    @pl.when(_group(1))
    def _():
        ka_ref[...] = acc

    @pl.when(_group(2))
    def _():
        va_ref[...] = acc

    is_tail = (i % tiles_per_batch) == (tiles_per_batch - 1)
    rows = acc.shape[0]

    @pl.when(is_tail & _group(4))
    def _():
        kb_ref[...] = acc[rows - tail:, :]

    @pl.when(is_tail & _group(5))
    def _():
        vb_ref[...] = acc[rows - tail:, :]


def _mixer_in(x2d, shift, scale, g, w_main, col_scale, w_f, b_f, *, nb, t, tiles_per_batch, tail):
    m, d = x2d.shape
    n = w_main.shape[1]
    tm = nb * t
    tn = 1024 if n % 1024 == 0 else 512
    npg = D_HEADS // tn
    n_batches = (m // tm) // tiles_per_batch
    if nb == 1:
        mod_map = lambda i, j: (i // tiles_per_batch, 0, 0)
    else:
        mod_map = lambda i, j: (i, 0, 0)

    def grp_map(gidx):
        return lambda i, j: (i, jnp.clip(j - gidx * npg, 0, npg - 1))

    def tail_map(gidx):
        def f(i, j):
            is_tail = (i % tiles_per_batch) == (tiles_per_batch - 1)
            return (i // tiles_per_batch, jnp.where(is_tail, jnp.clip(j - gidx * npg, 0, npg - 1), 0))
        return f

    kern = functools.partial(_mixer_kernel, nb=nb, t=t, tiles_per_batch=tiles_per_batch, tail=tail, npg=npg)
    pipelined = _nbytes((tm, d), F32) + _nbytes((d, tn), BF16) + _nbytes((tm, tn), BF16)
    resident = (_nbytes((tm, d), BF16) + 2 * _nbytes((tm, tn), F32) + 2 * _nbytes((tail, tn), F32)
                + 2 * _nbytes((tm, tn), F32))
    return pl.pallas_call(
        kern,
        out_shape=(jax.ShapeDtypeStruct((m, n), BF16),
                   jax.ShapeDtypeStruct((m, D_HEADS), F32),
                   jax.ShapeDtypeStruct((m, D_HEADS), F32),
                   jax.ShapeDtypeStruct((n_batches * tail, D_HEADS), F32),
                   jax.ShapeDtypeStruct((n_batches * tail, D_HEADS), F32),
                   jax.ShapeDtypeStruct((m, N_HEADS), F32)),
        grid=(m // tm, n // tn),
        in_specs=[pl.BlockSpec((tm, d), lambda i, j: (i, 0)),
                  pl.BlockSpec((nb, 1, d), mod_map),
                  pl.BlockSpec((nb, 1, d), mod_map),
                  pl.BlockSpec((1, d), lambda i, j: (0, 0)),
                  pl.BlockSpec((d, tn), lambda i, j: (0, j)),
                  pl.BlockSpec((1, tn), lambda i, j: (0, j)),
                  pl.BlockSpec((d, LANES), lambda i, j: (0, 0)),
                  pl.BlockSpec((1, N_HEADS), lambda i, j: (0, 0))],
        out_specs=(pl.BlockSpec((tm, tn), lambda i, j: (i, j)),
                   pl.BlockSpec((tm, tn), grp_map(1), pipeline_mode=pl.Buffered(1)),
                   pl.BlockSpec((tm, tn), grp_map(2), pipeline_mode=pl.Buffered(1)),
                   pl.BlockSpec((tail, tn), tail_map(4), pipeline_mode=pl.Buffered(1)),
                   pl.BlockSpec((tail, tn), tail_map(5), pipeline_mode=pl.Buffered(1)),
                   pl.BlockSpec((tm, N_HEADS), lambda i, j: (i, 0))),
        scratch_shapes=[pltpu.VMEM((tm, d), BF16)],
        compiler_params=pltpu.CompilerParams(
            dimension_semantics=("arbitrary", "arbitrary"),
            vmem_limit_bytes=_vmem_limit(pipelined, resident)),
        name="mixer_in",
    )(x2d, shift, scale, g, w_main, col_scale, w_f, b_f)


def _cumsum_kernel(x_ref, o_ref):
    grp, nr, _ = x_ref.shape
    rows = grp * nr
    r = lax.broadcasted_iota(jnp.int32, (LANES, LANES), 0)
    c = lax.broadcasted_iota(jnp.int32, (LANES, LANES), 1)
    upper = (r <= c).astype(BF16)
    rr = lax.broadcasted_iota(jnp.int32, (rows, rows), 0)
    cc = lax.broadcasted_iota(jnp.int32, (rows, rows), 1)

    def group(ix):
        return sum((ix >= gi * nr).astype(jnp.int32) for gi in range(1, grp))

    lower = ((cc < rr) & (group(cc) == group(rr))).astype(BF16)
    hi, mid, lo = _split3(x_ref[...].reshape(rows, LANES))
    within = _dot(hi, upper) + _dot(mid, upper) + _dot(lo, upper)
    tot = jnp.broadcast_to(within[:, LANES - 1:LANES], (rows, LANES))
    hi, mid, lo = _split3(tot)
    before = _dot(lower, hi) + _dot(lower, mid) + _dot(lower, lo)
    o_ref[...] = (within + before).reshape(grp, nr, LANES)


def _cumsum_rows(x):
    rows, length = x.shape
    nr = length // LANES
    grp = 8
    x3 = x.reshape(rows, nr, LANES)
    out = pl.pallas_call(
        _cumsum_kernel,
        out_shape=jax.ShapeDtypeStruct(x3.shape, F32),
        grid=(rows // grp,),
        in_specs=[pl.BlockSpec((grp, nr, LANES), lambda i: (i, 0, 0))],
        out_specs=pl.BlockSpec((grp, nr, LANES), lambda i: (i, 0, 0)),
        compiler_params=pltpu.CompilerParams(dimension_semantics=("arbitrary",)),
        name="cumsum_logf",
    )(x3)
    return out.reshape(rows, length)


def _fox_prompt_kernel(q_ref, k_ref, v_ref, f_ref, o_ref, qxt_scr, kx_scr, vt_scr, acc_scr, sa_scr, sb_scr,
                       pa_scr, pb_scr,
                       *, tq, tk):
    s_len = q_ref.shape[0]
    assert tq == 2 * tk
    sub = lax.broadcasted_iota(jnp.int32, (HEAD_DIM, tk), 0)
    minus_ones = jnp.where(sub < 3, -1.0, 0.0).astype(BF16)

    def setup(ci, carry):
        c0 = pl.multiple_of(ci * tk, tk)
        rows = pl.ds(c0, tk)
        qxt_scr[0:HEAD_DIM, rows] = _transpose_bf16(q_ref[rows, :])
        qxt_scr[HEAD_DIM:, rows] = minus_ones
        vt_scr[:, rows] = _transpose_bf16(v_ref[rows, :])
        hi, mid, lo = [part.astype(F32) for part in _split3(f_ref[:, rows] * LOG2E)]
        ext_t = jnp.where(sub == 0, hi, jnp.where(sub == 1, mid, jnp.where(sub == 2, lo, 0.0)))
        kx_scr[rows, 0:HEAD_DIM] = k_ref[rows, :]
        kx_scr[rows, HEAD_DIM:] = ext_t.T.astype(BF16)
        return carry

    lax.fori_loop(0, s_len // tk, setup, 0)

    key_ix = lax.broadcasted_iota(jnp.int32, (tq, tq), 0)
    qry_ix = lax.broadcasted_iota(jnp.int32, (tq, tq), 1)

    def scores(r0, c0):
        c0 = pl.multiple_of(c0, tq)
        return _dot(kx_scr[pl.ds(c0, tq), :], qxt_scr[:, pl.ds(r0, tq)])

    def softmax(s, f_q, m_old, l_old, p_ref):
        t_max = jnp.max(s, axis=0, keepdims=True) + f_q
        if m_old is None:
            m_new, alpha = t_max, None
        else:
            m_new = jnp.maximum(m_old, t_max)
            alpha = jnp.exp2(m_old - m_new)
        p = jnp.exp2(s - (m_new - f_q))
        p_sum = jnp.sum(p, axis=0, keepdims=True)
        p_ref[...] = p.astype(BF16)
        return m_new, (p_sum if m_old is None else alpha * l_old + p_sum), alpha

    def accumulate(c0, p_ref, alpha):
        c0 = pl.multiple_of(c0, tk)
        pv = _dot(vt_scr[:, pl.ds(c0, tk)], p_ref[...])
        acc_scr[...] = pv if alpha is None else alpha * acc_scr[...] + pv

    def open_block(r0):
        d = jnp.where(key_ix <= qry_ix, scores(r0, r0), NEG_INF)
        sa_scr[...] = scores(r0, 0)
        return d[0:tk], d[tk:]

    def diag_block(r0, f_q, d0, d1):
        m, l, _ = softmax(d0, f_q, None, None, pa_scr)
        accumulate(r0, pa_scr, None)
        m, l, alpha = softmax(d1, f_q, m, l, pb_scr)
        return r0 + tk, alpha, m, l

    def key_block(s_ref, c0, f_q, state):
        c_pend, alpha_pend, m, l = state
        accumulate(c_pend, pb_scr, alpha_pend)
        m, l, alpha_0 = softmax(s_ref[0:tk, :], f_q, m, l, pa_scr)
        accumulate(c0, pa_scr, alpha_0)
        m, l, alpha_1 = softmax(s_ref[tk:, :], f_q, m, l, pb_scr)
        return c0 + tk, alpha_1, m, l

    def close_block(r0, c_pend, alpha_pend, l):
        accumulate(c_pend, pb_scr, alpha_pend)
        o_t = acc_scr[...] * (1.0 / l)
        o_ref[pl.ds(pl.multiple_of(r0, tq), tq), :] = o_t.T.astype(BF16)

    def f_query(r0):
        return f_ref[:, pl.ds(r0, tq)] * LOG2E

    d0, d1 = open_block(0)
    c_pend, alpha_pend, _, l = diag_block(0, f_query(0), d0, d1)

    def q_body(qi, prev):
        r0 = pl.multiple_of(qi * tq, tq)
        f_q = f_query(r0)
        d0, d1 = open_block(r0)
        close_block(r0 - tq, *prev)
        state = diag_block(r0, f_q, d0, d1)

        def pair_body(pi, state):
            c0 = 2 * pi * tq
            sb_scr[...] = scores(r0, c0 + tq)
            state = key_block(sa_scr, c0, f_q, state)
            sa_scr[...] = scores(r0, c0 + 2 * tq)
            return key_block(sb_scr, c0 + tq, f_q, state)

        def last_body(_, state):
            return key_block(sa_scr, (qi - 1) * tq, f_q, state)

        state = lax.fori_loop(0, lax.shift_right_logical(qi, 1), pair_body, state)
        c_pend, alpha_pend, _, l = lax.fori_loop(0, qi & 1, last_body, state)
        return c_pend, alpha_pend, l

    n_q = s_len // tq
    last = lax.fori_loop(1, n_q, q_body, (c_pend, alpha_pend, l))
    close_block((n_q - 1) * tq, *last)


def _fox_prompt(zq, f_rows, *, n_batch, s_len):
    tq = min(FOX_TQ, s_len)
    tk = min(FOX_TK, tq)
    blk = (s_len, HEAD_DIM)
    scratch = [pltpu.VMEM((2 * HEAD_DIM, s_len), BF16), pltpu.VMEM((s_len, 2 * HEAD_DIM), BF16),
               pltpu.VMEM((HEAD_DIM, s_len), BF16), pltpu.VMEM((HEAD_DIM, tq), F32),
               pltpu.VMEM((tq, tq), F32), pltpu.VMEM((tq, tq), F32),
               pltpu.VMEM((tk, tq), BF16), pltpu.VMEM((tk, tq), BF16)]
    resident = 5 * _nbytes(blk, BF16) + 8 * _nbytes((tk, tq), F32)
    return pl.pallas_call(
        functools.partial(_fox_prompt_kernel, tq=tq, tk=tk),
        out_shape=jax.ShapeDtypeStruct((n_batch * s_len, D_HEADS), BF16),
        grid=(n_batch, N_HEADS),
        in_specs=[pl.BlockSpec(blk, lambda b, h: (b, h)),
                  pl.BlockSpec(blk, lambda b, h: (b, N_HEADS + h)),
                  pl.BlockSpec(blk, lambda b, h: (b, 2 * N_HEADS + h)),
                  pl.BlockSpec((None, 1, s_len), lambda b, h: (b * N_HEADS + h, 0, 0))],
        out_specs=pl.BlockSpec(blk, lambda b, h: (b, h)),
        scratch_shapes=scratch,
        compiler_params=pltpu.CompilerParams(
            dimension_semantics=("arbitrary", "arbitrary"),
            vmem_limit_bytes=_vmem_limit(4 * _nbytes(blk, BF16) + _nbytes((8, s_len), F32), resident)),
        name="fox_prompt",
    )(zq, zq, zq, f_rows)


BIAS_ROWS = 4 * REL_CLIP
BIAS_C0 = BAND_PAD - 2 * REL_CLIP


def _band_bias_kernel(tbl_ref, mbt_ref, mbs_ref, g_scr):
    h = pl.program_id(0)
    far = tbl_ref[h, 2 * REL_CLIP]
    diff = (lax.broadcasted_iota(jnp.int32, (BIAS_ROWS, LANES), 0) + BIAS_C0
            - lax.broadcasted_iota(jnp.int32, (BIAS_ROWS, LANES), 1))
    g_scr[...] = jnp.full((BIAS_ROWS, LANES), far, F32)
    for tix in range(1, 2 * REL_CLIP):
        d = BAND_PAD + REL_CLIP - tix
        lo = d - BIAS_C0
        a8 = max(0, lo // 8 * 8)
        b8 = min(BIAS_ROWS, -(-(lo + LANES) // 8) * 8)
        g_scr[a8:b8, :] = jnp.where(diff[a8:b8, :] == d, tbl_ref[h, tix], g_scr[a8:b8, :])
    g = g_scr[...]

    def far_rows(n):
        return jnp.full((n, LANES), far, F32)

    def masked(bias, q_ax, k_ax):
        rf = lax.broadcasted_iota(jnp.int32, bias.shape, q_ax) // CHUNK
        cf = lax.broadcasted_iota(jnp.int32, bias.shape, k_ax) // CHUNK
        visible = (cf >= rf) & (cf <= rf + N_PREV_CHUNKS)
        return jnp.where(visible, bias * LOG2E, NEG_INF)

    blocks = [jnp.concatenate([far_rows(BIAS_C0 + LANES * j), g[0:BAND_W - BIAS_C0 - LANES * j]], axis=0)
              for j in range(BAND_TQ // LANES)]
    mbt_ref[...] = masked(jnp.concatenate(blocks, axis=1), 1, 0)
    near = g[0:BAND_W_S - BIAS_C0].T
    mbs_ref[...] = masked(jnp.concatenate([jnp.full((BAND_TQ_S, BIAS_C0), far, F32), near], axis=1), 0, 1)


def _band_bias(table):
    assert BAND_TQ_S == LANES and BAND_W <= BIAS_C0 + BIAS_ROWS
    return pl.pallas_call(
        _band_bias_kernel,
        out_shape=(jax.ShapeDtypeStruct((N_HEADS, BAND_W, BAND_TQ), F32),
                   jax.ShapeDtypeStruct((N_HEADS, BAND_TQ_S, BAND_W_S), F32)),
        grid=(N_HEADS,),
        in_specs=[pl.BlockSpec(memory_space=pltpu.SMEM)],
        out_specs=(pl.BlockSpec((None, BAND_W, BAND_TQ), lambda h: (h, 0, 0)),
                   pl.BlockSpec((None, BAND_TQ_S, BAND_W_S), lambda h: (h, 0, 0))),
        scratch_shapes=[pltpu.VMEM((BIAS_ROWS, LANES), F32)],
        compiler_params=pltpu.CompilerParams(dimension_semantics=("arbitrary",)),
        name="band_bias",
    )(table.T)


def _band_prompt_kernel(q_ref, k_ref, v_ref, mb_ref, o_ref, qt_scr, kpad, vtpad, s_scr, pa_scr, pb_scr):
    s_len = q_ref.shape[0]
    kpad[0:BAND_PAD, :] = jnp.zeros((BAND_PAD, HEAD_DIM), BF16)
    kpad[BAND_PAD:, :] = k_ref[...]
    vtpad[:, 0:BAND_PAD] = jnp.zeros((HEAD_DIM, BAND_PAD), BF16)

    def setup(ci, carry):
        c0 = pl.multiple_of(ci * BAND_TQ, BAND_TQ)
        qt_scr[:, pl.ds(c0, BAND_TQ)] = _transpose_bf16(q_ref[pl.ds(c0, BAND_TQ), :])
        vtpad[:, pl.ds(pl.multiple_of(BAND_PAD + c0, BAND_TQ), BAND_TQ)] = _transpose_bf16(v_ref[pl.ds(c0, BAND_TQ), :])
        return carry

    n_tiles = s_len // BAND_TQ
    lax.fori_loop(0, n_tiles, setup, 0)
    key_ix = lax.broadcasted_iota(jnp.int32, (BAND_W, BAND_TQ), 0)

    def scores(ti):
        r0 = pl.multiple_of(ti * BAND_TQ, BAND_TQ)
        return _dot(kpad[pl.ds(r0, BAND_W), :], qt_scr[:, pl.ds(r0, BAND_TQ)]) + mb_ref[...]

    def softmax(s, p_ref):
        m = jnp.max(s, axis=0, keepdims=True)
        p = jnp.exp2(s - m)
        p_ref[...] = p.astype(BF16)
        return 1.0 / jnp.sum(p, axis=0, keepdims=True)

    def emit(ti, p_ref, r_den):
        r0 = pl.multiple_of(ti * BAND_TQ, BAND_TQ)
        o_t = _dot(vtpad[:, pl.ds(r0, BAND_W)], p_ref[...]) * r_den
        o_ref[pl.ds(r0, BAND_TQ), :] = o_t.T.astype(BF16)

    assert BAND_PAD // BAND_TQ == 2
    h0 = jnp.where(key_ix >= BAND_PAD, scores(0), NEG_INF)
    h1 = jnp.where(BAND_TQ + key_ix >= BAND_PAD, scores(1), NEG_INF)
    s_scr[...] = scores(2)
    r_0 = softmax(h0, pa_scr)
    emit(0, pa_scr, r_0)
    r_1 = softmax(h1, pb_scr)

    def pair(pi, r_pend):
        a = 2 + 2 * pi
        s_b = scores(a + 1)
        emit(a - 1, pb_scr, r_pend)
        r_a = softmax(s_scr[...], pa_scr)
        s_scr[...] = scores(jnp.minimum(a + 2, n_tiles - 1))
        emit(a, pa_scr, r_a)
        return softmax(s_b, pb_scr)

    r_last = lax.fori_loop(0, (n_tiles - 2) // 2, pair, r_1)
    emit(n_tiles - 1, pb_scr, r_last)


def _band_prompt(zq, mbt, *, n_batch, s_len):
    blk = (s_len, HEAD_DIM)
    off = 3 * N_HEADS
    assert (s_len // BAND_TQ - BAND_PAD // BAND_TQ) % 2 == 0
    scratch = [pltpu.VMEM((HEAD_DIM, s_len), BF16), pltpu.VMEM((BAND_PAD + s_len, HEAD_DIM), BF16),
               pltpu.VMEM((HEAD_DIM, BAND_PAD + s_len), BF16), pltpu.VMEM((BAND_W, BAND_TQ), F32),
               pltpu.VMEM((BAND_W, BAND_TQ), BF16), pltpu.VMEM((BAND_W, BAND_TQ), BF16)]
    resident = 3 * _nbytes((BAND_PAD + s_len, HEAD_DIM), BF16) + 8 * _nbytes((BAND_W, BAND_TQ), F32)
    return pl.pallas_call(
        _band_prompt_kernel,
        out_shape=jax.ShapeDtypeStruct((n_batch * s_len, D_HEADS), BF16),
        grid=(n_batch, N_HEADS),
        in_specs=[pl.BlockSpec(blk, lambda b, h: (b, off + h)),
                  pl.BlockSpec(blk, lambda b, h: (b, off + N_HEADS + h)),
                  pl.BlockSpec(blk, lambda b, h: (b, off + 2 * N_HEADS + h)),
                  pl.BlockSpec((None, BAND_W, BAND_TQ), lambda b, h: (h, 0, 0))],
        out_specs=pl.BlockSpec(blk, lambda b, h: (b, h)),
        scratch_shapes=scratch,
        compiler_params=pltpu.CompilerParams(
            dimension_semantics=("arbitrary", "arbitrary"),
            vmem_limit_bytes=_vmem_limit(4 * _nbytes(blk, BF16) + _nbytes((BAND_W, BAND_TQ), F32), resident)),
        name="band_prompt",
    )(zq, zq, zq, mbt)


def _fox_sample_kernel(q_ref, kn_ref, vn_ref, ck_ref, cv_ref, f_ref, o_ref,
                       m_scr, l_scr, acc_scr, fq_scr, *, past, tk):
    kt = pl.program_id(1)
    t_new = q_ref.shape[0]

    def heads(h):
        return slice(h * HEAD_DIM, (h + 1) * HEAD_DIM)

    @pl.when(kt == 0)
    def _():
        for h in range(N_HEADS):
            fq_scr[h] = _col_bcast(f_ref[h:h + 1, past:past + t_new] * LOG2E, t_new)
        m_scr[...] = jnp.full(m_scr.shape, NEG_INF, F32)
        l_scr[...] = jnp.zeros(l_scr.shape, F32)
        acc_scr[...] = jnp.zeros(acc_scr.shape, F32)

    def update(h, s, v):
        m_old = m_scr[h]
        m_new = jnp.maximum(m_old, jnp.max(s, axis=1, keepdims=True))
        alpha = jnp.exp2(m_old - m_new)
        p = jnp.exp2(s - m_new)
        l_scr[h] = alpha * l_scr[h] + jnp.sum(p, axis=1, keepdims=True)
        acc_scr[h] = alpha * acc_scr[h] + _dot(p.astype(BF16), v)
        m_scr[h] = m_new

    c0 = pl.multiple_of(kt * tk, tk)
    scores = []
    for h in range(N_HEADS):
        k = ck_ref[pl.ds(h, tk, stride=N_HEADS), :].astype(BF16)
        scores.append(_dot_nt(q_ref[:, heads(h)], k) + _tile_lanes(fq_scr[h], tk)
                      - f_ref[h:h + 1, pl.ds(c0, tk)] * LOG2E)
    for h in range(N_HEADS):
        update(h, scores[h], cv_ref[pl.ds(h, tk, stride=N_HEADS), :].astype(BF16))

    @pl.when(kt == pl.num_programs(1) - 1)
    def _():
        r = lax.broadcasted_iota(jnp.int32, (t_new, t_new), 0)
        c = lax.broadcasted_iota(jnp.int32, (t_new, t_new), 1)
        for h in range(N_HEADS):
            q = q_ref[:, heads(h)]
            f_new = f_ref[h:h + 1, past:past + t_new] * LOG2E
            s = _dot_nt(q, kn_ref[:, heads(h)]) + fq_scr[h][:, :t_new] - f_new
            s = jnp.where(c <= r, s, NEG_INF)
            update(h, s, vn_ref[:, heads(h)])
            o_ref[:, heads(h)] = (acc_scr[h] / l_scr[h]).astype(BF16)


def _fox_sample(zq, cache_k, cache_v, f_all, *, n_batch, t_new, past):
    tk = min(2048, past)
    lp = f_all.shape[-1]
    blk = (t_new, D_HEADS)
    cblk = (None, tk * N_HEADS, HEAD_DIM)
    return pl.pallas_call(
        functools.partial(_fox_sample_kernel, past=past, tk=tk),
        out_shape=jax.ShapeDtypeStruct((n_batch * t_new, D_HEADS), BF16),
        grid=(n_batch, past // tk),
        in_specs=[pl.BlockSpec(blk, lambda b, k: (b, 0)),
                  pl.BlockSpec(blk, lambda b, k: (b, 1)),
                  pl.BlockSpec(blk, lambda b, k: (b, 2)),
                  pl.BlockSpec(cblk, lambda b, k: (b, k, 0)),
                  pl.BlockSpec(cblk, lambda b, k: (b, k, 0)),
                  pl.BlockSpec((None, N_HEADS, lp), lambda b, k: (b, 0, 0))],
        out_specs=pl.BlockSpec(blk, lambda b, k: (b, 0)),
        scratch_shapes=[pltpu.VMEM((N_HEADS, t_new, 1), F32), pltpu.VMEM((N_HEADS, t_new, 1), F32),
                        pltpu.VMEM((N_HEADS, t_new, HEAD_DIM), F32),
                        pltpu.VMEM((N_HEADS, t_new, LANES), F32)],
        compiler_params=pltpu.CompilerParams(
            dimension_semantics=("arbitrary", "arbitrary"),
            vmem_limit_bytes=_vmem_limit(2 * _nbytes((tk, D_HEADS), F32) + 4 * _nbytes(blk, BF16)
                                         + _nbytes((N_HEADS, lp), F32), 8 * MIB)),
        name="fox_sample",
    )(zq, zq, zq, cache_k, cache_v, f_all)


def _band_sample_kernel(q_ref, kn_ref, vn_ref, ck_ref, cv_ref, mb_ref, o_ref):
    t_new = q_ref.shape[0]
    lb = ck_ref.shape[0] // N_HEADS
    for h in range(N_HEADS):
        hs = slice(h * HEAD_DIM, (h + 1) * HEAD_DIM)
        q = q_ref[:, hs]
        s1 = _dot_nt(q, ck_ref[pl.ds(h, lb, stride=N_HEADS), :].astype(BF16)) + mb_ref[h, :, 0:lb]
        s2 = _dot_nt(q, kn_ref[:, hs]) + mb_ref[h, :, lb:lb + t_new]
        m = jnp.maximum(jnp.max(s1, axis=1, keepdims=True), jnp.max(s2, axis=1, keepdims=True))
        p1 = jnp.exp2(s1 - m)
        p2 = jnp.exp2(s2 - m)
        den = jnp.sum(p1, axis=1, keepdims=True) + jnp.sum(p2, axis=1, keepdims=True)
        o = _dot(p1.astype(BF16), cv_ref[pl.ds(h, lb, stride=N_HEADS), :].astype(BF16)) + _dot(p2.astype(BF16), vn_ref[:, hs])
        o_ref[:, hs] = (o / den).astype(BF16)


def _band_sample(zq, cache_k, cache_v, mb, *, n_batch, t_new):
    lb = cache_k.shape[1] // N_HEADS
    blk = (t_new, D_HEADS)
    cblk = (None, lb * N_HEADS, HEAD_DIM)
    return pl.pallas_call(
        _band_sample_kernel,
        out_shape=jax.ShapeDtypeStruct((n_batch * t_new, D_HEADS), BF16),
        grid=(n_batch,),
        in_specs=[pl.BlockSpec(blk, lambda b: (b, 3)),
                  pl.BlockSpec(blk, lambda b: (b, 4)),
                  pl.BlockSpec(blk, lambda b: (b, 5)),
                  pl.BlockSpec(cblk, lambda b: (b, 0, 0)),
                  pl.BlockSpec(cblk, lambda b: (b, 0, 0)),
                  pl.BlockSpec((N_HEADS, t_new, BAND_W_S), lambda b: (0, 0, 0))],
        out_specs=pl.BlockSpec(blk, lambda b: (b, 0)),
        compiler_params=pltpu.CompilerParams(
            dimension_semantics=("arbitrary",),
            vmem_limit_bytes=_vmem_limit(2 * _nbytes((lb, D_HEADS), F32) + 4 * _nbytes(blk, BF16),
                                         _nbytes((N_HEADS, t_new, BAND_W_S), F32) + 4 * MIB)),
        name="band_sample",
    )(zq, zq, zq, cache_k, cache_v, mb)


def _merge_kernel(oa_ref, ob_ref, za_ref, zb_ref, x_ref, gt_ref, woa_ref, wob_ref, wout_ref, o_ref, mix_scr, *, nb, t):
    d = x_ref.shape[1]
    half = d // 2
    for c in range(2):
        cols = slice(c * half, (c + 1) * half)
        a = _dot(oa_ref[...], woa_ref[:, cols])
        b = _dot(ob_ref[...], wob_ref[:, cols])
        mix = _sigmoid(za_ref[:, cols].astype(F32)) * a + _sigmoid(zb_ref[:, cols].astype(F32)) * b
        mix_scr[:, cols] = mix.astype(BF16)
    for c in range(2):
        cols = slice(c * half, (c + 1) * half)
        y = _dot(mix_scr[...], wout_ref[:, cols])
        x = x_ref[:, cols]
        if nb == 1:
            o_ref[:, cols] = x + gt_ref[0][:, cols] * y
        else:
            gt = gt_ref[...][:, :, cols]
            o_ref[:, cols] = (x.reshape(nb, t, half) + gt * y.reshape(nb, t, half)).reshape(x.shape)


def _merge(oa, ob, zq, x2d, gate, w_oa, w_ob, w_out, *, nb, t, tiles_per_batch):
    m, d = x2d.shape
    tm = nb * t
    za_blk = Z_COLS // d
    if nb == 1:
        mod_map = lambda i: (i // tiles_per_batch, 0, 0)
    else:
        mod_map = lambda i: (i, 0, 0)
    const = lambda i: (0, 0)
    pipelined = (2 * _nbytes((tm, D_HEADS), BF16) + 2 * _nbytes((tm, d), BF16) + 2 * _nbytes((tm, d), F32))
    resident = (2 * _nbytes((D_HEADS, d), BF16) + _nbytes((d, d), BF16) + _nbytes((tm, d), BF16)
                + 2 * _nbytes((tm, d), F32))
    return pl.pallas_call(
        functools.partial(_merge_kernel, nb=nb, t=t),
        out_shape=jax.ShapeDtypeStruct((m, d), F32),
        grid=(m // tm,),
        in_specs=[pl.BlockSpec((tm, D_HEADS), lambda i: (i, 0)),
                  pl.BlockSpec((tm, D_HEADS), lambda i: (i, 0)),
                  pl.BlockSpec((tm, d), lambda i: (i, za_blk)),
                  pl.BlockSpec((tm, d), lambda i: (i, za_blk + 1)),
                  pl.BlockSpec((tm, d), lambda i: (i, 0)),
                  pl.BlockSpec((nb, 1, d), mod_map),
                  pl.BlockSpec((D_HEADS, d), const, pipeline_mode=pl.Buffered(1)),
                  pl.BlockSpec((D_HEADS, d), const, pipeline_mode=pl.Buffered(1)),
                  pl.BlockSpec((d, d), const, pipeline_mode=pl.Buffered(1))],
        out_specs=pl.BlockSpec((tm, d), lambda i: (i, 0)),
        scratch_shapes=[pltpu.VMEM((tm, d), BF16)],
        compiler_params=pltpu.CompilerParams(
            dimension_semantics=("arbitrary",),
            vmem_limit_bytes=_vmem_limit(pipelined, resident)),
        name="merge_out",
    )(oa, ob, zq, zq, x2d, gate, w_oa, w_ob, w_out)


def _ffn_kernel(x_ref, sh_ref, sc_ref, gt_ref, g_ref, gf_ref, wg_ref, wu_ref, wd_ref, o_ref,
                h_scr, acc_scr, *, nb, t):
    f = pl.program_id(1)

    @pl.when(f == 0)
    def _():
        h_scr[...] = _mod_norm(x_ref[...], g_ref, sc_ref, sh_ref, nb, t).astype(BF16)
        acc_scr[...] = jnp.zeros(acc_scr.shape, F32)

    h = h_scr[...]
    gate = _dot(h, wg_ref[...])
    up = _dot(h, wu_ref[...])
    act = (gate * _sigmoid(gate) * up).astype(BF16)
    acc_scr[...] += _dot(act, wd_ref[...])

    @pl.when(f == pl.num_programs(1) - 1)
    def _():
        x = x_ref[...]
        y, gt = _per_batch(acc_scr[...], gt_ref, nb, t)
        x3, _ = _per_batch(x, gt_ref, nb, t)
        x2 = (x3 + gt * y).reshape(x.shape)
        o_ref[...] = _rms_scale(x2) * gf_ref[...]


def _ffn(x2d, shift, scale, gate, g_ffn, g_final, w_gate, w_up, w_down, *, nb, t, tiles_per_batch):
    m, d = x2d.shape
    dff = w_gate.shape[1]
    tm = nb * t
    tf = 512
    if nb == 1:
        mod_map = lambda i, f: (i // tiles_per_batch, 0, 0)
    else:
        mod_map = lambda i, f: (i, 0, 0)
    pipelined = 2 * _nbytes((tm, d), F32) + 3 * _nbytes((d, tf), BF16)
    resident = _nbytes((tm, d), BF16) + _nbytes((tm, d), F32) + 3 * _nbytes((tm, tf), F32) + 2 * _nbytes((tm, d), F32)
    return pl.pallas_call(
        functools.partial(_ffn_kernel, nb=nb, t=t),
        out_shape=jax.ShapeDtypeStruct((m, d), F32),
        grid=(m // tm, dff // tf),
        in_specs=[pl.BlockSpec((tm, d), lambda i, f: (i, 0)),
                  pl.BlockSpec((nb, 1, d), mod_map),
                  pl.BlockSpec((nb, 1, d), mod_map),
                  pl.BlockSpec((nb, 1, d), mod_map),
                  pl.BlockSpec((1, d), lambda i, f: (0, 0)),
                  pl.BlockSpec((1, d), lambda i, f: (0, 0)),
                  pl.BlockSpec((d, tf), lambda i, f: (0, f)),
                  pl.BlockSpec((d, tf), lambda i, f: (0, f)),
                  pl.BlockSpec((tf, d), lambda i, f: (f, 0))],
        out_specs=pl.BlockSpec((tm, d), lambda i, f: (i, 0)),
        scratch_shapes=[pltpu.VMEM((tm, d), BF16), pltpu.VMEM((tm, d), F32)],
        compiler_params=pltpu.CompilerParams(
            dimension_semantics=("arbitrary", "arbitrary"),
            vmem_limit_bytes=_vmem_limit(pipelined, resident)),
        name="ffn_final",
    )(x2d, shift, scale, gate, g_ffn, g_final, w_gate, w_up, w_down)


def _row_tile(s_len, cap):
    tile = min(cap, s_len)
    assert s_len % tile == 0
    return tile


def kernel(x_prompt, x_sample, cache_fox_k, cache_fox_v, cache_fox_logf, cache_band_k, cache_band_v,
           c_prompt, c_sample, w_ada, b_ada, g_mix, w_in, b_f, rel_bias, w_oa, w_ob, w_out,
           g_ffn, w_gate, w_up, w_down, g_final):
    n_b, s_len, d = x_prompt.shape
    n_db, t_new, _ = x_sample.shape
    past = cache_fox_k.shape[2]
    lb = cache_band_k.shape[2]
    assert w_ada.shape[0] == 1, "single-layer trunk"
    assert lb == BAND_PAD and s_len >= BAND_PAD and s_len % FOX_TQ == 0
    assert Z_COLS % d == 0 and t_new % 8 == 0 and t_new + lb <= BAND_W_S and t_new <= CHUNK
    layer = 0
    q_scale = LOG2E / math.sqrt(HEAD_DIM)

    w_l = w_in[layer]
    w_main = jnp.concatenate([w_l[:, :3 * D_HEADS], w_l[:, 3 * D_HEADS + N_HEADS:]], axis=1).astype(BF16)
    w_f = jnp.pad(w_l[:, 3 * D_HEADS:3 * D_HEADS + N_HEADS], ((0, 0), (0, LANES - N_HEADS))).astype(BF16)
    n_main = w_main.shape[1]
    col = jnp.arange(n_main)
    is_q = (col < D_HEADS) | ((col >= 3 * D_HEADS) & (col < 4 * D_HEADS))
    col_scale = jnp.where(is_q, q_scale, 1.0).astype(F32).reshape(1, n_main)
    b_f2 = b_f[layer].reshape(1, N_HEADS)
    g_mix2 = g_mix[layer].reshape(1, d)
    g_ffn2 = g_ffn[layer].reshape(1, d)
    g_fin2 = g_final.reshape(1, d)
    w_oa_b, w_ob_b, w_out_b = w_oa[layer].astype(BF16), w_ob[layer].astype(BF16), w_out[layer].astype(BF16)
    w_gate_b, w_up_b, w_down_b = w_gate[layer].astype(BF16), w_up[layer].astype(BF16), w_down[layer].astype(BF16)

    mods = _ada(jnp.concatenate([c_prompt, c_sample], axis=0), w_ada[layer], b_ada[layer])

    def chunks(rows):
        return [rows[:, k * d:(k + 1) * d][:, None, :] for k in range(6)]

    sh1p, sc1p, gt1p, sh2p, sc2p, gt2p = chunks(mods[:n_b])
    sh1s, sc1s, gt1s, sh2s, sc2s, gt2s = chunks(mods[n_b:])
    mbt, mbs = _band_bias(rel_bias[layer])

    xp = x_prompt.reshape(n_b * s_len, d)
    tm = _row_tile(s_len, 1024)
    zq, ka, va, kb, vb, lf = _mixer_in(xp, sh1p, sc1p, g_mix2, w_main, col_scale, w_f, b_f2,
                                       nb=1, t=tm, tiles_per_batch=s_len // tm, tail=BAND_PAD)
    lf_rows = lf.reshape(n_b, s_len, N_HEADS).transpose(0, 2, 1).reshape(n_b * N_HEADS, s_len)
    f_rows = _cumsum_rows(lf_rows).reshape(n_b * N_HEADS, 1, s_len)
    oa = _fox_prompt(zq, f_rows, n_batch=n_b, s_len=s_len)
    ob = _band_prompt(zq, mbt, n_batch=n_b, s_len=s_len)
    tm = _row_tile(s_len, 512)
    x1 = _merge(oa, ob, zq, xp, gt1p, w_oa_b, w_ob_b, w_out_b, nb=1, t=tm, tiles_per_batch=s_len // tm)
    tm = _row_tile(s_len, 512)
    y_prompt = _ffn(x1, sh2p, sc2p, gt2p, g_ffn2, g_fin2, w_gate_b, w_up_b, w_down_b,
                    nb=1, t=tm, tiles_per_batch=s_len // tm).reshape(n_b, s_len, d)
    fox_shape = (1, n_b, s_len, N_HEADS, HEAD_DIM)
    band_shape = (1, n_b, BAND_PAD, N_HEADS, HEAD_DIM)
    prompt_out = (ka.reshape(fox_shape), va.reshape(fox_shape), lf.reshape(1, n_b, s_len, N_HEADS),
                  kb.reshape(band_shape), vb.reshape(band_shape))

    xs = x_sample.reshape(n_db * t_new, d)
    zqs, kas, vas, kbs, vbs, lfs = _mixer_in(xs, sh1s, sc1s, g_mix2, w_main, col_scale, w_f, b_f2,
                                             nb=n_db, t=t_new, tiles_per_batch=1, tail=n_db * t_new)
    lp = -(-(past + t_new) // 2048) * 2048
    lf_all = jnp.concatenate([cache_fox_logf[layer].transpose(0, 2, 1),
                              lfs.reshape(n_db, t_new, N_HEADS).transpose(0, 2, 1),
                              jnp.zeros((n_db, N_HEADS, lp - past - t_new), F32)], axis=-1)
    f_all = _cumsum_rows(lf_all.reshape(n_db * N_HEADS, lp)).reshape(n_db, N_HEADS, lp)
    oas = _fox_sample(zqs, cache_fox_k[layer].reshape(n_db, past * N_HEADS, HEAD_DIM),
                      cache_fox_v[layer].reshape(n_db, past * N_HEADS, HEAD_DIM), f_all,
                      n_batch=n_db, t_new=t_new, past=past)
    obs = _band_sample(zqs, cache_band_k[layer].reshape(n_db, lb * N_HEADS, HEAD_DIM),
                       cache_band_v[layer].reshape(n_db, lb * N_HEADS, HEAD_DIM), mbs, n_batch=n_db, t_new=t_new)
    x1s = _merge(oas, obs, zqs, xs, gt1s, w_oa_b, w_ob_b, w_out_b, nb=n_db, t=t_new, tiles_per_batch=1)
    y_sample = _ffn(x1s, sh2s, sc2s, gt2s, g_ffn2, g_fin2, w_gate_b, w_up_b, w_down_b,
                    nb=n_db, t=t_new, tiles_per_batch=1).reshape(n_db, t_new, d)
    s_shape = (1, n_db, t_new, N_HEADS, HEAD_DIM)
    sample_out = (kas.reshape(s_shape), vas.reshape(s_shape), lfs.reshape(1, n_db, t_new, N_HEADS),
                  kbs.reshape(s_shape), vbs.reshape(s_shape))

    return (y_prompt, y_sample) + prompt_out + sample_out
```

```python
import functools
import math

import jax
import jax.numpy as jnp
from jax import lax
from jax.experimental import pallas as pl
from jax.experimental.pallas import tpu as pltpu

F32 = jnp.float32
BF16 = jnp.bfloat16

HEAD_DIM = 128
N_HEADS = 8
D_HEADS = N_HEADS * HEAD_DIM
CHUNK = 64
N_PREV_CHUNKS = 8
BAND_PAD = N_PREV_CHUNKS * CHUNK
REL_CLIP = 128
RMS_EPS = 1e-6
NEG_INF = -1e30
LOG2E = math.log2(math.e)
LANES = 128
Z_COLS = 6 * D_HEADS

VMEM_CAP_BYTES = 60 * 1024 * 1024
MIB = 1024 * 1024

BAND_TQ = 256
BAND_W = BAND_PAD + BAND_TQ
BAND_TQ_S = 128
BAND_W_S = BAND_PAD + BAND_TQ_S
FOX_TQ = 512
FOX_TK = 256


def _vmem_limit(pipelined_bytes, resident_bytes=0):
    est = 2 * pipelined_bytes + resident_bytes + 8 * MIB
    return int(min(max(est, 16 * MIB), VMEM_CAP_BYTES))


def _nbytes(shape, dtype):
    return math.prod(shape) * jnp.dtype(dtype).itemsize


def _dot(a, b):
    return jnp.dot(a, b, preferred_element_type=F32)


def _dot_nt(a, b):
    return lax.dot_general(a, b, (((1,), (1,)), ((), ())), preferred_element_type=F32)


def _split3(x):
    hi = x.astype(BF16)
    r1 = x - hi.astype(F32)
    mid = r1.astype(BF16)
    lo = (r1 - mid.astype(F32)).astype(BF16)
    return hi, mid, lo


def _sigmoid(x):
    return 1.0 / (1.0 + jnp.exp(-x))


def _rms_scale(x):
    return x * lax.rsqrt(jnp.mean(x * x, axis=-1, keepdims=True) + RMS_EPS)


def _per_batch(y, ref, nb, t):
    if nb == 1:
        return y, ref[0]
    return y.reshape(nb, t, y.shape[-1]), ref[...]


def _mod_norm(x, g_ref, sc_ref, sh_ref, nb, t):
    y = _rms_scale(x) * g_ref[...]
    y, sc = _per_batch(y, sc_ref, nb, t)
    _, sh = _per_batch(x, sh_ref, nb, t)
    h = y * (1.0 + sc) + sh
    return h.reshape(x.shape)


def _col_bcast(frow, n):
    r = lax.broadcasted_iota(jnp.int32, (n, n), 0)
    c = lax.broadcasted_iota(jnp.int32, (n, n), 1)
    d = jnp.where(r == c, jnp.broadcast_to(frow, (n, n)), 0.0)
    ones = jnp.ones((n, LANES), BF16)
    hi, mid, lo = _split3(d)
    return _dot(hi, ones) + _dot(mid, ones) + _dot(lo, ones)


def _tile_lanes(x, n):
    reps = n // x.shape[1]
    return x if reps == 1 else jnp.concatenate([x] * reps, axis=1)


def _transpose_bf16(x):
    return x.astype(F32).T.astype(BF16)


def _ada_kernel(c_ref, w_ref, b_ref, o_ref):
    c = c_ref[...]
    a = (c * _sigmoid(c)).astype(BF16)
    o_ref[...] = _dot(a, w_ref[...].astype(BF16)) + b_ref[...]


def _ada(c, w, b):
    n, d = c.shape
    nout = w.shape[1]
    tn = 1024 if nout % 1024 == 0 else 512
    return pl.pallas_call(
        _ada_kernel,
        out_shape=jax.ShapeDtypeStruct((n, nout), F32),
        grid=(nout // tn,),
        in_specs=[pl.BlockSpec((n, d), lambda j: (0, 0)),
                  pl.BlockSpec((d, tn), lambda j: (0, j)),
                  pl.BlockSpec((1, tn), lambda j: (0, j))],
        out_specs=pl.BlockSpec((n, tn), lambda j: (0, j)),
        compiler_params=pltpu.CompilerParams(
            dimension_semantics=("arbitrary",),
            vmem_limit_bytes=_vmem_limit(_nbytes((d, tn), F32) + _nbytes((n, tn), F32),
                                         _nbytes((n, d), F32) + _nbytes((d, tn), BF16))),
        name="ada",
    )(c, w, b.reshape(1, nout))


def _mixer_kernel(x_ref, sh_ref, sc_ref, g_ref, w_ref, cs_ref, wf_ref, bf_ref,
                  zq_ref, ka_ref, va_ref, kb_ref, vb_ref, lf_ref, h_scr,
                  *, nb, t, tiles_per_batch, tail, npg):
    i = pl.program_id(0)
    j = pl.program_id(1)

    @pl.when(j == 0)
    def _():
        h = _mod_norm(x_ref[...], g_ref, sc_ref, sh_ref, nb, t).astype(BF16)
        h_scr[...] = h
        fa = _dot(h, wf_ref[...])[:, :N_HEADS] + bf_ref[...]
        lf_ref[...] = jnp.minimum(fa, 0.0) - jnp.log1p(jnp.exp(-jnp.abs(fa)))

    acc = _dot(h_scr[...], w_ref[...])
    zq_ref[...] = (acc * cs_ref[...]).astype(BF16)

    def _group(g):
        return (j >= g * npg) & (j < (g + 1) * npg)

    @pl.when(_group(1))
    def _():
        ka_ref[...] = acc

    @pl.when(_group(2))
    def _():
        va_ref[...] = acc

    is_tail = (i % tiles_per_batch) == (tiles_per_batch - 1)
    rows = acc.shape[0]

    @pl.when(is_tail & _group(4))
    def _():
        kb_ref[...] = acc[rows - tail:, :]

    @pl.when(is_tail & _group(5))
    def _():
        vb_ref[...] = acc[rows - tail:, :]


def _mixer_in(x2d, shift, scale, g, w_main, col_scale, w_f, b_f, *, nb, t, tiles_per_batch, tail):
    m, d = x2d.shape
    n = w_main.shape[1]
    tm = nb * t
    tn = 1024 if n % 1024 == 0 else 512
    npg = D_HEADS // tn
    n_batches = (m // tm) // tiles_per_batch
    if nb == 1:
        mod_map = lambda i, j: (i // tiles_per_batch, 0, 0)
    else:
        mod_map = lambda i, j: (i, 0, 0)

    def grp_map(gidx):
        return lambda i, j: (i, jnp.clip(j - gidx * npg, 0, npg - 1))

    def tail_map(gidx):
        def f(i, j):
            is_tail = (i % tiles_per_batch) == (tiles_per_batch - 1)
            return (i // tiles_per_batch, jnp.where(is_tail, jnp.clip(j - gidx * npg, 0, npg - 1), 0))
        return f

    kern = functools.partial(_mixer_kernel, nb=nb, t=t, tiles_per_batch=tiles_per_batch, tail=tail, npg=npg)
    pipelined = _nbytes((tm, d), F32) + _nbytes((d, tn), BF16) + _nbytes((tm, tn), BF16)
    resident = (_nbytes((tm, d), BF16) + 2 * _nbytes((tm, tn), F32) + 2 * _nbytes((tail, tn), F32)
                + 2 * _nbytes((tm, tn), F32))
    return pl.pallas_call(
        kern,
        out_shape=(jax.ShapeDtypeStruct((m, n), BF16),
                   jax.ShapeDtypeStruct((m, D_HEADS), F32),
                   jax.ShapeDtypeStruct((m, D_HEADS), F32),
                   jax.ShapeDtypeStruct((n_batches * tail, D_HEADS), F32),
                   jax.ShapeDtypeStruct((n_batches * tail, D_HEADS), F32),
                   jax.ShapeDtypeStruct((m, N_HEADS), F32)),
        grid=(m // tm, n // tn),
        in_specs=[pl.BlockSpec((tm, d), lambda i, j: (i, 0)),
                  pl.BlockSpec((nb, 1, d), mod_map),
                  pl.BlockSpec((nb, 1, d), mod_map),
                  pl.BlockSpec((1, d), lambda i, j: (0, 0)),
                  pl.BlockSpec((d, tn), lambda i, j: (0, j)),
                  pl.BlockSpec((1, tn), lambda i, j: (0, j)),
                  pl.BlockSpec((d, LANES), lambda i, j: (0, 0)),
                  pl.BlockSpec((1, N_HEADS), lambda i, j: (0, 0))],
        out_specs=(pl.BlockSpec((tm, tn), lambda i, j: (i, j)),
                   pl.BlockSpec((tm, tn), grp_map(1), pipeline_mode=pl.Buffered(1)),
                   pl.BlockSpec((tm, tn), grp_map(2), pipeline_mode=pl.Buffered(1)),
                   pl.BlockSpec((tail, tn), tail_map(4), pipeline_mode=pl.Buffered(1)),
                   pl.BlockSpec((tail, tn), tail_map(5), pipeline_mode=pl.Buffered(1)),
                   pl.BlockSpec((tm, N_HEADS), lambda i, j: (i, 0))),
        scratch_shapes=[pltpu.VMEM((tm, d), BF16)],
        compiler_params=pltpu.CompilerParams(
            dimension_semantics=("arbitrary", "arbitrary"),
            vmem_limit_bytes=_vmem_limit(pipelined, resident)),
        name="mixer_in",
    )(x2d, shift, scale, g, w_main, col_scale, w_f, b_f)


def _cumsum_kernel(x_ref, o_ref):
    grp, nr, _ = x_ref.shape
    rows = grp * nr
    r = lax.broadcasted_iota(jnp.int32, (LANES, LANES), 0)
    c = lax.broadcasted_iota(jnp.int32, (LANES, LANES), 1)
    upper = (r <= c).astype(BF16)
    rr = lax.broadcasted_iota(jnp.int32, (rows, rows), 0)
    cc = lax.broadcasted_iota(jnp.int32, (rows, rows), 1)

    def group(ix):
        return sum((ix >= gi * nr).astype(jnp.int32) for gi in range(1, grp))

    lower = ((cc < rr) & (group(cc) == group(rr))).astype(BF16)
    hi, mid, lo = _split3(x_ref[...].reshape(rows, LANES))
    within = _dot(hi, upper) + _dot(mid, upper) + _dot(lo, upper)
    tot = jnp.broadcast_to(within[:, LANES - 1:LANES], (rows, LANES))
    hi, mid, lo = _split3(tot)
    before = _dot(lower, hi) + _dot(lower, mid) + _dot(lower, lo)
    o_ref[...] = (within + before).reshape(grp, nr, LANES)


def _cumsum_rows(x):
    rows, length = x.shape
    nr = length // LANES
    grp = 8
    x3 = x.reshape(rows, nr, LANES)
    out = pl.pallas_call(
        _cumsum_kernel,
        out_shape=jax.ShapeDtypeStruct(x3.shape, F32),
        grid=(rows // grp,),
        in_specs=[pl.BlockSpec((grp, nr, LANES), lambda i: (i, 0, 0))],
        out_specs=pl.BlockSpec((grp, nr, LANES), lambda i: (i, 0, 0)),
        compiler_params=pltpu.CompilerParams(dimension_semantics=("arbitrary",)),
        name="cumsum_logf",
    )(x3)
    return out.reshape(rows, length)


def _fox_prompt_kernel(q_ref, k_ref, v_ref, f_ref, o_ref, qxt_scr, kx_scr, vt_scr, acc_scr, sa_scr, sb_scr,
                       pa_scr, pb_scr,
                       *, tq, tk):
    s_len = q_ref.shape[0]
    assert tq == 2 * tk
    sub = lax.broadcasted_iota(jnp.int32, (HEAD_DIM, tk), 0)
    minus_ones = jnp.where(sub < 3, -1.0, 0.0).astype(BF16)

    def setup(ci, carry):
        c0 = pl.multiple_of(ci * tk, tk)
        rows = pl.ds(c0, tk)
        qxt_scr[0:HEAD_DIM, rows] = _transpose_bf16(q_ref[rows, :])
        qxt_scr[HEAD_DIM:, rows] = minus_ones
        vt_scr[:, rows] = _transpose_bf16(v_ref[rows, :])
        hi, mid, lo = [part.astype(F32) for part in _split3(f_ref[:, rows] * LOG2E)]
        ext_t = jnp.where(sub == 0, hi, jnp.where(sub == 1, mid, jnp.where(sub == 2, lo, 0.0)))
        kx_scr[rows, 0:HEAD_DIM] = k_ref[rows, :]
        kx_scr[rows, HEAD_DIM:] = ext_t.T.astype(BF16)
        return carry

    lax.fori_loop(0, s_len // tk, setup, 0)

    key_ix = lax.broadcasted_iota(jnp.int32, (tq, tq), 0)
    qry_ix = lax.broadcasted_iota(jnp.int32, (tq, tq), 1)

    def scores(r0, c0):
        c0 = pl.multiple_of(c0, tq)
        return _dot(kx_scr[pl.ds(c0, tq), :], qxt_scr[:, pl.ds(r0, tq)])

    def softmax(s, f_q, m_old, l_old, p_ref):
        t_max = jnp.max(s, axis=0, keepdims=True) + f_q
        if m_old is None:
            m_new, alpha = t_max, None
        else:
            m_new = jnp.maximum(m_old, t_max)
            alpha = jnp.exp2(m_old - m_new)
        p = jnp.exp2(s - (m_new - f_q))
        p_sum = jnp.sum(p, axis=0, keepdims=True)
        p_ref[...] = p.astype(BF16)
        return m_new, (p_sum if m_old is None else alpha * l_old + p_sum), alpha

    def accumulate(c0, p_ref, alpha):
        c0 = pl.multiple_of(c0, tk)
        pv = _dot(vt_scr[:, pl.ds(c0, tk)], p_ref[...])
        acc_scr[...] = pv if alpha is None else alpha * acc_scr[...] + pv

    def open_block(r0):
        d = jnp.where(key_ix <= qry_ix, scores(r0, r0), NEG_INF)
        sa_scr[...] = scores(r0, 0)
        return d[0:tk], d[tk:]

    def diag_block(r0, f_q, d0, d1):
        m, l, _ = softmax(d0, f_q, None, None, pa_scr)
        accumulate(r0, pa_scr, None)
        m, l, alpha = softmax(d1, f_q, m, l, pb_scr)
        return r0 + tk, alpha, m, l

    def key_block(s_ref, c0, f_q, state):
        c_pend, alpha_pend, m, l = state
        accumulate(c_pend, pb_scr, alpha_pend)
        m, l, alpha_0 = softmax(s_ref[0:tk, :], f_q, m, l, pa_scr)
        accumulate(c0, pa_scr, alpha_0)
        m, l, alpha_1 = softmax(s_ref[tk:, :], f_q, m, l, pb_scr)
        return c0 + tk, alpha_1, m, l

    def close_block(r0, c_pend, alpha_pend, l):
        accumulate(c_pend, pb_scr, alpha_pend)
        o_t = acc_scr[...] * (1.0 / l)
        o_ref[pl.ds(pl.multiple_of(r0, tq), tq), :] = o_t.T.astype(BF16)

    def f_query(r0):
        return f_ref[:, pl.ds(r0, tq)] * LOG2E

    d0, d1 = open_block(0)
    c_pend, alpha_pend, _, l = diag_block(0, f_query(0), d0, d1)

    def q_body(qi, prev):
        r0 = pl.multiple_of(qi * tq, tq)
        f_q = f_query(r0)
        d0, d1 = open_block(r0)
        close_block(r0 - tq, *prev)
        state = diag_block(r0, f_q, d0, d1)

        def pair_body(pi, state):
            c0 = 2 * pi * tq
            sb_scr[...] = scores(r0, c0 + tq)
            state = key_block(sa_scr, c0, f_q, state)
            sa_scr[...] = scores(r0, c0 + 2 * tq)
            return key_block(sb_scr, c0 + tq, f_q, state)

        def last_body(_, state):
            return key_block(sa_scr, (qi - 1) * tq, f_q, state)

        state = lax.fori_loop(0, lax.shift_right_logical(qi, 1), pair_body, state)
        c_pend, alpha_pend, _, l = lax.fori_loop(0, qi & 1, last_body, state)
        return c_pend, alpha_pend, l

    n_q = s_len // tq
    last = lax.fori_loop(1, n_q, q_body, (c_pend, alpha_pend, l))
    close_block((n_q - 1) * tq, *last)


def _fox_prompt(zq, f_rows, *, n_batch, s_len):
    tq = min(FOX_TQ, s_len)
    tk = min(FOX_TK, tq)
    blk = (s_len, HEAD_DIM)
    scratch = [pltpu.VMEM((2 * HEAD_DIM, s_len), BF16), pltpu.VMEM((s_len, 2 * HEAD_DIM), BF16),
               pltpu.VMEM((HEAD_DIM, s_len), BF16), pltpu.VMEM((HEAD_DIM, tq), F32),
               pltpu.VMEM((tq, tq), F32), pltpu.VMEM((tq, tq), F32),
               pltpu.VMEM((tk, tq), BF16), pltpu.VMEM((tk, tq), BF16)]
    resident = 5 * _nbytes(blk, BF16) + 8 * _nbytes((tk, tq), F32)
    return pl.pallas_call(
        functools.partial(_fox_prompt_kernel, tq=tq, tk=tk),
        out_shape=jax.ShapeDtypeStruct((n_batch * s_len, D_HEADS), BF16),
        grid=(n_batch, N_HEADS),
        in_specs=[pl.BlockSpec(blk, lambda b, h: (b, h)),
                  pl.BlockSpec(blk, lambda b, h: (b, N_HEADS + h)),
                  pl.BlockSpec(blk, lambda b, h: (b, 2 * N_HEADS + h)),
                  pl.BlockSpec((None, 1, s_len), lambda b, h: (b * N_HEADS + h, 0, 0))],
        out_specs=pl.BlockSpec(blk, lambda b, h: (b, h)),
        scratch_shapes=scratch,
        compiler_params=pltpu.CompilerParams(
            dimension_semantics=("arbitrary", "arbitrary"),
            vmem_limit_bytes=_vmem_limit(4 * _nbytes(blk, BF16) + _nbytes((8, s_len), F32), resident)),
        name="fox_prompt",
    )(zq, zq, zq, f_rows)


BIAS_ROWS = 4 * REL_CLIP
BIAS_C0 = BAND_PAD - 2 * REL_CLIP


def _band_bias_kernel(tbl_ref, mbt_ref, mbs_ref, g_scr):
    h = pl.program_id(0)
    far = tbl_ref[h, 2 * REL_CLIP]
    diff = (lax.broadcasted_iota(jnp.int32, (BIAS_ROWS, LANES), 0) + BIAS_C0
            - lax.broadcasted_iota(jnp.int32, (BIAS_ROWS, LANES), 1))
    g_scr[...] = jnp.full((BIAS_ROWS, LANES), far, F32)
    for tix in range(1, 2 * REL_CLIP):
        d = BAND_PAD + REL_CLIP - tix
        lo = d - BIAS_C0
        a8 = max(0, lo // 8 * 8)
        b8 = min(BIAS_ROWS, -(-(lo + LANES) // 8) * 8)
        g_scr[a8:b8, :] = jnp.where(diff[a8:b8, :] == d, tbl_ref[h, tix], g_scr[a8:b8, :])
    g = g_scr[...]

    def far_rows(n):
        return jnp.full((n, LANES), far, F32)

    def masked(bias, q_ax, k_ax):
        rf = lax.broadcasted_iota(jnp.int32, bias.shape, q_ax) // CHUNK
        cf = lax.broadcasted_iota(jnp.int32, bias.shape, k_ax) // CHUNK
        visible = (cf >= rf) & (cf <= rf + N_PREV_CHUNKS)
        return jnp.where(visible, bias * LOG2E, NEG_INF)

    blocks = [jnp.concatenate([far_rows(BIAS_C0 + LANES * j), g[0:BAND_W - BIAS_C0 - LANES * j]], axis=0)
              for j in range(BAND_TQ // LANES)]
    mbt_ref[...] = masked(jnp.concatenate(blocks, axis=1), 1, 0)
    near = g[0:BAND_W_S - BIAS_C0].T
    mbs_ref[...] = masked(jnp.concatenate([jnp.full((BAND_TQ_S, BIAS_C0), far, F32), near], axis=1), 0, 1)


def _band_bias(table):
    assert BAND_TQ_S == LANES and BAND_W <= BIAS_C0 + BIAS_ROWS
    return pl.pallas_call(
        _band_bias_kernel,
        out_shape=(jax.ShapeDtypeStruct((N_HEADS, BAND_W, BAND_TQ), F32),
                   jax.ShapeDtypeStruct((N_HEADS, BAND_TQ_S, BAND_W_S), F32)),
        grid=(N_HEADS,),
        in_specs=[pl.BlockSpec(memory_space=pltpu.SMEM)],
        out_specs=(pl.BlockSpec((None, BAND_W, BAND_TQ), lambda h: (h, 0, 0)),
                   pl.BlockSpec((None, BAND_TQ_S, BAND_W_S), lambda h: (h, 0, 0))),
        scratch_shapes=[pltpu.VMEM((BIAS_ROWS, LANES), F32)],
        compiler_params=pltpu.CompilerParams(dimension_semantics=("arbitrary",)),
        name="band_bias",
    )(table.T)


def _band_prompt_kernel(q_ref, k_ref, v_ref, mb_ref, o_ref, qt_scr, kpad, vtpad, s_scr, pa_scr, pb_scr):
    s_len = q_ref.shape[0]
    kpad[0:BAND_PAD, :] = jnp.zeros((BAND_PAD, HEAD_DIM), BF16)
    kpad[BAND_PAD:, :] = k_ref[...]
    vtpad[:, 0:BAND_PAD] = jnp.zeros((HEAD_DIM, BAND_PAD), BF16)

    def setup(ci, carry):
        c0 = pl.multiple_of(ci * BAND_TQ, BAND_TQ)
        qt_scr[:, pl.ds(c0, BAND_TQ)] = _transpose_bf16(q_ref[pl.ds(c0, BAND_TQ), :])
        vtpad[:, pl.ds(pl.multiple_of(BAND_PAD + c0, BAND_TQ), BAND_TQ)] = _transpose_bf16(v_ref[pl.ds(c0, BAND_TQ), :])
        return carry

    n_tiles = s_len // BAND_TQ
    lax.fori_loop(0, n_tiles, setup, 0)
    key_ix = lax.broadcasted_iota(jnp.int32, (BAND_W, BAND_TQ), 0)

    def scores(ti):
        r0 = pl.multiple_of(ti * BAND_TQ, BAND_TQ)
        return _dot(kpad[pl.ds(r0, BAND_W), :], qt_scr[:, pl.ds(r0, BAND_TQ)]) + mb_ref[...]

    def softmax(s, p_ref):
        m = jnp.max(s, axis=0, keepdims=True)
        p = jnp.exp2(s - m)
        p_ref[...] = p.astype(BF16)
        return 1.0 / jnp.sum(p, axis=0, keepdims=True)

    def emit(ti, p_ref, r_den):
        r0 = pl.multiple_of(ti * BAND_TQ, BAND_TQ)
        o_t = _dot(vtpad[:, pl.ds(r0, BAND_W)], p_ref[...]) * r_den
        o_ref[pl.ds(r0, BAND_TQ), :] = o_t.T.astype(BF16)

    assert BAND_PAD // BAND_TQ == 2
    h0 = jnp.where(key_ix >= BAND_PAD, scores(0), NEG_INF)
    h1 = jnp.where(BAND_TQ + key_ix >= BAND_PAD, scores(1), NEG_INF)
    s_scr[...] = scores(2)
    r_0 = softmax(h0, pa_scr)
    emit(0, pa_scr, r_0)
    r_1 = softmax(h1, pb_scr)

    def pair(pi, r_pend):
        a = 2 + 2 * pi
        s_b = scores(a + 1)
        emit(a - 1, pb_scr, r_pend)
        r_a = softmax(s_scr[...], pa_scr)
        s_scr[...] = scores(jnp.minimum(a + 2, n_tiles - 1))
        emit(a, pa_scr, r_a)
        return softmax(s_b, pb_scr)

    r_last = lax.fori_loop(0, (n_tiles - 2) // 2, pair, r_1)
    emit(n_tiles - 1, pb_scr, r_last)


def _band_prompt(zq, mbt, *, n_batch, s_len):
    blk = (s_len, HEAD_DIM)
    off = 3 * N_HEADS
    assert (s_len // BAND_TQ - BAND_PAD // BAND_TQ) % 2 == 0
    scratch = [pltpu.VMEM((HEAD_DIM, s_len), BF16), pltpu.VMEM((BAND_PAD + s_len, HEAD_DIM), BF16),
               pltpu.VMEM((HEAD_DIM, BAND_PAD + s_len), BF16), pltpu.VMEM((BAND_W, BAND_TQ), F32),
               pltpu.VMEM((BAND_W, BAND_TQ), BF16), pltpu.VMEM((BAND_W, BAND_TQ), BF16)]
    resident = 3 * _nbytes((BAND_PAD + s_len, HEAD_DIM), BF16) + 8 * _nbytes((BAND_W, BAND_TQ), F32)
    return pl.pallas_call(
        _band_prompt_kernel,
        out_shape=jax.ShapeDtypeStruct((n_batch * s_len, D_HEADS), BF16),
        grid=(n_batch, N_HEADS),
        in_specs=[pl.BlockSpec(blk, lambda b, h: (b, off + h)),
                  pl.BlockSpec(blk, lambda b, h: (b, off + N_HEADS + h)),
                  pl.BlockSpec(blk, lambda b, h: (b, off + 2 * N_HEADS + h)),
                  pl.BlockSpec((None, BAND_W, BAND_TQ), lambda b, h: (h, 0, 0))],
        out_specs=pl.BlockSpec(blk, lambda b, h: (b, h)),
        scratch_shapes=scratch,
        compiler_params=pltpu.CompilerParams(
            dimension_semantics=("arbitrary", "arbitrary"),
            vmem_limit_bytes=_vmem_limit(4 * _nbytes(blk, BF16) + _nbytes((BAND_W, BAND_TQ), F32), resident)),
        name="band_prompt",
    )(zq, zq, zq, mbt)


def _fox_sample_kernel(q_ref, kn_ref, vn_ref, ck_ref, cv_ref, f_ref, o_ref,
                       m_scr, l_scr, acc_scr, fq_scr, *, past, tk):
    kt = pl.program_id(1)
    t_new = q_ref.shape[0]

    def heads(h):
        return slice(h * HEAD_DIM, (h + 1) * HEAD_DIM)

    @pl.when(kt == 0)
    def _():
        for h in range(N_HEADS):
            fq_scr[h] = _col_bcast(f_ref[h:h + 1, past:past + t_new] * LOG2E, t_new)
        m_scr[...] = jnp.full(m_scr.shape, NEG_INF, F32)
        l_scr[...] = jnp.zeros(l_scr.shape, F32)
        acc_scr[...] = jnp.zeros(acc_scr.shape, F32)

    def update(h, s, v):
        m_old = m_scr[h]
        m_new = jnp.maximum(m_old, jnp.max(s, axis=1, keepdims=True))
        alpha = jnp.exp2(m_old - m_new)
        p = jnp.exp2(s - m_new)
        l_scr[h] = alpha * l_scr[h] + jnp.sum(p, axis=1, keepdims=True)
        acc_scr[h] = alpha * acc_scr[h] + _dot(p.astype(BF16), v)
        m_scr[h] = m_new

    c0 = pl.multiple_of(kt * tk, tk)
    scores = []
    for h in range(N_HEADS):
        k = ck_ref[pl.ds(h, tk, stride=N_HEADS), :].astype(BF16)
        scores.append(_dot_nt(q_ref[:, heads(h)], k) + _tile_lanes(fq_scr[h], tk)
                      - f_ref[h:h + 1, pl.ds(c0, tk)] * LOG2E)
    for h in range(N_HEADS):
        update(h, scores[h], cv_ref[pl.ds(h, tk, stride=N_HEADS), :].astype(BF16))

    @pl.when(kt == pl.num_programs(1) - 1)
    def _():
        r = lax.broadcasted_iota(jnp.int32, (t_new, t_new), 0)
        c = lax.broadcasted_iota(jnp.int32, (t_new, t_new), 1)
        for h in range(N_HEADS):
            q = q_ref[:, heads(h)]
            f_new = f_ref[h:h + 1, past:past + t_new] * LOG2E
            s = _dot_nt(q, kn_ref[:, heads(h)]) + fq_scr[h][:, :t_new] - f_new
            s = jnp.where(c <= r, s, NEG_INF)
            update(h, s, vn_ref[:, heads(h)])
            o_ref[:, heads(h)] = (acc_scr[h] / l_scr[h]).astype(BF16)


def _fox_sample(zq, cache_k, cache_v, f_all, *, n_batch, t_new, past):
    tk = min(2048, past)
    lp = f_all.shape[-1]
    blk = (t_new, D_HEADS)
    cblk = (None, tk * N_HEADS, HEAD_DIM)
    return pl.pallas_call(
        functools.partial(_fox_sample_kernel, past=past, tk=tk),
        out_shape=jax.ShapeDtypeStruct((n_batch * t_new, D_HEADS), BF16),
        grid=(n_batch, past // tk),
        in_specs=[pl.BlockSpec(blk, lambda b, k: (b, 0)),
                  pl.BlockSpec(blk, lambda b, k: (b, 1)),
                  pl.BlockSpec(blk, lambda b, k: (b, 2)),
                  pl.BlockSpec(cblk, lambda b, k: (b, k, 0)),
                  pl.BlockSpec(cblk, lambda b, k: (b, k, 0)),
                  pl.BlockSpec((None, N_HEADS, lp), lambda b, k: (b, 0, 0))],
        out_specs=pl.BlockSpec(blk, lambda b, k: (b, 0)),
        scratch_shapes=[pltpu.VMEM((N_HEADS, t_new, 1), F32), pltpu.VMEM((N_HEADS, t_new, 1), F32),
                        pltpu.VMEM((N_HEADS, t_new, HEAD_DIM), F32),
                        pltpu.VMEM((N_HEADS, t_new, LANES), F32)],
        compiler_params=pltpu.CompilerParams(
            dimension_semantics=("arbitrary", "arbitrary"),
            vmem_limit_bytes=_vmem_limit(2 * _nbytes((tk, D_HEADS), F32) + 4 * _nbytes(blk, BF16)
                                         + _nbytes((N_HEADS, lp), F32), 8 * MIB)),
        name="fox_sample",
    )(zq, zq, zq, cache_k, cache_v, f_all)


def _band_sample_kernel(q_ref, kn_ref, vn_ref, ck_ref, cv_ref, mb_ref, o_ref):
    t_new = q_ref.shape[0]
    lb = ck_ref.shape[0] // N_HEADS
    scores = []
    for h in range(N_HEADS):
        hs = slice(h * HEAD_DIM, (h + 1) * HEAD_DIM)
        q = q_ref[:, hs]
        scores.append((_dot_nt(q, ck_ref[pl.ds(h, lb, stride=N_HEADS), :].astype(BF16)) + mb_ref[h, :, 0:lb],
                       _dot_nt(q, kn_ref[:, hs]) + mb_ref[h, :, lb:lb + t_new]))
    for h in range(N_HEADS):
        hs = slice(h * HEAD_DIM, (h + 1) * HEAD_DIM)
        s1, s2 = scores[h]
        m = jnp.maximum(jnp.max(s1, axis=1, keepdims=True), jnp.max(s2, axis=1, keepdims=True))
        p1 = jnp.exp2(s1 - m)
        p2 = jnp.exp2(s2 - m)
        den = jnp.sum(p1, axis=1, keepdims=True) + jnp.sum(p2, axis=1, keepdims=True)
        o = _dot(p1.astype(BF16), cv_ref[pl.ds(h, lb, stride=N_HEADS), :].astype(BF16)) + _dot(p2.astype(BF16), vn_ref[:, hs])
        o_ref[:, hs] = (o / den).astype(BF16)


def _band_sample(zq, cache_k, cache_v, mb, *, n_batch, t_new):
    lb = cache_k.shape[1] // N_HEADS
    blk = (t_new, D_HEADS)
    cblk = (None, lb * N_HEADS, HEAD_DIM)
    return pl.pallas_call(
        _band_sample_kernel,
        out_shape=jax.ShapeDtypeStruct((n_batch * t_new, D_HEADS), BF16),
        grid=(n_batch,),
        in_specs=[pl.BlockSpec(blk, lambda b: (b, 3)),
                  pl.BlockSpec(blk, lambda b: (b, 4)),
                  pl.BlockSpec(blk, lambda b: (b, 5)),
                  pl.BlockSpec(cblk, lambda b: (b, 0, 0)),
                  pl.BlockSpec(cblk, lambda b: (b, 0, 0)),
                  pl.BlockSpec((N_HEADS, t_new, BAND_W_S), lambda b: (0, 0, 0))],
        out_specs=pl.BlockSpec(blk, lambda b: (b, 0)),
        compiler_params=pltpu.CompilerParams(
            dimension_semantics=("arbitrary",),
            vmem_limit_bytes=_vmem_limit(2 * _nbytes((lb, D_HEADS), F32) + 4 * _nbytes(blk, BF16),
                                         _nbytes((N_HEADS, t_new, BAND_W_S), F32) + 4 * MIB)),
        name="band_sample",
    )(zq, zq, zq, cache_k, cache_v, mb)


def _merge_kernel(oa_ref, ob_ref, za_ref, zb_ref, x_ref, gt_ref, woa_ref, wob_ref, wout_ref, o_ref, mix_scr, *, nb, t):
    d = x_ref.shape[1]
    half = d // 2
    for c in range(2):
        cols = slice(c * half, (c + 1) * half)
        a = _dot(oa_ref[...], woa_ref[:, cols])
        b = _dot(ob_ref[...], wob_ref[:, cols])
        mix = _sigmoid(za_ref[:, cols].astype(F32)) * a + _sigmoid(zb_ref[:, cols].astype(F32)) * b
        mix_scr[:, cols] = mix.astype(BF16)
    for c in range(2):
        cols = slice(c * half, (c + 1) * half)
        y = _dot(mix_scr[...], wout_ref[:, cols])
        x = x_ref[:, cols]
        if nb == 1:
            o_ref[:, cols] = x + gt_ref[0][:, cols] * y
        else:
            gt = gt_ref[...][:, :, cols]
            o_ref[:, cols] = (x.reshape(nb, t, half) + gt * y.reshape(nb, t, half)).reshape(x.shape)


def _merge(oa, ob, zq, x2d, gate, w_oa, w_ob, w_out, *, nb, t, tiles_per_batch):
    m, d = x2d.shape
    tm = nb * t
    za_blk = Z_COLS // d
    if nb == 1:
        mod_map = lambda i: (i // tiles_per_batch, 0, 0)
    else:
        mod_map = lambda i: (i, 0, 0)
    const = lambda i: (0, 0)
    pipelined = (2 * _nbytes((tm, D_HEADS), BF16) + 2 * _nbytes((tm, d), BF16) + 2 * _nbytes((tm, d), F32))
    resident = (2 * _nbytes((D_HEADS, d), BF16) + _nbytes((d, d), BF16) + _nbytes((tm, d), BF16)
                + 2 * _nbytes((tm, d), F32))
    return pl.pallas_call(
        functools.partial(_merge_kernel, nb=nb, t=t),
        out_shape=jax.ShapeDtypeStruct((m, d), F32),
        grid=(m // tm,),
        in_specs=[pl.BlockSpec((tm, D_HEADS), lambda i: (i, 0)),
                  pl.BlockSpec((tm, D_HEADS), lambda i: (i, 0)),
                  pl.BlockSpec((tm, d), lambda i: (i, za_blk)),
                  pl.BlockSpec((tm, d), lambda i: (i, za_blk + 1)),
                  pl.BlockSpec((tm, d), lambda i: (i, 0)),
                  pl.BlockSpec((nb, 1, d), mod_map),
                  pl.BlockSpec((D_HEADS, d), const, pipeline_mode=pl.Buffered(1)),
                  pl.BlockSpec((D_HEADS, d), const, pipeline_mode=pl.Buffered(1)),
                  pl.BlockSpec((d, d), const, pipeline_mode=pl.Buffered(1))],
        out_specs=pl.BlockSpec((tm, d), lambda i: (i, 0)),
        scratch_shapes=[pltpu.VMEM((tm, d), BF16)],
        compiler_params=pltpu.CompilerParams(
            dimension_semantics=("arbitrary",),
            vmem_limit_bytes=_vmem_limit(pipelined, resident)),
        name="merge_out",
    )(oa, ob, zq, zq, x2d, gate, w_oa, w_ob, w_out)


def _ffn_kernel(x_ref, sh_ref, sc_ref, gt_ref, g_ref, gf_ref, wg_ref, wu_ref, wd_ref, o_ref,
                h_scr, acc_scr, *, nb, t):
    f = pl.program_id(1)

    @pl.when(f == 0)
    def _():
        h_scr[...] = _mod_norm(x_ref[...], g_ref, sc_ref, sh_ref, nb, t).astype(BF16)
        acc_scr[...] = jnp.zeros(acc_scr.shape, F32)

    h = h_scr[...]
    gate = _dot(h, wg_ref[...])
    up = _dot(h, wu_ref[...])
    act = (gate * _sigmoid(gate) * up).astype(BF16)
    acc_scr[...] += _dot(act, wd_ref[...])

    @pl.when(f == pl.num_programs(1) - 1)
    def _():
        x = x_ref[...]
        y, gt = _per_batch(acc_scr[...], gt_ref, nb, t)
        x3, _ = _per_batch(x, gt_ref, nb, t)
        x2 = (x3 + gt * y).reshape(x.shape)
        o_ref[...] = _rms_scale(x2) * gf_ref[...]


def _ffn(x2d, shift, scale, gate, g_ffn, g_final, w_gate, w_up, w_down, *, nb, t, tiles_per_batch):
    m, d = x2d.shape
    dff = w_gate.shape[1]
    tm = nb * t
    tf = 512
    if nb == 1:
        mod_map = lambda i, f: (i // tiles_per_batch, 0, 0)
    else:
        mod_map = lambda i, f: (i, 0, 0)
    pipelined = 2 * _nbytes((tm, d), F32) + 3 * _nbytes((d, tf), BF16)
    resident = _nbytes((tm, d), BF16) + _nbytes((tm, d), F32) + 3 * _nbytes((tm, tf), F32) + 2 * _nbytes((tm, d), F32)
    return pl.pallas_call(
        functools.partial(_ffn_kernel, nb=nb, t=t),
        out_shape=jax.ShapeDtypeStruct((m, d), F32),
        grid=(m // tm, dff // tf),
        in_specs=[pl.BlockSpec((tm, d), lambda i, f: (i, 0)),
                  pl.BlockSpec((nb, 1, d), mod_map),
                  pl.BlockSpec((nb, 1, d), mod_map),
                  pl.BlockSpec((nb, 1, d), mod_map),
                  pl.BlockSpec((1, d), lambda i, f: (0, 0)),
                  pl.BlockSpec((1, d), lambda i, f: (0, 0)),
                  pl.BlockSpec((d, tf), lambda i, f: (0, f)),
                  pl.BlockSpec((d, tf), lambda i, f: (0, f)),
                  pl.BlockSpec((tf, d), lambda i, f: (f, 0))],
        out_specs=pl.BlockSpec((tm, d), lambda i, f: (i, 0)),
        scratch_shapes=[pltpu.VMEM((tm, d), BF16), pltpu.VMEM((tm, d), F32)],
        compiler_params=pltpu.CompilerParams(
            dimension_semantics=("arbitrary", "arbitrary"),
            vmem_limit_bytes=_vmem_limit(pipelined, resident)),
        name="ffn_final",
    )(x2d, shift, scale, gate, g_ffn, g_final, w_gate, w_up, w_down)


def _row_tile(s_len, cap):
    tile = min(cap, s_len)
    assert s_len % tile == 0
    return tile


def kernel(x_prompt, x_sample, cache_fox_k, cache_fox_v, cache_fox_logf, cache_band_k, cache_band_v,
           c_prompt, c_sample, w_ada, b_ada, g_mix, w_in, b_f, rel_bias, w_oa, w_ob, w_out,
           g_ffn, w_gate, w_up, w_down, g_final):
    n_b, s_len, d = x_prompt.shape
    n_db, t_new, _ = x_sample.shape
    past = cache_fox_k.shape[2]
    lb = cache_band_k.shape[2]
    assert w_ada.shape[0] == 1, "single-layer trunk"
    assert lb == BAND_PAD and s_len >= BAND_PAD and s_len % FOX_TQ == 0
    assert Z_COLS % d == 0 and t_new % 8 == 0 and t_new + lb <= BAND_W_S and t_new <= CHUNK
    layer = 0
    q_scale = LOG2E / math.sqrt(HEAD_DIM)

    w_l = w_in[layer]
    w_main = jnp.concatenate([w_l[:, :3 * D_HEADS], w_l[:, 3 * D_HEADS + N_HEADS:]], axis=1).astype(BF16)
    w_f = jnp.pad(w_l[:, 3 * D_HEADS:3 * D_HEADS + N_HEADS], ((0, 0), (0, LANES - N_HEADS))).astype(BF16)
    n_main = w_main.shape[1]
    col = jnp.arange(n_main)
    is_q = (col < D_HEADS) | ((col >= 3 * D_HEADS) & (col < 4 * D_HEADS))
    col_scale = jnp.where(is_q, q_scale, 1.0).astype(F32).reshape(1, n_main)
    b_f2 = b_f[layer].reshape(1, N_HEADS)
    g_mix2 = g_mix[layer].reshape(1, d)
    g_ffn2 = g_ffn[layer].reshape(1, d)
    g_fin2 = g_final.reshape(1, d)
    w_oa_b, w_ob_b, w_out_b = w_oa[layer].astype(BF16), w_ob[layer].astype(BF16), w_out[layer].astype(BF16)
    w_gate_b, w_up_b, w_down_b = w_gate[layer].astype(BF16), w_up[layer].astype(BF16), w_down[layer].astype(BF16)

    mods = _ada(jnp.concatenate([c_prompt, c_sample], axis=0), w_ada[layer], b_ada[layer])

    def chunks(rows):
        return [rows[:, k * d:(k + 1) * d][:, None, :] for k in range(6)]

    sh1p, sc1p, gt1p, sh2p, sc2p, gt2p = chunks(mods[:n_b])
    sh1s, sc1s, gt1s, sh2s, sc2s, gt2s = chunks(mods[n_b:])
    mbt, mbs = _band_bias(rel_bias[layer])

    xp = x_prompt.reshape(n_b * s_len, d)
    tm = _row_tile(s_len, 1024)
    zq, ka, va, kb, vb, lf = _mixer_in(xp, sh1p, sc1p, g_mix2, w_main, col_scale, w_f, b_f2,
                                       nb=1, t=tm, tiles_per_batch=s_len // tm, tail=BAND_PAD)
    lf_rows = lf.reshape(n_b, s_len, N_HEADS).transpose(0, 2, 1).reshape(n_b * N_HEADS, s_len)
    f_rows = _cumsum_rows(lf_rows).reshape(n_b * N_HEADS, 1, s_len)
    oa = _fox_prompt(zq, f_rows, n_batch=n_b, s_len=s_len)
    ob = _band_prompt(zq, mbt, n_batch=n_b, s_len=s_len)
    tm = _row_tile(s_len, 512)
    x1 = _merge(oa, ob, zq, xp, gt1p, w_oa_b, w_ob_b, w_out_b, nb=1, t=tm, tiles_per_batch=s_len // tm)
    tm = _row_tile(s_len, 512)
    y_prompt = _ffn(x1, sh2p, sc2p, gt2p, g_ffn2, g_fin2, w_gate_b, w_up_b, w_down_b,
                    nb=1, t=tm, tiles_per_batch=s_len // tm).reshape(n_b, s_len, d)
    fox_shape = (1, n_b, s_len, N_HEADS, HEAD_DIM)
    band_shape = (1, n_b, BAND_PAD, N_HEADS, HEAD_DIM)
    prompt_out = (ka.reshape(fox_shape), va.reshape(fox_shape), lf.reshape(1, n_b, s_len, N_HEADS),
                  kb.reshape(band_shape), vb.reshape(band_shape))

    xs = x_sample.reshape(n_db * t_new, d)
    zqs, kas, vas, kbs, vbs, lfs = _mixer_in(xs, sh1s, sc1s, g_mix2, w_main, col_scale, w_f, b_f2,
                                             nb=n_db, t=t_new, tiles_per_batch=1, tail=n_db * t_new)
    lp = -(-(past + t_new) // 2048) * 2048
    lf_all = jnp.concatenate([cache_fox_logf[layer].transpose(0, 2, 1),
                              lfs.reshape(n_db, t_new, N_HEADS).transpose(0, 2, 1),
                              jnp.zeros((n_db, N_HEADS, lp - past - t_new), F32)], axis=-1)
    f_all = _cumsum_rows(lf_all.reshape(n_db * N_HEADS, lp)).reshape(n_db, N_HEADS, lp)
    oas = _fox_sample(zqs, cache_fox_k[layer].reshape(n_db, past * N_HEADS, HEAD_DIM),
                      cache_fox_v[layer].reshape(n_db, past * N_HEADS, HEAD_DIM), f_all,
                      n_batch=n_db, t_new=t_new, past=past)
    obs = _band_sample(zqs, cache_band_k[layer].reshape(n_db, lb * N_HEADS, HEAD_DIM),
                       cache_band_v[layer].reshape(n_db, lb * N_HEADS, HEAD_DIM), mbs, n_batch=n_db, t_new=t_new)
    x1s = _merge(oas, obs, zqs, xs, gt1s, w_oa_b, w_ob_b, w_out_b, nb=n_db, t=t_new, tiles_per_batch=1)
    y_sample = _ffn(x1s, sh2s, sc2s, gt2s, g_ffn2, g_fin2, w_gate_b, w_up_b, w_down_b,
                    nb=n_db, t=t_new, tiles_per_batch=1).reshape(n_db, t_new, d)
    s_shape = (1, n_db, t_new, N_HEADS, HEAD_DIM)
    sample_out = (kas.reshape(s_shape), vas.reshape(s_shape), lfs.reshape(1, n_db, t_new, N_HEADS),
                  kbs.reshape(s_shape), vbs.reshape(s_shape))

    return (y_prompt, y_sample) + prompt_out + sample_out
```
